```python
import jax, jax.numpy as jnp
from jax import lax
import numpy as np

D_MODEL = 1024
BATCH = 2
SEQ = 16384
DEPTH = 2
DEC_BATCH = 2
DEC_SEQ = 8192
PAST_LEN = 128

HEAD_DIM = 64
A_Q_HEADS = 8
A_KV_HEADS = 2
B_Q_HEADS = 8
B_KV_HEADS = 2
A_WIDTH = A_Q_HEADS * HEAD_DIM
B_WIDTH = B_Q_HEADS * HEAD_DIM
MIX_WIDTH = A_WIDTH + B_WIDTH
A_KV_WIDTH = A_KV_HEADS * HEAD_DIM
B_KV_WIDTH = B_KV_HEADS * HEAD_DIM
IN_WIDTH = A_WIDTH + 2 * A_KV_WIDTH + B_WIDTH + 2 * B_KV_WIDTH
D_FF = 2816
GRID_W = 64
ROPE_THETA = 10000.0
WINDOW = 128
Q_BLOCK = 128
NORM_EPS = 1e-6
FFN_RESID = 0.5
ATTN_SCALE = HEAD_DIM ** -0.5

kernel_name = "hybrid_axial_window_macaron_encoder"


def rms_norm(x, g):
    xf = x.astype(jnp.float32)
    y = xf * lax.rsqrt(jnp.mean(xf * xf, axis=-1, keepdims=True) + NORM_EPS)
    return (y * g.astype(jnp.float32)).astype(x.dtype)


def swiglu(x, w_gate, w_up, w_down):
    return (jax.nn.silu(x @ w_gate) * (x @ w_up)) @ w_down


def axial_rope_tables(seq_len):
    n_rows = seq_len // GRID_W
    row = jnp.repeat(jnp.arange(n_rows, dtype=jnp.float32), GRID_W)
    col = jnp.tile(jnp.arange(GRID_W, dtype=jnp.float32), n_rows)
    n_freq = HEAD_DIM // 4
    inv_freq = ROPE_THETA ** (-jnp.arange(n_freq, dtype=jnp.float32) / n_freq)
    ang = jnp.concatenate([row[:, None] * inv_freq[None, :], col[:, None] * inv_freq[None, :]], axis=-1)
    return jnp.cos(ang), jnp.sin(ang)


def apply_rope(x, cos, sin):
    b, s, h, d = x.shape
    xf = x.astype(jnp.float32).reshape(b, s, h, d // 2, 2)
    x0, x1 = xf[..., 0], xf[..., 1]
    c = cos[None, :, None, :]
    sn = sin[None, :, None, :]
    out = jnp.stack([x0 * c - x1 * sn, x0 * sn + x1 * c], axis=-1)
    return out.reshape(b, s, h, d).astype(x.dtype)


def alibi_slopes():
    return 2.0 ** (-8.0 * jnp.arange(1, B_Q_HEADS + 1, dtype=jnp.float32) / B_Q_HEADS)


def global_attention(q, k, v):
    b, s, hq, d = q.shape
    hkv = k.shape[2]
    g = hq // hkv
    nblk = s // Q_BLOCK
    qb = (q * ATTN_SCALE).reshape(b, nblk, Q_BLOCK, hkv, g, d).transpose(1, 0, 2, 3, 4, 5)

    def one_block(qi):
        sc = jnp.einsum('bqhgd,bshd->bhgqs', qi, k, preferred_element_type=jnp.float32)
        p = jax.nn.softmax(sc, axis=-1)
        return jnp.einsum('bhgqs,bshd->bqhgd', p.astype(v.dtype), v)

    o = lax.map(one_block, qb)
    return o.transpose(1, 0, 2, 3, 4, 5).reshape(b, s, hq * d)


def window_attention(q, k, v, sink, slopes):
    b, s, hq, d = q.shape
    hkv = k.shape[2]
    g = hq // hkv
    nblk = s // Q_BLOCK
    pad = ((0, 0), (WINDOW, WINDOW), (0, 0), (0, 0))
    kp = jnp.pad(k, pad).reshape(b, nblk + 2, Q_BLOCK, hkv, d)
    vp = jnp.pad(v, pad).reshape(b, nblk + 2, Q_BLOCK, hkv, d)
    kband = jnp.concatenate([kp[:, :-2], kp[:, 1:-1], kp[:, 2:]], axis=2)
    vband = jnp.concatenate([vp[:, :-2], vp[:, 1:-1], vp[:, 2:]], axis=2)
    qb = (q * ATTN_SCALE).reshape(b, nblk, Q_BLOCK, hkv, g, d)
    sc = jnp.einsum('bnqhgd,bnkhd->bnhgqk', qb, kband, preferred_element_type=jnp.float32)
    blk = jnp.arange(nblk)[:, None] * Q_BLOCK
    qpos = blk + jnp.arange(Q_BLOCK)[None, :]
    kpos = blk - WINDOW + jnp.arange(3 * Q_BLOCK)[None, :]
    dist = jnp.abs(qpos[:, :, None] - kpos[:, None, :])
    valid = (dist <= WINDOW) & (kpos >= 0)[:, None, :] & (kpos < s)[:, None, :]
    m_h = slopes.reshape(hkv, g)[None, None, :, :, None, None]
    sc = sc - m_h * dist.astype(jnp.float32)[None, :, None, None, :, :]
    sc = jnp.where(valid[None, :, None, None, :, :], sc, -jnp.inf)
    sink_l = sink.astype(jnp.float32).reshape(hkv, g)[None, None, :, :, None, None]
    mx = jnp.maximum(jnp.max(sc, axis=-1, keepdims=True), sink_l)
    e = jnp.exp(sc - mx)
    p = e / (jnp.sum(e, axis=-1, keepdims=True) + jnp.exp(sink_l - mx))
    o = jnp.einsum('bnhgqk,bnkhd->bnqhgd', p.astype(v.dtype), vband)
    return o.reshape(b, s, hq * d)


def token_mixer(h, w_in, a_q_norm, a_k_norm, b_sink, a_out_norm, b_out_norm, w_out, cos, sin, slopes):
    b, s, _ = h.shape
    proj = h @ w_in
    offs = np.cumsum([A_WIDTH, A_KV_WIDTH, A_KV_WIDTH, B_WIDTH, B_KV_WIDTH]).tolist()
    qa, ka, va, qb, kb, vb = jnp.split(proj, offs, axis=-1)
    qa = qa.reshape(b, s, A_Q_HEADS, HEAD_DIM)
    ka = ka.reshape(b, s, A_KV_HEADS, HEAD_DIM)
    va = va.reshape(b, s, A_KV_HEADS, HEAD_DIM)
    qa = apply_rope(rms_norm(qa, a_q_norm), cos, sin)
    ka = apply_rope(rms_norm(ka, a_k_norm), cos, sin)
    oa = global_attention(qa, ka, va)
    qb = qb.reshape(b, s, B_Q_HEADS, HEAD_DIM)
    kb = kb.reshape(b, s, B_KV_HEADS, HEAD_DIM)
    vb = vb.reshape(b, s, B_KV_HEADS, HEAD_DIM)
    ob = window_attention(qb, kb, vb, b_sink, slopes)
    merged = jnp.concatenate([rms_norm(oa, a_out_norm), rms_norm(ob, b_out_norm)], axis=-1)
    return merged @ w_out


def encoder_trunk(x, ffn1_pre, ffn1_post, ffn1_w_gate, ffn1_w_up, ffn1_w_down,
                  mix_pre, mix_post, w_in, a_q_norm, a_k_norm, b_sink, a_out_norm, b_out_norm, w_out,
                  ffn2_pre, ffn2_post, ffn2_w_gate, ffn2_w_up, ffn2_w_down):
    seq_len = x.shape[1]
    cos, sin = axial_rope_tables(seq_len)
    slopes = alibi_slopes()
    for l in range(DEPTH):
        h = swiglu(rms_norm(x, ffn1_pre[l]), ffn1_w_gate[l], ffn1_w_up[l], ffn1_w_down[l])
        x = x + FFN_RESID * rms_norm(h, ffn1_post[l])
        h = token_mixer(rms_norm(x, mix_pre[l]), w_in[l], a_q_norm[l], a_k_norm[l], b_sink[l],
                        a_out_norm[l], b_out_norm[l], w_out[l], cos, sin, slopes)
        x = x + rms_norm(h, mix_post[l])
        h = swiglu(rms_norm(x, ffn2_pre[l]), ffn2_w_gate[l], ffn2_w_up[l], ffn2_w_down[l])
        x = x + FFN_RESID * rms_norm(h, ffn2_post[l])
    return x


def setup_inputs(seed: int = 0) -> dict:
    key = jax.random.key(seed)
    ks = jax.random.split(key, 24)
    f32 = jnp.float32

    def nrm(k, shape, scale):
        return jax.random.normal(k, shape, f32) * scale

    def gain(k, n):
        return jnp.ones((DEPTH, n), f32) + 0.05 * jax.random.normal(k, (DEPTH, n), f32)

    return {
        "x_prompt": jax.random.normal(ks[0], (BATCH, SEQ, D_MODEL), f32),
        "x_sample": jax.random.normal(ks[1], (DEC_BATCH, DEC_SEQ, D_MODEL), f32),
        "ffn1_pre": gain(ks[2], D_MODEL),
        "ffn1_post": gain(ks[3], D_MODEL),
        "ffn1_w_gate": nrm(ks[4], (DEPTH, D_MODEL, D_FF), D_MODEL ** -0.5),
        "ffn1_w_up": nrm(ks[5], (DEPTH, D_MODEL, D_FF), D_MODEL ** -0.5),
        "ffn1_w_down": nrm(ks[6], (DEPTH, D_FF, D_MODEL), D_FF ** -0.5),
        "mix_pre": gain(ks[7], D_MODEL),
        "mix_post": gain(ks[8], D_MODEL),
        "w_in": nrm(ks[9], (DEPTH, D_MODEL, IN_WIDTH), D_MODEL ** -0.5),
        "a_q_norm": gain(ks[10], HEAD_DIM),
        "a_k_norm": gain(ks[11], HEAD_DIM),
        "b_sink": nrm(ks[12], (DEPTH, B_Q_HEADS), 0.5),
        "a_out_norm": gain(ks[13], A_WIDTH),
        "b_out_norm": gain(ks[14], B_WIDTH),
        "w_out": nrm(ks[15], (DEPTH, MIX_WIDTH, D_MODEL), MIX_WIDTH ** -0.5),
        "ffn2_pre": gain(ks[16], D_MODEL),
        "ffn2_post": gain(ks[17], D_MODEL),
        "ffn2_w_gate": nrm(ks[18], (DEPTH, D_MODEL, D_FF), D_MODEL ** -0.5),
        "ffn2_w_up": nrm(ks[19], (DEPTH, D_MODEL, D_FF), D_MODEL ** -0.5),
        "ffn2_w_down": nrm(ks[20], (DEPTH, D_FF, D_MODEL), D_FF ** -0.5),
    }


def reference(x_prompt, x_sample, ffn1_pre, ffn1_post, ffn1_w_gate, ffn1_w_up, ffn1_w_down,
              mix_pre, mix_post, w_in, a_q_norm, a_k_norm, b_sink, a_out_norm, b_out_norm, w_out,
              ffn2_pre, ffn2_post, ffn2_w_gate, ffn2_w_up, ffn2_w_down):
    y_prompt = encoder_trunk(x_prompt, ffn1_pre, ffn1_post, ffn1_w_gate, ffn1_w_up, ffn1_w_down,
                             mix_pre, mix_post, w_in, a_q_norm, a_k_norm, b_sink, a_out_norm, b_out_norm, w_out,
                             ffn2_pre, ffn2_post, ffn2_w_gate, ffn2_w_up, ffn2_w_down)
    y_sample = encoder_trunk(x_sample, ffn1_pre, ffn1_post, ffn1_w_gate, ffn1_w_up, ffn1_w_down,
                             mix_pre, mix_post, w_in, a_q_norm, a_k_norm, b_sink, a_out_norm, b_out_norm, w_out,
                             ffn2_pre, ffn2_post, ffn2_w_gate, ffn2_w_up, ffn2_w_down)
    return (y_prompt, y_sample)
```

```python
import functools
import math

import jax
import jax.numpy as jnp
import numpy as np
from jax import lax
from jax.experimental import pallas as pl
from jax.experimental.pallas import tpu as pltpu

D_MODEL = 1024
HEAD_DIM = 64
HALF_DIM = HEAD_DIM // 2
Q_HEADS = 8
KV_HEADS = 2
HEADS_PER_KV = Q_HEADS // KV_HEADS
GROUP_WIDTH = Q_HEADS * HEAD_DIM
KV_WIDTH = KV_HEADS * HEAD_DIM
Q_PER_KV_ROWS = HEADS_PER_KV * HEAD_DIM
IN_WIDTH = 2 * (GROUP_WIDTH + 2 * KV_WIDTH)
D_FF = 2816
GRID_W = 64
ROPE_THETA = 10000.0
WINDOW = 128
NORM_EPS = 1e-6
FFN_RESID = 0.5
LOG2E = math.log2(math.e)
Q_SCALE = HEAD_DIM ** -0.5 * LOG2E
NEG_BIG = -1e30

FF_CHUNK = 256
N_FF_CHUNKS = D_FF // FF_CHUNK
TOKEN_TILE = 512
V_CHUNK = TOKEN_TILE
Q_TILE = 256
ONES_ROWS = 16
WQ_TILE = 512
N_SUB = WQ_TILE // WINDOW
BAND = 3 * WINDOW

VMEM_LIMIT = 56 * 1024 * 1024


def _rms_rows(x, gain_row):
    ms = jnp.mean(x * x, axis=-1, keepdims=True)
    return x * lax.rsqrt(ms + NORM_EPS) * gain_row


def _rms_cols(xt, gain_col):
    ms = jnp.mean(xt * xt, axis=0, keepdims=True)
    return xt * lax.rsqrt(ms + NORM_EPS) * gain_col


def _ffn_kernel(x_ref, pre_ref, post_ref, wgu_ref, wd_ref, o_ref, xn_ref, acc_ref):
    x = x_ref[...]
    xn_ref[...] = _rms_rows(x, pre_ref[...]).astype(jnp.bfloat16)
    acc_ref[...] = jnp.zeros_like(acc_ref)

    def chunk(c, carry):
        gu = jnp.dot(xn_ref[...], wgu_ref[c], preferred_element_type=jnp.float32)
        g = gu[:, :FF_CHUNK]
        u = gu[:, FF_CHUNK:]
        h = (g * jax.nn.sigmoid(g) * u).astype(jnp.bfloat16)
        acc_ref[...] += jnp.dot(h, wd_ref[c], preferred_element_type=jnp.float32)
        return carry

    lax.fori_loop(0, N_FF_CHUNKS, chunk, 0)
    o_ref[...] = x + FFN_RESID * _rms_rows(acc_ref[...], post_ref[...])


def _ffn(x, pre, post, wgu, wd):
    t = x.shape[0]
    const3 = lambda i: (0, 0, 0)
    return pl.pallas_call(
        _ffn_kernel,
        name="ffn",
        grid=(t // TOKEN_TILE,),
        in_specs=[
            pl.BlockSpec((TOKEN_TILE, D_MODEL), lambda i: (i, 0)),
            pl.BlockSpec((1, D_MODEL), lambda i: (0, 0)),
            pl.BlockSpec((1, D_MODEL), lambda i: (0, 0)),
            pl.BlockSpec((N_FF_CHUNKS, D_MODEL, 2 * FF_CHUNK), const3),
            pl.BlockSpec((N_FF_CHUNKS, FF_CHUNK, D_MODEL), const3),
        ],
        out_specs=pl.BlockSpec((TOKEN_TILE, D_MODEL), lambda i: (i, 0)),
        out_shape=jax.ShapeDtypeStruct((t, D_MODEL), jnp.float32),
        scratch_shapes=[
            pltpu.VMEM((TOKEN_TILE, D_MODEL), jnp.bfloat16),
            pltpu.VMEM((TOKEN_TILE, D_MODEL), jnp.float32),
        ],
        compiler_params=pltpu.CompilerParams(
            dimension_semantics=("arbitrary",), vmem_limit_bytes=VMEM_LIMIT),
    )(x, pre, post, wgu, wd)


def _rope_norm_head(xt, gain_col, cos_t, sin_t):
    y = _rms_cols(xt, gain_col)
    x0 = y[:HALF_DIM]
    x1 = y[HALF_DIM:]
    return jnp.concatenate([x0 * cos_t - x1 * sin_t, x0 * sin_t + x1 * cos_t], axis=0)


def _mix_in_kernel(x_ref, pre_ref, w_ref, gq_ref, gk_ref, cos_ref, sin_ref,
                   qa_ref, qb_ref, k_ref, v_ref):
    xn = _rms_rows(x_ref[0], pre_ref[...]).astype(jnp.bfloat16)
    pt = lax.dot_general(w_ref[...], xn, (((1,), (1,)), ((), ())),
                         preferred_element_type=jnp.float32)
    cos_t = cos_ref[...]
    sin_t = sin_ref[...]
    gq = gq_ref[...]
    gk = gk_ref[...]
    off = 0
    for h in range(Q_HEADS):
        r = _rope_norm_head(pt[off:off + HEAD_DIM], gq, cos_t, sin_t)
        qa_ref[0, h * HEAD_DIM:(h + 1) * HEAD_DIM, :] = (r * Q_SCALE).astype(jnp.bfloat16)
        off += HEAD_DIM
    for h in range(KV_HEADS):
        r = _rope_norm_head(pt[off:off + HEAD_DIM], gk, cos_t, sin_t)
        k_ref[0, h] = r.T.astype(jnp.bfloat16)
        off += HEAD_DIM
    v_ref[0, 0, :KV_WIDTH, :] = pt[off:off + KV_WIDTH].astype(jnp.bfloat16)
    off += KV_WIDTH
    qb_ref[0] = (pt[off:off + GROUP_WIDTH] * Q_SCALE).astype(jnp.bfloat16)
    off += GROUP_WIDTH
    for h in range(KV_HEADS):
        k_ref[0, KV_HEADS + h] = pt[off:off + HEAD_DIM].T.astype(jnp.bfloat16)
        off += HEAD_DIM
    v_ref[0, 0, KV_WIDTH:, :] = pt[off:off + KV_WIDTH].astype(jnp.bfloat16)


def _mix_in(x, pre, w_in_t, gq, gk, cos_t, sin_t):
    b, s, _ = x.shape
    n_tiles = s // TOKEN_TILE
    return pl.pallas_call(
        _mix_in_kernel,
        name="mix_in",
        grid=(b, n_tiles),
        in_specs=[
            pl.BlockSpec((1, TOKEN_TILE, D_MODEL), lambda bi, i: (bi, i, 0)),
            pl.BlockSpec((1, D_MODEL), lambda bi, i: (0, 0)),
            pl.BlockSpec((IN_WIDTH, D_MODEL), lambda bi, i: (0, 0)),
            pl.BlockSpec((HEAD_DIM, 1), lambda bi, i: (0, 0)),
            pl.BlockSpec((HEAD_DIM, 1), lambda bi, i: (0, 0)),
            pl.BlockSpec((HALF_DIM, TOKEN_TILE), lambda bi, i: (0, i)),
            pl.BlockSpec((HALF_DIM, TOKEN_TILE), lambda bi, i: (0, i)),
        ],
        out_specs=[
            pl.BlockSpec((1, GROUP_WIDTH, TOKEN_TILE), lambda bi, i: (bi, 0, i)),
            pl.BlockSpec((1, GROUP_WIDTH, TOKEN_TILE), lambda bi, i: (bi, 0, i)),
            pl.BlockSpec((1, 2 * KV_HEADS, TOKEN_TILE, HEAD_DIM), lambda bi, i: (bi, 0, i, 0)),
            pl.BlockSpec((1, 1, 2 * KV_WIDTH, V_CHUNK), lambda bi, i: (bi, i, 0, 0)),
        ],
        out_shape=[
            jax.ShapeDtypeStruct((b, GROUP_WIDTH, s), jnp.bfloat16),
            jax.ShapeDtypeStruct((b, GROUP_WIDTH, s), jnp.bfloat16),
            jax.ShapeDtypeStruct((b, 2 * KV_HEADS, s, HEAD_DIM), jnp.bfloat16),
            jax.ShapeDtypeStruct((b, n_tiles, 2 * KV_WIDTH, V_CHUNK), jnp.bfloat16),
        ],
        compiler_params=pltpu.CompilerParams(
            dimension_semantics=("arbitrary", "arbitrary"), vmem_limit_bytes=VMEM_LIMIT),
    )(x, pre, w_in_t, gq, gk, cos_t, sin_t)


def _global_kernel(q_ref, k_ref, v_ref, o_ref, m_ref, acc_ref, *, n_chunks):
    m_ref[...] = jnp.full_like(m_ref, NEG_BIG)
    acc_ref[...] = jnp.zeros_like(acc_ref)
    ones = jnp.ones((ONES_ROWS, V_CHUNK), jnp.bfloat16)

    def chunk(j, carry):
        start = pl.multiple_of(j * V_CHUNK, V_CHUNK)
        kc = k_ref[0, 0, pl.ds(start, V_CHUNK), :]
        vext = jnp.concatenate([v_ref[0, j], ones], axis=0)
        for h in range(HEADS_PER_KV):
            st = jnp.dot(kc, q_ref[0, h * HEAD_DIM:(h + 1) * HEAD_DIM, :],
                         preferred_element_type=jnp.float32)
            m_old = m_ref[h]
            m_new = jnp.maximum(m_old, jnp.max(st, axis=0, keepdims=True))
            p = jnp.exp2(st - m_new).astype(jnp.bfloat16)
            alpha = jnp.exp2(m_old - m_new)
            acc_ref[h] = alpha * acc_ref[h] + jnp.dot(vext, p, preferred_element_type=jnp.float32)
            m_ref[h] = m_new
        return carry

    lax.fori_loop(0, n_chunks, chunk, 0)
    for h in range(HEADS_PER_KV):
        a = acc_ref[h]
        o_ref[0, h * HEAD_DIM:(h + 1) * HEAD_DIM, :] = a[:HEAD_DIM] / a[HEAD_DIM:HEAD_DIM + 1]


def _global_attention(q_t, k, v_t):
    b, _, s = q_t.shape
    n_chunks = s // V_CHUNK
    return pl.pallas_call(
        functools.partial(_global_kernel, n_chunks=n_chunks),
        name="global_attn",
        grid=(b, KV_HEADS, s // Q_TILE),
        in_specs=[
            pl.BlockSpec((1, Q_PER_KV_ROWS, Q_TILE), lambda bi, g, i: (bi, g, i)),
            pl.BlockSpec((1, 1, s, HEAD_DIM), lambda bi, g, i: (bi, g, 0, 0)),
            pl.BlockSpec((1, n_chunks, HEAD_DIM, V_CHUNK), lambda bi, g, i: (bi, 0, g, 0)),
        ],
        out_specs=pl.BlockSpec((1, Q_PER_KV_ROWS, Q_TILE), lambda bi, g, i: (bi, g, i)),
        out_shape=jax.ShapeDtypeStruct((b, GROUP_WIDTH, s), jnp.float32),
        scratch_shapes=[
            pltpu.VMEM((HEADS_PER_KV, 1, Q_TILE), jnp.float32),
            pltpu.VMEM((HEADS_PER_KV, HEAD_DIM + ONES_ROWS, Q_TILE), jnp.float32),
        ],
        compiler_params=pltpu.CompilerParams(
            dimension_semantics=("arbitrary", "arbitrary", "arbitrary"),
            vmem_limit_bytes=VMEM_LIMIT),
    )(q_t, k, v_t)


def _window_kernel(q_ref, kp_ref, kc_ref, kn_ref, vp_ref, vc_ref, vn_ref, bias_ref, sink_ref,
                   o_ref):
    i = pl.program_id(2)
    has_prev = i > 0
    has_next = i < pl.num_programs(2) - 1
    kcat = jnp.concatenate([kp_ref[0, 0], kc_ref[0, 0], kn_ref[0, 0]], axis=0)
    vcat = jnp.concatenate([vp_ref[0, 0], vc_ref[0, 0], vn_ref[0, 0]], axis=1)
    bias = bias_ref[0]
    sink = sink_ref[0]
    for sb in range(N_SUB):
        lo = sb * WINDOW
        ks = kcat[lo:lo + BAND]
        vs = vcat[:, lo:lo + BAND]
        qs = jnp.concatenate(
            [q_ref[0, h * HEAD_DIM:(h + 1) * HEAD_DIM, lo:lo + WINDOW] for h in range(HEADS_PER_KV)],
            axis=1)
        st = jnp.dot(ks, qs, preferred_element_type=jnp.float32) + bias
        if sb == 0:
            row = lax.broadcasted_iota(jnp.int32, st.shape, 0)
            st = jnp.where(jnp.logical_or(has_prev, row >= WINDOW), st, NEG_BIG)
        if sb == N_SUB - 1:
            row = lax.broadcasted_iota(jnp.int32, st.shape, 0)
            st = jnp.where(jnp.logical_or(has_next, row < 2 * WINDOW), st, NEG_BIG)
        mx = jnp.maximum(jnp.max(st, axis=0, keepdims=True), sink)
        e = jnp.exp2(st - mx)
        denom = jnp.sum(e, axis=0, keepdims=True) + jnp.exp2(sink - mx)
        ot = jnp.dot(vs, e.astype(jnp.bfloat16), preferred_element_type=jnp.float32) / denom
        for h in range(HEADS_PER_KV):
            o_ref[0, h * HEAD_DIM:(h + 1) * HEAD_DIM, lo:lo + WINDOW] = (
                ot[:, h * WINDOW:(h + 1) * WINDOW])


def _window_attention(q_t, k, v_t, bias, sink):
    b, _, s = q_t.shape
    n_blk = s // WINDOW
    per = WQ_TILE // WINDOW
    vper = V_CHUNK // WINDOW

    def prev_blk(i):
        return jnp.maximum(i * per - 1, 0)

    def next_blk(i):
        return jnp.minimum(i * per + per, n_blk - 1)

    kh = lambda g: KV_HEADS + g
    return pl.pallas_call(
        _window_kernel,
        name="window_attn",
        grid=(b, KV_HEADS, s // WQ_TILE),
        in_specs=[
            pl.BlockSpec((1, Q_PER_KV_ROWS, WQ_TILE), lambda bi, g, i: (bi, g, i)),
            pl.BlockSpec((1, 1, WINDOW, HEAD_DIM), lambda bi, g, i: (bi, kh(g), prev_blk(i), 0)),
            pl.BlockSpec((1, 1, WQ_TILE, HEAD_DIM), lambda bi, g, i: (bi, kh(g), i, 0)),
            pl.BlockSpec((1, 1, WINDOW, HEAD_DIM), lambda bi, g, i: (bi, kh(g), next_blk(i), 0)),
            pl.BlockSpec((1, 1, HEAD_DIM, WINDOW),
                         lambda bi, g, i: (bi, prev_blk(i) // vper, kh(g), prev_blk(i) % vper)),
            pl.BlockSpec((1, 1, HEAD_DIM, WQ_TILE), lambda bi, g, i: (bi, i, kh(g), 0)),
            pl.BlockSpec((1, 1, HEAD_DIM, WINDOW),
                         lambda bi, g, i: (bi, next_blk(i) // vper, kh(g), next_blk(i) % vper)),
            pl.BlockSpec((1, BAND, HEADS_PER_KV * WINDOW), lambda bi, g, i: (g, 0, 0)),
            pl.BlockSpec((1, 1, HEADS_PER_KV * WINDOW), lambda bi, g, i: (g, 0, 0)),
        ],
        out_specs=pl.BlockSpec((1, Q_PER_KV_ROWS, WQ_TILE), lambda bi, g, i: (bi, g, i)),
        out_shape=jax.ShapeDtypeStruct((b, GROUP_WIDTH, s), jnp.float32),
        compiler_params=pltpu.CompilerParams(
            dimension_semantics=("arbitrary", "arbitrary", "arbitrary"),
            vmem_limit_bytes=VMEM_LIMIT),
    )(q_t, k, k, k, v_t, v_t, v_t, bias, sink)


def _mix_out_kernel(x_ref, oa_ref, ob_ref, ga_ref, gb_ref, w_ref, post_ref, o_ref):
    na = _rms_cols(oa_ref[0], ga_ref[...])
    nb = _rms_cols(ob_ref[0], gb_ref[...])
    mt = jnp.concatenate([na, nb], axis=0).astype(jnp.bfloat16)
    ht = jnp.dot(w_ref[...], mt, preferred_element_type=jnp.float32)
    ms = jnp.mean(ht * ht, axis=0, keepdims=True)
    yt = ht * lax.rsqrt(ms + NORM_EPS)
    o_ref[0] = x_ref[0] + yt.T * post_ref[...]


def _mix_out(x, oa_t, ob_t, ga, gb, w_out_t, post):
    b, s, _ = x.shape
    return pl.pallas_call(
        _mix_out_kernel,
        name="mix_out",
        grid=(b, s // TOKEN_TILE),
        in_specs=[
            pl.BlockSpec((1, TOKEN_TILE, D_MODEL), lambda bi, i: (bi, i, 0)),
            pl.BlockSpec((1, GROUP_WIDTH, TOKEN_TILE), lambda bi, i: (bi, 0, i)),
            pl.BlockSpec((1, GROUP_WIDTH, TOKEN_TILE), lambda bi, i: (bi, 0, i)),
            pl.BlockSpec((GROUP_WIDTH, 1), lambda bi, i: (0, 0)),
            pl.BlockSpec((GROUP_WIDTH, 1), lambda bi, i: (0, 0)),
            pl.BlockSpec((D_MODEL, D_MODEL), lambda bi, i: (0, 0)),
            pl.BlockSpec((1, D_MODEL), lambda bi, i: (0, 0)),
        ],
        out_specs=pl.BlockSpec((1, TOKEN_TILE, D_MODEL), lambda bi, i: (bi, i, 0)),
        out_shape=jax.ShapeDtypeStruct((b, s, D_MODEL), jnp.float32),
        compiler_params=pltpu.CompilerParams(
            dimension_semantics=("arbitrary", "arbitrary"), vmem_limit_bytes=VMEM_LIMIT),
    )(x, oa_t, ob_t, ga, gb, w_out_t, post)


_DEINTERLEAVE = np.concatenate([np.arange(0, HEAD_DIM, 2), np.arange(1, HEAD_DIM, 2)])


def _w_in_rows():
    n_rot = (Q_HEADS + KV_HEADS) * HEAD_DIM
    rot = (np.arange(Q_HEADS + KV_HEADS)[:, None] * HEAD_DIM + _DEINTERLEAVE[None, :]).reshape(-1)
    return np.concatenate([rot, np.arange(n_rot, IN_WIDTH)])


def _rope_tables_t(seq_len):
    n_rows = seq_len // GRID_W
    row = jnp.repeat(jnp.arange(n_rows, dtype=jnp.float32), GRID_W)
    col = jnp.tile(jnp.arange(GRID_W, dtype=jnp.float32), n_rows)
    n_freq = HEAD_DIM // 4
    inv_freq = ROPE_THETA ** (-jnp.arange(n_freq, dtype=jnp.float32) / n_freq)
    ang = jnp.concatenate([inv_freq[:, None] * row[None, :], inv_freq[:, None] * col[None, :]], axis=0)
    return jnp.cos(ang), jnp.sin(ang)


def _window_bias():
    slopes = 2.0 ** (-8.0 * np.arange(1, Q_HEADS + 1, dtype=np.float64) / Q_HEADS)
    kpos = np.arange(BAND)[:, None] - WINDOW
    qpos = np.arange(WINDOW)[None, :]
    dist = np.abs(qpos - kpos).astype(np.float64)
    per_head = np.where(dist[None] <= WINDOW, -slopes[:, None, None] * dist[None] * LOG2E, NEG_BIG)
    per_kv = per_head.reshape(KV_HEADS, HEADS_PER_KV, BAND, WINDOW).transpose(0, 2, 1, 3)
    return jnp.asarray(per_kv.reshape(KV_HEADS, BAND, HEADS_PER_KV * WINDOW), jnp.float32)


def _ffn_weights(w_gate, w_up, w_down):
    wg = w_gate.reshape(D_MODEL, N_FF_CHUNKS, FF_CHUNK)
    wu = w_up.reshape(D_MODEL, N_FF_CHUNKS, FF_CHUNK)
    wgu = jnp.concatenate([wg, wu], axis=-1).transpose(1, 0, 2).astype(jnp.bfloat16)
    wd = w_down.reshape(N_FF_CHUNKS, FF_CHUNK, D_MODEL).astype(jnp.bfloat16)
    return wgu, wd


def _trunk(x, layers, bias):
    b, s, _ = x.shape
    cos_t, sin_t = _rope_tables_t(s)
    for p in layers:
        x = _ffn(x.reshape(b * s, D_MODEL), p["ffn1_pre"], p["ffn1_post"], p["ffn1_wgu"],
                 p["ffn1_wd"]).reshape(b, s, D_MODEL)
        qa_t, qb_t, k, v_t = _mix_in(x, p["mix_pre"], p["w_in_t"], p["gq"], p["gk"], cos_t, sin_t)
        oa_t = _global_attention(qa_t, k, v_t)
        ob_t = _window_attention(qb_t, k, v_t, bias, p["sink"])
        x = _mix_out(x, oa_t, ob_t, p["ga"], p["gb"], p["w_out_t"], p["mix_post"])
        x = _ffn(x.reshape(b * s, D_MODEL), p["ffn2_pre"], p["ffn2_post"], p["ffn2_wgu"],
                 p["ffn2_wd"]).reshape(b, s, D_MODEL)
    return x


def kernel(x_prompt, x_sample, ffn1_pre, ffn1_post, ffn1_w_gate, ffn1_w_up, ffn1_w_down, mix_pre, mix_post, w_in, a_q_norm, a_k_norm, b_sink, a_out_norm, b_out_norm, w_out, ffn2_pre, ffn2_post, ffn2_w_gate, ffn2_w_up, ffn2_w_down):
    depth = w_in.shape[0]
    rows = _w_in_rows()
    layers = []
    for l in range(depth):
        wgu1, wd1 = _ffn_weights(ffn1_w_gate[l], ffn1_w_up[l], ffn1_w_down[l])
        wgu2, wd2 = _ffn_weights(ffn2_w_gate[l], ffn2_w_up[l], ffn2_w_down[l])
        sink = jnp.repeat(b_sink[l].astype(jnp.float32) * LOG2E, WINDOW)
        layers.append(dict(
            ffn1_pre=ffn1_pre[l][None, :], ffn1_post=ffn1_post[l][None, :], ffn1_wgu=wgu1, ffn1_wd=wd1,
            ffn2_pre=ffn2_pre[l][None, :], ffn2_post=ffn2_post[l][None, :], ffn2_wgu=wgu2, ffn2_wd=wd2,
            mix_pre=mix_pre[l][None, :], mix_post=mix_post[l][None, :],
            w_in_t=w_in[l].T[rows].astype(jnp.bfloat16),
            gq=a_q_norm[l][_DEINTERLEAVE][:, None], gk=a_k_norm[l][_DEINTERLEAVE][:, None],
            sink=sink.reshape(KV_HEADS, 1, HEADS_PER_KV * WINDOW),
            ga=a_out_norm[l][:, None], gb=b_out_norm[l][:, None],
            w_out_t=w_out[l].T.astype(jnp.bfloat16),
        ))
    bias = _window_bias()
    return (_trunk(x_prompt, layers, bias), _trunk(x_sample, layers, bias))
```

```python
import functools
import math

import jax
import jax.numpy as jnp
import numpy as np
from jax import lax
from jax.experimental import pallas as pl
from jax.experimental.pallas import tpu as pltpu

D_MODEL = 1024
HEAD_DIM = 64
HALF_DIM = HEAD_DIM // 2
Q_HEADS = 8
KV_HEADS = 2
HEADS_PER_KV = Q_HEADS // KV_HEADS
GROUP_WIDTH = Q_HEADS * HEAD_DIM
KV_WIDTH = KV_HEADS * HEAD_DIM
Q_PER_KV_ROWS = HEADS_PER_KV * HEAD_DIM
IN_WIDTH = 2 * (GROUP_WIDTH + 2 * KV_WIDTH)
D_FF = 2816
GRID_W = 64
ROPE_THETA = 10000.0
WINDOW = 128
NORM_EPS = 1e-6
FFN_RESID = 0.5
LOG2E = math.log2(math.e)
Q_SCALE = HEAD_DIM ** -0.5 * LOG2E
NEG_BIG = -1e30

FF_CHUNK = 256
N_FF_CHUNKS = D_FF // FF_CHUNK
TOKEN_TILE = 512
V_CHUNK = TOKEN_TILE
Q_TILE = 256
ONES_ROWS = 16
WQ_TILE = 512
N_SUB = WQ_TILE // WINDOW
BAND = 3 * WINDOW

VMEM_LIMIT = 56 * 1024 * 1024


def _rms_rows(x, gain_row):
    ms = jnp.mean(x * x, axis=-1, keepdims=True)
    return x * lax.rsqrt(ms + NORM_EPS) * gain_row


def _rms_cols(xt, gain_col):
    ms = jnp.mean(xt * xt, axis=0, keepdims=True)
    return xt * lax.rsqrt(ms + NORM_EPS) * gain_col


def _ffn_kernel(x_ref, pre_ref, post_ref, wgu_ref, wd_ref, o_ref, xn_ref, acc_ref):
    x = x_ref[...]
    xn_ref[...] = _rms_rows(x, pre_ref[...]).astype(jnp.bfloat16)
    acc_ref[...] = jnp.zeros_like(acc_ref)

    def chunk(c, carry):
        gu = jnp.dot(xn_ref[...], wgu_ref[c], preferred_element_type=jnp.float32)
        g = gu[:, :FF_CHUNK]
        u = gu[:, FF_CHUNK:]
        h = (g * jax.nn.sigmoid(g) * u).astype(jnp.bfloat16)
        acc_ref[...] += jnp.dot(h, wd_ref[c], preferred_element_type=jnp.float32)
        return carry

    lax.fori_loop(0, N_FF_CHUNKS, chunk, 0)
    o_ref[...] = x + FFN_RESID * _rms_rows(acc_ref[...], post_ref[...])


def _ffn(x, pre, post, wgu, wd):
    t = x.shape[0]
    const3 = lambda i: (0, 0, 0)
    return pl.pallas_call(
        _ffn_kernel,
        name="ffn",
        grid=(t // TOKEN_TILE,),
        in_specs=[
            pl.BlockSpec((TOKEN_TILE, D_MODEL), lambda i: (i, 0)),
            pl.BlockSpec((1, D_MODEL), lambda i: (0, 0)),
            pl.BlockSpec((1, D_MODEL), lambda i: (0, 0)),
            pl.BlockSpec((N_FF_CHUNKS, D_MODEL, 2 * FF_CHUNK), const3),
            pl.BlockSpec((N_FF_CHUNKS, FF_CHUNK, D_MODEL), const3),
        ],
        out_specs=pl.BlockSpec((TOKEN_TILE, D_MODEL), lambda i: (i, 0)),
        out_shape=jax.ShapeDtypeStruct((t, D_MODEL), jnp.float32),
        scratch_shapes=[
            pltpu.VMEM((TOKEN_TILE, D_MODEL), jnp.bfloat16),
            pltpu.VMEM((TOKEN_TILE, D_MODEL), jnp.float32),
        ],
        compiler_params=pltpu.CompilerParams(
            dimension_semantics=("arbitrary",), vmem_limit_bytes=VMEM_LIMIT),
    )(x, pre, post, wgu, wd)


def _rope_norm_head(xt, gain_col, cos_t, sin_t):
    y = _rms_cols(xt, gain_col)
    x0 = y[:HALF_DIM]
    x1 = y[HALF_DIM:]
    return jnp.concatenate([x0 * cos_t - x1 * sin_t, x0 * sin_t + x1 * cos_t], axis=0)


def _mix_in_kernel(x_ref, pre_ref, w_ref, gq_ref, gk_ref, cos_ref, sin_ref,
                   qa_ref, qb_ref, k_ref, v_ref):
    xn = _rms_rows(x_ref[0], pre_ref[...]).astype(jnp.bfloat16)
    pt = lax.dot_general(w_ref[...], xn, (((1,), (1,)), ((), ())),
                         preferred_element_type=jnp.float32)
    cos_t = cos_ref[...]
    sin_t = sin_ref[...]
    gq = gq_ref[...]
    gk = gk_ref[...]
    off = 0
    for h in range(Q_HEADS):
        r = _rope_norm_head(pt[off:off + HEAD_DIM], gq, cos_t, sin_t)
        qa_ref[0, h * HEAD_DIM:(h + 1) * HEAD_DIM, :] = (r * Q_SCALE).astype(jnp.bfloat16)
        off += HEAD_DIM
    for h in range(KV_HEADS):
        r = _rope_norm_head(pt[off:off + HEAD_DIM], gk, cos_t, sin_t)
        k_ref[0, h] = r.T.astype(jnp.bfloat16)
        off += HEAD_DIM
    v_ref[0, 0, :KV_WIDTH, :] = pt[off:off + KV_WIDTH].astype(jnp.bfloat16)
    off += KV_WIDTH
    qb_ref[0] = (pt[off:off + GROUP_WIDTH] * Q_SCALE).astype(jnp.bfloat16)
    off += GROUP_WIDTH
    for h in range(KV_HEADS):
        k_ref[0, KV_HEADS + h] = pt[off:off + HEAD_DIM].T.astype(jnp.bfloat16)
        off += HEAD_DIM
    v_ref[0, 0, KV_WIDTH:, :] = pt[off:off + KV_WIDTH].astype(jnp.bfloat16)


def _mix_in(x, pre, w_in_t, gq, gk, cos_t, sin_t):
    b, s, _ = x.shape
    n_tiles = s // TOKEN_TILE
    return pl.pallas_call(
        _mix_in_kernel,
        name="mix_in",
        grid=(b, n_tiles),
        in_specs=[
            pl.BlockSpec((1, TOKEN_TILE, D_MODEL), lambda bi, i: (bi, i, 0)),
            pl.BlockSpec((1, D_MODEL), lambda bi, i: (0, 0)),
            pl.BlockSpec((IN_WIDTH, D_MODEL), lambda bi, i: (0, 0)),
            pl.BlockSpec((HEAD_DIM, 1), lambda bi, i: (0, 0)),
            pl.BlockSpec((HEAD_DIM, 1), lambda bi, i: (0, 0)),
            pl.BlockSpec((HALF_DIM, TOKEN_TILE), lambda bi, i: (0, i)),
            pl.BlockSpec((HALF_DIM, TOKEN_TILE), lambda bi, i: (0, i)),
        ],
        out_specs=[
            pl.BlockSpec((1, GROUP_WIDTH, TOKEN_TILE), lambda bi, i: (bi, 0, i)),
            pl.BlockSpec((1, GROUP_WIDTH, TOKEN_TILE), lambda bi, i: (bi, 0, i)),
            pl.BlockSpec((1, 2 * KV_HEADS, TOKEN_TILE, HEAD_DIM), lambda bi, i: (bi, 0, i, 0)),
            pl.BlockSpec((1, 1, 2 * KV_WIDTH, V_CHUNK), lambda bi, i: (bi, i, 0, 0)),
        ],
        out_shape=[
            jax.ShapeDtypeStruct((b, GROUP_WIDTH, s), jnp.bfloat16),
            jax.ShapeDtypeStruct((b, GROUP_WIDTH, s), jnp.bfloat16),
            jax.ShapeDtypeStruct((b, 2 * KV_HEADS, s, HEAD_DIM), jnp.bfloat16),
            jax.ShapeDtypeStruct((b, n_tiles, 2 * KV_WIDTH, V_CHUNK), jnp.bfloat16),
        ],
        compiler_params=pltpu.CompilerParams(
            dimension_semantics=("arbitrary", "arbitrary"), vmem_limit_bytes=VMEM_LIMIT),
    )(x, pre, w_in_t, gq, gk, cos_t, sin_t)


def _global_kernel(q_ref, k_ref, v_ref, o_ref, s_ref, smax_ref, p_ref, alpha_ref, m_ref, acc_ref,
                   *, n_chunks):
    m_ref[...] = jnp.full_like(m_ref, NEG_BIG)
    acc_ref[...] = jnp.zeros_like(acc_ref)
    ones = jnp.ones((ONES_ROWS, V_CHUNK), jnp.bfloat16)

    def scores(j, slot):
        start = pl.multiple_of(j * V_CHUNK, V_CHUNK)
        kc = k_ref[0, 0, pl.ds(start, V_CHUNK), :]
        for h in range(HEADS_PER_KV):
            st = jnp.dot(kc, q_ref[0, h * HEAD_DIM:(h + 1) * HEAD_DIM, :],
                         preferred_element_type=jnp.float32)
            s_ref[slot, h] = st
            smax_ref[slot, h] = jnp.max(st, axis=0, keepdims=True)

    def softmax(slot):
        for h in range(HEADS_PER_KV):
            m_old = m_ref[h]
            m_new = jnp.maximum(m_old, smax_ref[slot, h])
            p_ref[slot, h] = jnp.exp2(s_ref[slot, h] - m_new).astype(jnp.bfloat16)
            alpha_ref[slot, h] = jnp.exp2(m_old - m_new)
            m_ref[h] = m_new

    def values(j, slot):
        vext = jnp.concatenate([v_ref[0, j], ones], axis=0)
        for h in range(HEADS_PER_KV):
            acc_ref[h] = alpha_ref[slot, h] * acc_ref[h] + jnp.dot(
                vext, p_ref[slot, h], preferred_element_type=jnp.float32)

    scores(0, 0)
    scores(1, 1)
    softmax(0)

    def pair(t, carry):
        j = 2 * t + 1
        scores(j + 1, 0)
        softmax(1)
        values(j - 1, 0)
        scores(j + 2, 1)
        softmax(0)
        values(j, 1)
        return carry

    lax.fori_loop(0, (n_chunks - 2) // 2, pair, 0)
    softmax(1)
    values(n_chunks - 2, 0)
    values(n_chunks - 1, 1)
    for h in range(HEADS_PER_KV):
        a = acc_ref[h]
        o_ref[0, h * HEAD_DIM:(h + 1) * HEAD_DIM, :] = a[:HEAD_DIM] / a[HEAD_DIM:HEAD_DIM + 1]


def _global_attention(q_t, k, v_t):
    b, _, s = q_t.shape
    n_chunks = s // V_CHUNK
    assert n_chunks >= 2 and n_chunks % 2 == 0, "the kv pipeline runs chunks in pairs"
    return pl.pallas_call(
        functools.partial(_global_kernel, n_chunks=n_chunks),
        name="global_attn",
        grid=(b, KV_HEADS, s // Q_TILE),
        in_specs=[
            pl.BlockSpec((1, Q_PER_KV_ROWS, Q_TILE), lambda bi, g, i: (bi, g, i)),
            pl.BlockSpec((1, 1, s, HEAD_DIM), lambda bi, g, i: (bi, g, 0, 0)),
            pl.BlockSpec((1, n_chunks, HEAD_DIM, V_CHUNK), lambda bi, g, i: (bi, 0, g, 0)),
        ],
        out_specs=pl.BlockSpec((1, Q_PER_KV_ROWS, Q_TILE), lambda bi, g, i: (bi, g, i)),
        out_shape=jax.ShapeDtypeStruct((b, GROUP_WIDTH, s), jnp.float32),
        scratch_shapes=[
            pltpu.VMEM((2, HEADS_PER_KV, V_CHUNK, Q_TILE), jnp.float32),
            pltpu.VMEM((2, HEADS_PER_KV, 1, Q_TILE), jnp.float32),
            pltpu.VMEM((2, HEADS_PER_KV, V_CHUNK, Q_TILE), jnp.bfloat16),
            pltpu.VMEM((2, HEADS_PER_KV, 1, Q_TILE), jnp.float32),
            pltpu.VMEM((HEADS_PER_KV, 1, Q_TILE), jnp.float32),
            pltpu.VMEM((HEADS_PER_KV, HEAD_DIM + ONES_ROWS, Q_TILE), jnp.float32),
        ],
        compiler_params=pltpu.CompilerParams(
            dimension_semantics=("arbitrary", "arbitrary", "arbitrary"),
            vmem_limit_bytes=VMEM_LIMIT),
    )(q_t, k, v_t)


def _window_kernel(q_ref, kp_ref, kc_ref, kn_ref, vp_ref, vc_ref, vn_ref, bias_ref, sink_ref,
                   o_ref):
    i = pl.program_id(2)
    has_prev = i > 0
    has_next = i < pl.num_programs(2) - 1
    kcat = jnp.concatenate([kp_ref[0, 0], kc_ref[0, 0], kn_ref[0, 0]], axis=0)
    vcat = jnp.concatenate([vp_ref[0, 0], vc_ref[0, 0], vn_ref[0, 0]], axis=1)
    bias = bias_ref[0]
    sink = sink_ref[0]
    for sb in range(N_SUB):
        lo = sb * WINDOW
        ks = kcat[lo:lo + BAND]
        vs = vcat[:, lo:lo + BAND]
        qs = jnp.concatenate(
            [q_ref[0, h * HEAD_DIM:(h + 1) * HEAD_DIM, lo:lo + WINDOW] for h in range(HEADS_PER_KV)],
            axis=1)
        st = jnp.dot(ks, qs, preferred_element_type=jnp.float32) + bias
        if sb == 0:
            row = lax.broadcasted_iota(jnp.int32, st.shape, 0)
            st = jnp.where(jnp.logical_or(has_prev, row >= WINDOW), st, NEG_BIG)
        if sb == N_SUB - 1:
            row = lax.broadcasted_iota(jnp.int32, st.shape, 0)
            st = jnp.where(jnp.logical_or(has_next, row < 2 * WINDOW), st, NEG_BIG)
        mx = jnp.maximum(jnp.max(st, axis=0, keepdims=True), sink)
        e = jnp.exp2(st - mx)
        denom = jnp.sum(e, axis=0, keepdims=True) + jnp.exp2(sink - mx)
        ot = jnp.dot(vs, e.astype(jnp.bfloat16), preferred_element_type=jnp.float32) / denom
        for h in range(HEADS_PER_KV):
            o_ref[0, h * HEAD_DIM:(h + 1) * HEAD_DIM, lo:lo + WINDOW] = (
                ot[:, h * WINDOW:(h + 1) * WINDOW])


def _window_attention(q_t, k, v_t, bias, sink):
    b, _, s = q_t.shape
    n_blk = s // WINDOW
    per = WQ_TILE // WINDOW
    vper = V_CHUNK // WINDOW

    def prev_blk(i):
        return jnp.maximum(i * per - 1, 0)

    def next_blk(i):
        return jnp.minimum(i * per + per, n_blk - 1)

    kh = lambda g: KV_HEADS + g
    return pl.pallas_call(
        _window_kernel,
        name="window_attn",
        grid=(b, KV_HEADS, s // WQ_TILE),
        in_specs=[
            pl.BlockSpec((1, Q_PER_KV_ROWS, WQ_TILE), lambda bi, g, i: (bi, g, i)),
            pl.BlockSpec((1, 1, WINDOW, HEAD_DIM), lambda bi, g, i: (bi, kh(g), prev_blk(i), 0)),
            pl.BlockSpec((1, 1, WQ_TILE, HEAD_DIM), lambda bi, g, i: (bi, kh(g), i, 0)),
            pl.BlockSpec((1, 1, WINDOW, HEAD_DIM), lambda bi, g, i: (bi, kh(g), next_blk(i), 0)),
            pl.BlockSpec((1, 1, HEAD_DIM, WINDOW),
                         lambda bi, g, i: (bi, prev_blk(i) // vper, kh(g), prev_blk(i) % vper)),
            pl.BlockSpec((1, 1, HEAD_DIM, WQ_TILE), lambda bi, g, i: (bi, i, kh(g), 0)),
            pl.BlockSpec((1, 1, HEAD_DIM, WINDOW),
                         lambda bi, g, i: (bi, next_blk(i) // vper, kh(g), next_blk(i) % vper)),
            pl.BlockSpec((1, BAND, HEADS_PER_KV * WINDOW), lambda bi, g, i: (g, 0, 0)),
            pl.BlockSpec((1, 1, HEADS_PER_KV * WINDOW), lambda bi, g, i: (g, 0, 0)),
        ],
        out_specs=pl.BlockSpec((1, Q_PER_KV_ROWS, WQ_TILE), lambda bi, g, i: (bi, g, i)),
        out_shape=jax.ShapeDtypeStruct((b, GROUP_WIDTH, s), jnp.float32),
        compiler_params=pltpu.CompilerParams(
            dimension_semantics=("arbitrary", "arbitrary", "arbitrary"),
            vmem_limit_bytes=VMEM_LIMIT),
    )(q_t, k, k, k, v_t, v_t, v_t, bias, sink)


def _mix_out_kernel(x_ref, oa_ref, ob_ref, ga_ref, gb_ref, w_ref, post_ref, o_ref):
    na = _rms_cols(oa_ref[0], ga_ref[...])
    nb = _rms_cols(ob_ref[0], gb_ref[...])
    mt = jnp.concatenate([na, nb], axis=0).astype(jnp.bfloat16)
    ht = jnp.dot(w_ref[...], mt, preferred_element_type=jnp.float32)
    ms = jnp.mean(ht * ht, axis=0, keepdims=True)
    yt = ht * lax.rsqrt(ms + NORM_EPS)
    o_ref[0] = x_ref[0] + yt.T * post_ref[...]


def _mix_out(x, oa_t, ob_t, ga, gb, w_out_t, post):
    b, s, _ = x.shape
    return pl.pallas_call(
        _mix_out_kernel,
        name="mix_out",
        grid=(b, s // TOKEN_TILE),
        in_specs=[
            pl.BlockSpec((1, TOKEN_TILE, D_MODEL), lambda bi, i: (bi, i, 0)),
            pl.BlockSpec((1, GROUP_WIDTH, TOKEN_TILE), lambda bi, i: (bi, 0, i)),
            pl.BlockSpec((1, GROUP_WIDTH, TOKEN_TILE), lambda bi, i: (bi, 0, i)),
            pl.BlockSpec((GROUP_WIDTH, 1), lambda bi, i: (0, 0)),
            pl.BlockSpec((GROUP_WIDTH, 1), lambda bi, i: (0, 0)),
            pl.BlockSpec((D_MODEL, D_MODEL), lambda bi, i: (0, 0)),
            pl.BlockSpec((1, D_MODEL), lambda bi, i: (0, 0)),
        ],
        out_specs=pl.BlockSpec((1, TOKEN_TILE, D_MODEL), lambda bi, i: (bi, i, 0)),
        out_shape=jax.ShapeDtypeStruct((b, s, D_MODEL), jnp.float32),
        compiler_params=pltpu.CompilerParams(
            dimension_semantics=("arbitrary", "arbitrary"), vmem_limit_bytes=VMEM_LIMIT),
    )(x, oa_t, ob_t, ga, gb, w_out_t, post)


_DEINTERLEAVE = np.concatenate([np.arange(0, HEAD_DIM, 2), np.arange(1, HEAD_DIM, 2)])


def _w_in_rows():
    n_rot = (Q_HEADS + KV_HEADS) * HEAD_DIM
    rot = (np.arange(Q_HEADS + KV_HEADS)[:, None] * HEAD_DIM + _DEINTERLEAVE[None, :]).reshape(-1)
    return np.concatenate([rot, np.arange(n_rot, IN_WIDTH)])


def _rope_tables_t(seq_len):
    n_rows = seq_len // GRID_W
    row = jnp.repeat(jnp.arange(n_rows, dtype=jnp.float32), GRID_W)
    col = jnp.tile(jnp.arange(GRID_W, dtype=jnp.float32), n_rows)
    n_freq = HEAD_DIM // 4
    inv_freq = ROPE_THETA ** (-jnp.arange(n_freq, dtype=jnp.float32) / n_freq)
    ang = jnp.concatenate([inv_freq[:, None] * row[None, :], inv_freq[:, None] * col[None, :]], axis=0)
    return jnp.cos(ang), jnp.sin(ang)


def _window_bias():
    slopes = 2.0 ** (-8.0 * np.arange(1, Q_HEADS + 1, dtype=np.float64) / Q_HEADS)
    kpos = np.arange(BAND)[:, None] - WINDOW
    qpos = np.arange(WINDOW)[None, :]
    dist = np.abs(qpos - kpos).astype(np.float64)
    per_head = np.where(dist[None] <= WINDOW, -slopes[:, None, None] * dist[None] * LOG2E, NEG_BIG)
    per_kv = per_head.reshape(KV_HEADS, HEADS_PER_KV, BAND, WINDOW).transpose(0, 2, 1, 3)
    return jnp.asarray(per_kv.reshape(KV_HEADS, BAND, HEADS_PER_KV * WINDOW), jnp.float32)


def _ffn_weights(w_gate, w_up, w_down):
    wg = w_gate.reshape(D_MODEL, N_FF_CHUNKS, FF_CHUNK)
    wu = w_up.reshape(D_MODEL, N_FF_CHUNKS, FF_CHUNK)
    wgu = jnp.concatenate([wg, wu], axis=-1).transpose(1, 0, 2).astype(jnp.bfloat16)
    wd = w_down.reshape(N_FF_CHUNKS, FF_CHUNK, D_MODEL).astype(jnp.bfloat16)
    return wgu, wd


def _trunk(x, layers, bias):
    b, s, _ = x.shape
    cos_t, sin_t = _rope_tables_t(s)
    for p in layers:
        x = _ffn(x.reshape(b * s, D_MODEL), p["ffn1_pre"], p["ffn1_post"], p["ffn1_wgu"],
                 p["ffn1_wd"]).reshape(b, s, D_MODEL)
        qa_t, qb_t, k, v_t = _mix_in(x, p["mix_pre"], p["w_in_t"], p["gq"], p["gk"], cos_t, sin_t)
        oa_t = _global_attention(qa_t, k, v_t)
        ob_t = _window_attention(qb_t, k, v_t, bias, p["sink"])
        x = _mix_out(x, oa_t, ob_t, p["ga"], p["gb"], p["w_out_t"], p["mix_post"])
        x = _ffn(x.reshape(b * s, D_MODEL), p["ffn2_pre"], p["ffn2_post"], p["ffn2_wgu"],
                 p["ffn2_wd"]).reshape(b, s, D_MODEL)
    return x


def kernel(x_prompt, x_sample, ffn1_pre, ffn1_post, ffn1_w_gate, ffn1_w_up, ffn1_w_down, mix_pre, mix_post, w_in, a_q_norm, a_k_norm, b_sink, a_out_norm, b_out_norm, w_out, ffn2_pre, ffn2_post, ffn2_w_gate, ffn2_w_up, ffn2_w_down):
    depth = w_in.shape[0]
    rows = _w_in_rows()
    layers = []
    for l in range(depth):
        wgu1, wd1 = _ffn_weights(ffn1_w_gate[l], ffn1_w_up[l], ffn1_w_down[l])
        wgu2, wd2 = _ffn_weights(ffn2_w_gate[l], ffn2_w_up[l], ffn2_w_down[l])
        sink = jnp.repeat(b_sink[l].astype(jnp.float32) * LOG2E, WINDOW)
        layers.append(dict(
            ffn1_pre=ffn1_pre[l][None, :], ffn1_post=ffn1_post[l][None, :], ffn1_wgu=wgu1, ffn1_wd=wd1,
            ffn2_pre=ffn2_pre[l][None, :], ffn2_post=ffn2_post[l][None, :], ffn2_wgu=wgu2, ffn2_wd=wd2,
            mix_pre=mix_pre[l][None, :], mix_post=mix_post[l][None, :],
            w_in_t=w_in[l].T[rows].astype(jnp.bfloat16),
            gq=a_q_norm[l][_DEINTERLEAVE][:, None], gk=a_k_norm[l][_DEINTERLEAVE][:, None],
            sink=sink.reshape(KV_HEADS, 1, HEADS_PER_KV * WINDOW),
            ga=a_out_norm[l][:, None], gb=b_out_norm[l][:, None],
            w_out_t=w_out[l].T.astype(jnp.bfloat16),
        ))
    bias = _window_bias()
    return (_trunk(x_prompt, layers, bias), _trunk(x_sample, layers, bias))
```

```python
import functools
import math

import jax
import jax.numpy as jnp
import numpy as np
from jax import lax
from jax.experimental import pallas as pl
from jax.experimental.pallas import tpu as pltpu

D_MODEL = 1024
HEAD_DIM = 64
HALF_DIM = HEAD_DIM // 2
Q_HEADS = 8
KV_HEADS = 2
HEADS_PER_KV = Q_HEADS // KV_HEADS
GROUP_WIDTH = Q_HEADS * HEAD_DIM
KV_WIDTH = KV_HEADS * HEAD_DIM
Q_PER_KV_ROWS = HEADS_PER_KV * HEAD_DIM
IN_WIDTH = 2 * (GROUP_WIDTH + 2 * KV_WIDTH)
D_FF = 2816
GRID_W = 64
ROPE_THETA = 10000.0
WINDOW = 128
NORM_EPS = 1e-6
FFN_RESID = 0.5
LOG2E = math.log2(math.e)
Q_SCALE = HEAD_DIM ** -0.5 * LOG2E
NEG_BIG = -1e30
BOUND_SLACK = 1.01
MAX_FIXED_SHIFT = 48.0

FF_CHUNK = 256
N_FF_CHUNKS = D_FF // FF_CHUNK
TOKEN_TILE = 512
V_CHUNK = TOKEN_TILE
Q_TILE = 512
ONES_ROWS = 16
ROW_BLOCK = 128
WQ_TILE = 512
N_SUB = WQ_TILE // WINDOW
BAND = 3 * WINDOW

VMEM_LIMIT = 56 * 1024 * 1024


def _rms_rows(x, gain_row):
    ms = jnp.mean(x * x, axis=-1, keepdims=True)
    return x * lax.rsqrt(ms + NORM_EPS) * gain_row


def _rms_cols(xt, gain_col):
    ms = jnp.mean(xt * xt, axis=0, keepdims=True)
    return xt * lax.rsqrt(ms + NORM_EPS) * gain_col


def _ffn_kernel(x_ref, pre_ref, post_ref, wgu_ref, wd_ref, o_ref, xn_ref, acc_ref):
    x = x_ref[...]
    xn_ref[...] = _rms_rows(x, pre_ref[...]).astype(jnp.bfloat16)
    acc_ref[...] = jnp.zeros_like(acc_ref)

    def chunk(c, carry):
        gu = jnp.dot(xn_ref[...], wgu_ref[c], preferred_element_type=jnp.float32)
        g = gu[:, :FF_CHUNK]
        u = gu[:, FF_CHUNK:]
        h = (g * jax.nn.sigmoid(g) * u).astype(jnp.bfloat16)
        acc_ref[...] += jnp.dot(h, wd_ref[c], preferred_element_type=jnp.float32)
        return carry

    lax.fori_loop(0, N_FF_CHUNKS, chunk, 0)
    o_ref[...] = x + FFN_RESID * _rms_rows(acc_ref[...], post_ref[...])


def _ffn(x, pre, post, wgu, wd):
    t = x.shape[0]
    const3 = lambda i: (0, 0, 0)
    return pl.pallas_call(
        _ffn_kernel,
        name="ffn",
        grid=(t // TOKEN_TILE,),
        in_specs=[
            pl.BlockSpec((TOKEN_TILE, D_MODEL), lambda i: (i, 0)),
            pl.BlockSpec((1, D_MODEL), lambda i: (0, 0)),
            pl.BlockSpec((1, D_MODEL), lambda i: (0, 0)),
            pl.BlockSpec((N_FF_CHUNKS, D_MODEL, 2 * FF_CHUNK), const3),
            pl.BlockSpec((N_FF_CHUNKS, FF_CHUNK, D_MODEL), const3),
        ],
        out_specs=pl.BlockSpec((TOKEN_TILE, D_MODEL), lambda i: (i, 0)),
        out_shape=jax.ShapeDtypeStruct((t, D_MODEL), jnp.float32),
        scratch_shapes=[
            pltpu.VMEM((TOKEN_TILE, D_MODEL), jnp.bfloat16),
            pltpu.VMEM((TOKEN_TILE, D_MODEL), jnp.float32),
        ],
        compiler_params=pltpu.CompilerParams(
            dimension_semantics=("arbitrary",), vmem_limit_bytes=VMEM_LIMIT),
    )(x, pre, post, wgu, wd)


def _rope_norm_head(xt, gain_col, cos_t, sin_t):
    y = _rms_cols(xt, gain_col)
    x0 = y[:HALF_DIM]
    x1 = y[HALF_DIM:]
    return jnp.concatenate([x0 * cos_t - x1 * sin_t, x0 * sin_t + x1 * cos_t], axis=0)


def _mix_in_kernel(x_ref, pre_ref, w_ref, gq_ref, gk_ref, cos_ref, sin_ref,
                   qa_ref, qb_ref, k_ref, v_ref, kn_ref):
    xn = _rms_rows(x_ref[0], pre_ref[...]).astype(jnp.bfloat16)
    pt = lax.dot_general(w_ref[...], xn, (((1,), (1,)), ((), ())),
                         preferred_element_type=jnp.float32)
    cos_t = cos_ref[...]
    sin_t = sin_ref[...]
    gq = gq_ref[...]
    gk = gk_ref[...]
    off = 0
    for h in range(Q_HEADS):
        r = _rope_norm_head(pt[off:off + HEAD_DIM], gq, cos_t, sin_t)
        qa_ref[0, h * HEAD_DIM:(h + 1) * HEAD_DIM, :] = (r * Q_SCALE).astype(jnp.bfloat16)
        off += HEAD_DIM
    for h in range(KV_HEADS):
        r = _rope_norm_head(pt[off:off + HEAD_DIM], gk, cos_t, sin_t)
        k_ref[0, h] = r.T.astype(jnp.bfloat16)
        kn_ref[0, h] = jnp.sum(r * r, axis=0, keepdims=True)
        off += HEAD_DIM
    v_ref[0, 0, :KV_WIDTH, :] = pt[off:off + KV_WIDTH].astype(jnp.bfloat16)
    off += KV_WIDTH
    qb_ref[0] = (pt[off:off + GROUP_WIDTH] * Q_SCALE).astype(jnp.bfloat16)
    off += GROUP_WIDTH
    for h in range(KV_HEADS):
        k_ref[0, KV_HEADS + h] = pt[off:off + HEAD_DIM].T.astype(jnp.bfloat16)
        off += HEAD_DIM
    v_ref[0, 0, KV_WIDTH:, :] = pt[off:off + KV_WIDTH].astype(jnp.bfloat16)


def _mix_in(x, pre, w_in_t, gq, gk, cos_t, sin_t):
    b, s, _ = x.shape
    n_tiles = s // TOKEN_TILE
    return pl.pallas_call(
        _mix_in_kernel,
        name="mix_in",
        grid=(b, n_tiles),
        in_specs=[
            pl.BlockSpec((1, TOKEN_TILE, D_MODEL), lambda bi, i: (bi, i, 0)),
            pl.BlockSpec((1, D_MODEL), lambda bi, i: (0, 0)),
            pl.BlockSpec((IN_WIDTH, D_MODEL), lambda bi, i: (0, 0)),
            pl.BlockSpec((HEAD_DIM, 1), lambda bi, i: (0, 0)),
            pl.BlockSpec((HEAD_DIM, 1), lambda bi, i: (0, 0)),
            pl.BlockSpec((HALF_DIM, TOKEN_TILE), lambda bi, i: (0, i)),
            pl.BlockSpec((HALF_DIM, TOKEN_TILE), lambda bi, i: (0, i)),
        ],
        out_specs=[
            pl.BlockSpec((1, GROUP_WIDTH, TOKEN_TILE), lambda bi, i: (bi, 0, i)),
            pl.BlockSpec((1, GROUP_WIDTH, TOKEN_TILE), lambda bi, i: (bi, 0, i)),
            pl.BlockSpec((1, 2 * KV_HEADS, TOKEN_TILE, HEAD_DIM), lambda bi, i: (bi, 0, i, 0)),
            pl.BlockSpec((1, 1, 2 * KV_WIDTH, V_CHUNK), lambda bi, i: (bi, i, 0, 0)),
            pl.BlockSpec((1, KV_HEADS, 1, TOKEN_TILE), lambda bi, i: (bi, 0, 0, i)),
        ],
        out_shape=[
            jax.ShapeDtypeStruct((b, GROUP_WIDTH, s), jnp.bfloat16),
            jax.ShapeDtypeStruct((b, GROUP_WIDTH, s), jnp.bfloat16),
            jax.ShapeDtypeStruct((b, 2 * KV_HEADS, s, HEAD_DIM), jnp.bfloat16),
            jax.ShapeDtypeStruct((b, n_tiles, 2 * KV_WIDTH, V_CHUNK), jnp.bfloat16),
            jax.ShapeDtypeStruct((b, KV_HEADS, 1, s), jnp.float32),
        ],
        compiler_params=pltpu.CompilerParams(
            dimension_semantics=("arbitrary", "arbitrary"), vmem_limit_bytes=VMEM_LIMIT),
    )(x, pre, w_in_t, gq, gk, cos_t, sin_t)


def _global_kernel(q_ref, k_ref, v_ref, kn_ref, o_ref, s_ref, smax_ref, p_ref, alpha_ref, m_ref,
                   c_ref, acc_ref, *, n_chunks):
    ones = jnp.ones((ONES_ROWS, V_CHUNK), jnp.bfloat16)
    n_blocks = V_CHUNK // ROW_BLOCK
    acc_ref[...] = jnp.zeros_like(acc_ref)

    kmax2 = jnp.max(kn_ref[0, 0], axis=1, keepdims=True)
    for h in range(HEADS_PER_KV):
        qf = q_ref[0, h * HEAD_DIM:(h + 1) * HEAD_DIM, :].astype(jnp.float32)
        c_ref[h] = BOUND_SLACK * jnp.sqrt(jnp.sum(qf * qf, axis=0, keepdims=True) * kmax2)
    use_fixed_shift = jnp.max(c_ref[...]) <= MAX_FIXED_SHIFT

    def k_rows(chunk, r):
        start = chunk * V_CHUNK + r * ROW_BLOCK
        if not isinstance(start, int):
            start = pl.multiple_of(start, ROW_BLOCK)
        return k_ref[0, 0, pl.ds(start, ROW_BLOCK), :]

    def v_ext(chunk):
        return jnp.concatenate([v_ref[0, chunk], ones], axis=0)

    def values_block(vext, slot, h, r):
        cols = slice((r - 1) * ROW_BLOCK, (r + 1) * ROW_BLOCK)
        return jnp.dot(vext[:, cols], p_ref[slot, h, cols, :], preferred_element_type=jnp.float32)

    def fixed_step(j, parity, do_scores=True, do_values=True):
        cur, other = parity, 1 - parity
        if do_values:
            vext = v_ext(j)
        for h in range(HEADS_PER_KV):
            q_h = q_ref[0, h * HEAD_DIM:(h + 1) * HEAD_DIM, :]
            c_h = c_ref[h]
            pv = None
            for r in range(n_blocks):
                rows = slice(r * ROW_BLOCK, (r + 1) * ROW_BLOCK)
                if do_scores:
                    st = jnp.dot(k_rows(j + 1, r), q_h, preferred_element_type=jnp.float32)
                    p_ref[other, h, rows, :] = jnp.exp2(st - c_h).astype(jnp.bfloat16)
                if do_values and r % 2 == 1:
                    d = values_block(vext, cur, h, r)
                    pv = d if pv is None else pv + d
            if do_values:
                acc_ref[h] += pv

    def online_step(j, parity, do_scores=True, do_softmax=True, do_values=True):
        cur, other = parity, 1 - parity
        if do_values:
            vext = v_ext(j - 1)
        for h in range(HEADS_PER_KV):
            q_h = q_ref[0, h * HEAD_DIM:(h + 1) * HEAD_DIM, :]
            if do_softmax:
                m_old = m_ref[h]
                m_new = jnp.maximum(m_old, smax_ref[cur, h])
                alpha_ref[cur, h] = jnp.exp2(m_old - m_new)
                m_ref[h] = m_new
            blk_max = None
            pv = None
            for r in range(n_blocks):
                rows = slice(r * ROW_BLOCK, (r + 1) * ROW_BLOCK)
                if do_scores:
                    st = jnp.dot(k_rows(j + 1, r), q_h, preferred_element_type=jnp.float32)
                    s_ref[other, h, rows, :] = st
                    part = jnp.max(st.reshape(ROW_BLOCK // 8, 8, Q_TILE), axis=0)
                    blk_max = part if blk_max is None else jnp.maximum(blk_max, part)
                if do_softmax:
                    p_ref[cur, h, rows, :] = jnp.exp2(s_ref[cur, h, rows, :] - m_new).astype(
                        jnp.bfloat16)
                if do_values and r % 2 == 1:
                    d = values_block(vext, other, h, r)
                    pv = d if pv is None else pv + d
            if do_scores:
                smax_ref[other, h] = jnp.max(blk_max, axis=0, keepdims=True)
            if do_values:
                acc_ref[h] = alpha_ref[other, h] * acc_ref[h] + pv

    @pl.when(use_fixed_shift)
    def _():
        fixed_step(-1, 1, do_values=False)

        def pair(t, carry):
            fixed_step(2 * t, 0)
            fixed_step(2 * t + 1, 1)
            return carry

        lax.fori_loop(0, (n_chunks - 2) // 2, pair, 0)
        fixed_step(n_chunks - 2, 0)
        fixed_step(n_chunks - 1, 1, do_scores=False)

    @pl.when(jnp.logical_not(use_fixed_shift))
    def _():
        m_ref[...] = jnp.full_like(m_ref, NEG_BIG)
        online_step(-1, 1, do_softmax=False, do_values=False)
        online_step(0, 0, do_values=False)

        def pair(t, carry):
            online_step(2 * t + 1, 1)
            online_step(2 * t + 2, 0)
            return carry

        lax.fori_loop(0, (n_chunks - 2) // 2, pair, 0)
        online_step(n_chunks - 1, 1, do_scores=False)
        online_step(n_chunks, 0, do_scores=False, do_softmax=False)

    for h in range(HEADS_PER_KV):
        a = acc_ref[h]
        o_ref[0, h * HEAD_DIM:(h + 1) * HEAD_DIM, :] = a[:HEAD_DIM] / a[HEAD_DIM:HEAD_DIM + 1]


def _global_attention(q_t, k, v_t, k_norm2):
    b, _, s = q_t.shape
    n_chunks = s // V_CHUNK
    assert n_chunks >= 2 and n_chunks % 2 == 0, "the kv pipeline runs chunks in pairs"
    return pl.pallas_call(
        functools.partial(_global_kernel, n_chunks=n_chunks),
        name="global_attn",
        grid=(b, KV_HEADS, s // Q_TILE),
        in_specs=[
            pl.BlockSpec((1, Q_PER_KV_ROWS, Q_TILE), lambda bi, g, i: (bi, g, i)),
            pl.BlockSpec((1, 1, s, HEAD_DIM), lambda bi, g, i: (bi, g, 0, 0)),
            pl.BlockSpec((1, n_chunks, HEAD_DIM, V_CHUNK), lambda bi, g, i: (bi, 0, g, 0)),
            pl.BlockSpec((1, 1, 1, s), lambda bi, g, i: (bi, g, 0, 0)),
        ],
        out_specs=pl.BlockSpec((1, Q_PER_KV_ROWS, Q_TILE), lambda bi, g, i: (bi, g, i)),
        out_shape=jax.ShapeDtypeStruct((b, GROUP_WIDTH, s), jnp.float32),
        scratch_shapes=[
            pltpu.VMEM((2, HEADS_PER_KV, V_CHUNK, Q_TILE), jnp.float32),
            pltpu.VMEM((2, HEADS_PER_KV, 1, Q_TILE), jnp.float32),
            pltpu.VMEM((2, HEADS_PER_KV, V_CHUNK, Q_TILE), jnp.bfloat16),
            pltpu.VMEM((2, HEADS_PER_KV, 1, Q_TILE), jnp.float32),
            pltpu.VMEM((HEADS_PER_KV, 1, Q_TILE), jnp.float32),
            pltpu.VMEM((HEADS_PER_KV, 1, Q_TILE), jnp.float32),
            pltpu.VMEM((HEADS_PER_KV, HEAD_DIM + ONES_ROWS, Q_TILE), jnp.float32),
        ],
        compiler_params=pltpu.CompilerParams(
            dimension_semantics=("arbitrary", "arbitrary", "arbitrary"),
            vmem_limit_bytes=VMEM_LIMIT),
    )(q_t, k, v_t, k_norm2)


def _window_kernel(q_ref, kp_ref, kc_ref, kn_ref, vp_ref, vc_ref, vn_ref, bias_ref, sink_ref,
                   o_ref):
    i = pl.program_id(2)
    has_prev = i > 0
    has_next = i < pl.num_programs(2) - 1
    kcat = jnp.concatenate([kp_ref[0, 0], kc_ref[0, 0], kn_ref[0, 0]], axis=0)
    vcat = jnp.concatenate([vp_ref[0, 0], vc_ref[0, 0], vn_ref[0, 0]], axis=1)
    bias = bias_ref[0]
    sink = sink_ref[0]
    for sb in range(N_SUB):
        lo = sb * WINDOW
        ks = kcat[lo:lo + BAND]
        vs = vcat[:, lo:lo + BAND]
        qs = jnp.concatenate(
            [q_ref[0, h * HEAD_DIM:(h + 1) * HEAD_DIM, lo:lo + WINDOW] for h in range(HEADS_PER_KV)],
            axis=1)
        st = jnp.dot(ks, qs, preferred_element_type=jnp.float32) + bias
        if sb == 0:
            row = lax.broadcasted_iota(jnp.int32, st.shape, 0)
            st = jnp.where(jnp.logical_or(has_prev, row >= WINDOW), st, NEG_BIG)
        if sb == N_SUB - 1:
            row = lax.broadcasted_iota(jnp.int32, st.shape, 0)
            st = jnp.where(jnp.logical_or(has_next, row < 2 * WINDOW), st, NEG_BIG)
        mx = jnp.maximum(jnp.max(st, axis=0, keepdims=True), sink)
        e = jnp.exp2(st - mx)
        denom = jnp.sum(e, axis=0, keepdims=True) + jnp.exp2(sink - mx)
        ot = jnp.dot(vs, e.astype(jnp.bfloat16), preferred_element_type=jnp.float32) / denom
        for h in range(HEADS_PER_KV):
            o_ref[0, h * HEAD_DIM:(h + 1) * HEAD_DIM, lo:lo + WINDOW] = (
                ot[:, h * WINDOW:(h + 1) * WINDOW])


def _window_attention(q_t, k, v_t, bias, sink):
    b, _, s = q_t.shape
    n_blk = s // WINDOW
    per = WQ_TILE // WINDOW
    vper = V_CHUNK // WINDOW

    def prev_blk(i):
        return jnp.maximum(i * per - 1, 0)

    def next_blk(i):
        return jnp.minimum(i * per + per, n_blk - 1)

    kh = lambda g: KV_HEADS + g
    return pl.pallas_call(
        _window_kernel,
        name="window_attn",
        grid=(b, KV_HEADS, s // WQ_TILE),
        in_specs=[
            pl.BlockSpec((1, Q_PER_KV_ROWS, WQ_TILE), lambda bi, g, i: (bi, g, i)),
            pl.BlockSpec((1, 1, WINDOW, HEAD_DIM), lambda bi, g, i: (bi, kh(g), prev_blk(i), 0)),
            pl.BlockSpec((1, 1, WQ_TILE, HEAD_DIM), lambda bi, g, i: (bi, kh(g), i, 0)),
            pl.BlockSpec((1, 1, WINDOW, HEAD_DIM), lambda bi, g, i: (bi, kh(g), next_blk(i), 0)),
            pl.BlockSpec((1, 1, HEAD_DIM, WINDOW),
                         lambda bi, g, i: (bi, prev_blk(i) // vper, kh(g), prev_blk(i) % vper)),
            pl.BlockSpec((1, 1, HEAD_DIM, WQ_TILE), lambda bi, g, i: (bi, i, kh(g), 0)),
            pl.BlockSpec((1, 1, HEAD_DIM, WINDOW),
                         lambda bi, g, i: (bi, next_blk(i) // vper, kh(g), next_blk(i) % vper)),
            pl.BlockSpec((1, BAND, HEADS_PER_KV * WINDOW), lambda bi, g, i: (g, 0, 0)),
            pl.BlockSpec((1, 1, HEADS_PER_KV * WINDOW), lambda bi, g, i: (g, 0, 0)),
        ],
        out_specs=pl.BlockSpec((1, Q_PER_KV_ROWS, WQ_TILE), lambda bi, g, i: (bi, g, i)),
        out_shape=jax.ShapeDtypeStruct((b, GROUP_WIDTH, s), jnp.float32),
        compiler_params=pltpu.CompilerParams(
            dimension_semantics=("arbitrary", "arbitrary", "arbitrary"),
            vmem_limit_bytes=VMEM_LIMIT),
    )(q_t, k, k, k, v_t, v_t, v_t, bias, sink)


def _mix_out_kernel(x_ref, oa_ref, ob_ref, ga_ref, gb_ref, w_ref, post_ref, o_ref):
    na = _rms_cols(oa_ref[0], ga_ref[...])
    nb = _rms_cols(ob_ref[0], gb_ref[...])
    mt = jnp.concatenate([na, nb], axis=0).astype(jnp.bfloat16)
    ht = jnp.dot(w_ref[...], mt, preferred_element_type=jnp.float32)
    ms = jnp.mean(ht * ht, axis=0, keepdims=True)
    yt = ht * lax.rsqrt(ms + NORM_EPS)
    o_ref[0] = x_ref[0] + yt.T * post_ref[...]


def _mix_out(x, oa_t, ob_t, ga, gb, w_out_t, post):
    b, s, _ = x.shape
    return pl.pallas_call(
        _mix_out_kernel,
        name="mix_out",
        grid=(b, s // TOKEN_TILE),
        in_specs=[
            pl.BlockSpec((1, TOKEN_TILE, D_MODEL), lambda bi, i: (bi, i, 0)),
            pl.BlockSpec((1, GROUP_WIDTH, TOKEN_TILE), lambda bi, i: (bi, 0, i)),
            pl.BlockSpec((1, GROUP_WIDTH, TOKEN_TILE), lambda bi, i: (bi, 0, i)),
            pl.BlockSpec((GROUP_WIDTH, 1), lambda bi, i: (0, 0)),
            pl.BlockSpec((GROUP_WIDTH, 1), lambda bi, i: (0, 0)),
            pl.BlockSpec((D_MODEL, D_MODEL), lambda bi, i: (0, 0)),
            pl.BlockSpec((1, D_MODEL), lambda bi, i: (0, 0)),
        ],
        out_specs=pl.BlockSpec((1, TOKEN_TILE, D_MODEL), lambda bi, i: (bi, i, 0)),
        out_shape=jax.ShapeDtypeStruct((b, s, D_MODEL), jnp.float32),
        compiler_params=pltpu.CompilerParams(
            dimension_semantics=("arbitrary", "arbitrary"), vmem_limit_bytes=VMEM_LIMIT),
    )(x, oa_t, ob_t, ga, gb, w_out_t, post)


_DEINTERLEAVE = np.concatenate([np.arange(0, HEAD_DIM, 2), np.arange(1, HEAD_DIM, 2)])


def _w_in_rows():
    n_rot = (Q_HEADS + KV_HEADS) * HEAD_DIM
    rot = (np.arange(Q_HEADS + KV_HEADS)[:, None] * HEAD_DIM + _DEINTERLEAVE[None, :]).reshape(-1)
    return np.concatenate([rot, np.arange(n_rot, IN_WIDTH)])


def _rope_tables_t(seq_len):
    n_rows = seq_len // GRID_W
    row = jnp.repeat(jnp.arange(n_rows, dtype=jnp.float32), GRID_W)
    col = jnp.tile(jnp.arange(GRID_W, dtype=jnp.float32), n_rows)
    n_freq = HEAD_DIM // 4
    inv_freq = ROPE_THETA ** (-jnp.arange(n_freq, dtype=jnp.float32) / n_freq)
    ang = jnp.concatenate([inv_freq[:, None] * row[None, :], inv_freq[:, None] * col[None, :]], axis=0)
    return jnp.cos(ang), jnp.sin(ang)


def _window_bias():
    slopes = 2.0 ** (-8.0 * np.arange(1, Q_HEADS + 1, dtype=np.float64) / Q_HEADS)
    kpos = np.arange(BAND)[:, None] - WINDOW
    qpos = np.arange(WINDOW)[None, :]
    dist = np.abs(qpos - kpos).astype(np.float64)
    per_head = np.where(dist[None] <= WINDOW, -slopes[:, None, None] * dist[None] * LOG2E, NEG_BIG)
    per_kv = per_head.reshape(KV_HEADS, HEADS_PER_KV, BAND, WINDOW).transpose(0, 2, 1, 3)
    return jnp.asarray(per_kv.reshape(KV_HEADS, BAND, HEADS_PER_KV * WINDOW), jnp.float32)


def _ffn_weights(w_gate, w_up, w_down):
    wg = w_gate.reshape(D_MODEL, N_FF_CHUNKS, FF_CHUNK)
    wu = w_up.reshape(D_MODEL, N_FF_CHUNKS, FF_CHUNK)
    wgu = jnp.concatenate([wg, wu], axis=-1).transpose(1, 0, 2).astype(jnp.bfloat16)
    wd = w_down.reshape(N_FF_CHUNKS, FF_CHUNK, D_MODEL).astype(jnp.bfloat16)
    return wgu, wd


def _trunk(x, layers, bias):
    b, s, _ = x.shape
    cos_t, sin_t = _rope_tables_t(s)
    for p in layers:
        x = _ffn(x.reshape(b * s, D_MODEL), p["ffn1_pre"], p["ffn1_post"], p["ffn1_wgu"],
                 p["ffn1_wd"]).reshape(b, s, D_MODEL)
        qa_t, qb_t, k, v_t, k_norm2 = _mix_in(x, p["mix_pre"], p["w_in_t"], p["gq"], p["gk"],
                                              cos_t, sin_t)
        oa_t = _global_attention(qa_t, k, v_t, k_norm2)
        ob_t = _window_attention(qb_t, k, v_t, bias, p["sink"])
        x = _mix_out(x, oa_t, ob_t, p["ga"], p["gb"], p["w_out_t"], p["mix_post"])
        x = _ffn(x.reshape(b * s, D_MODEL), p["ffn2_pre"], p["ffn2_post"], p["ffn2_wgu"],
                 p["ffn2_wd"]).reshape(b, s, D_MODEL)
    return x


def kernel(x_prompt, x_sample, ffn1_pre, ffn1_post, ffn1_w_gate, ffn1_w_up, ffn1_w_down, mix_pre, mix_post, w_in, a_q_norm, a_k_norm, b_sink, a_out_norm, b_out_norm, w_out, ffn2_pre, ffn2_post, ffn2_w_gate, ffn2_w_up, ffn2_w_down):
    depth = w_in.shape[0]
    rows = _w_in_rows()
    layers = []
    for l in range(depth):
        wgu1, wd1 = _ffn_weights(ffn1_w_gate[l], ffn1_w_up[l], ffn1_w_down[l])
        wgu2, wd2 = _ffn_weights(ffn2_w_gate[l], ffn2_w_up[l], ffn2_w_down[l])
        sink = jnp.repeat(b_sink[l].astype(jnp.float32) * LOG2E, WINDOW)
        layers.append(dict(
            ffn1_pre=ffn1_pre[l][None, :], ffn1_post=ffn1_post[l][None, :], ffn1_wgu=wgu1, ffn1_wd=wd1,
            ffn2_pre=ffn2_pre[l][None, :], ffn2_post=ffn2_post[l][None, :], ffn2_wgu=wgu2, ffn2_wd=wd2,
            mix_pre=mix_pre[l][None, :], mix_post=mix_post[l][None, :],
            w_in_t=w_in[l].T[rows].astype(jnp.bfloat16),
            gq=a_q_norm[l][_DEINTERLEAVE][:, None], gk=a_k_norm[l][_DEINTERLEAVE][:, None],
            sink=sink.reshape(KV_HEADS, 1, HEADS_PER_KV * WINDOW),
            ga=a_out_norm[l][:, None], gb=b_out_norm[l][:, None],
            w_out_t=w_out[l].T.astype(jnp.bfloat16),
        ))
    bias = _window_bias()
    return (_trunk(x_prompt, layers, bias), _trunk(x_sample, layers, bias))
```

```python
import functools
import math

import jax
import jax.numpy as jnp
import numpy as np
from jax import lax
from jax.experimental import pallas as pl
from jax.experimental.pallas import tpu as pltpu

D_MODEL = 1024
HEAD_DIM = 64
HALF_DIM = HEAD_DIM // 2
Q_HEADS = 8
KV_HEADS = 2
HEADS_PER_KV = Q_HEADS // KV_HEADS
GROUP_WIDTH = Q_HEADS * HEAD_DIM
KV_WIDTH = KV_HEADS * HEAD_DIM
Q_PER_KV_ROWS = HEADS_PER_KV * HEAD_DIM
IN_WIDTH = 2 * (GROUP_WIDTH + 2 * KV_WIDTH)
D_FF = 2816
GRID_W = 64
ROPE_THETA = 10000.0
WINDOW = 128
NORM_EPS = 1e-6
FFN_RESID = 0.5
LOG2E = math.log2(math.e)
Q_SCALE = HEAD_DIM ** -0.5 * LOG2E
NEG_BIG = -1e30
BOUND_SLACK = 1.01
MAX_FIXED_SHIFT = 48.0

FF_CHUNK = 256
N_FF_CHUNKS = D_FF // FF_CHUNK
TOKEN_TILE = 512
V_CHUNK = TOKEN_TILE
Q_TILE = 512
ONES_ROWS = 16
ROW_BLOCK = 128
FIXED_UNROLL = 4
WQ_TILE = 512
N_SUB = WQ_TILE // WINDOW
BAND = 3 * WINDOW

VMEM_LIMIT = 56 * 1024 * 1024


def _rms_rows(x, gain_row):
    ms = jnp.mean(x * x, axis=-1, keepdims=True)
    return x * lax.rsqrt(ms + NORM_EPS) * gain_row


def _rms_cols(xt, gain_col):
    ms = jnp.mean(xt * xt, axis=0, keepdims=True)
    return xt * lax.rsqrt(ms + NORM_EPS) * gain_col


def _ffn_kernel(x_ref, pre_ref, post_ref, wgu_ref, wd_ref, o_ref, xn_ref, acc_ref):
    x = x_ref[...]
    xn_ref[...] = _rms_rows(x, pre_ref[...]).astype(jnp.bfloat16)
    acc_ref[...] = jnp.zeros_like(acc_ref)

    for c in range(N_FF_CHUNKS):
        gu = jnp.dot(xn_ref[...], wgu_ref[c], preferred_element_type=jnp.float32)
        g = gu[:, :FF_CHUNK]
        u = gu[:, FF_CHUNK:]
        h = (g * jax.nn.sigmoid(g) * u).astype(jnp.bfloat16)
        acc_ref[...] += jnp.dot(h, wd_ref[c], preferred_element_type=jnp.float32)
    o_ref[...] = x + FFN_RESID * _rms_rows(acc_ref[...], post_ref[...])


def _ffn(x, pre, post, wgu, wd):
    t = x.shape[0]
    const3 = lambda i: (0, 0, 0)
    return pl.pallas_call(
        _ffn_kernel,
        name="ffn",
        grid=(t // TOKEN_TILE,),
        in_specs=[
            pl.BlockSpec((TOKEN_TILE, D_MODEL), lambda i: (i, 0)),
            pl.BlockSpec((1, D_MODEL), lambda i: (0, 0)),
            pl.BlockSpec((1, D_MODEL), lambda i: (0, 0)),
            pl.BlockSpec((N_FF_CHUNKS, D_MODEL, 2 * FF_CHUNK), const3),
            pl.BlockSpec((N_FF_CHUNKS, FF_CHUNK, D_MODEL), const3),
        ],
        out_specs=pl.BlockSpec((TOKEN_TILE, D_MODEL), lambda i: (i, 0)),
        out_shape=jax.ShapeDtypeStruct((t, D_MODEL), jnp.float32),
        scratch_shapes=[
            pltpu.VMEM((TOKEN_TILE, D_MODEL), jnp.bfloat16),
            pltpu.VMEM((TOKEN_TILE, D_MODEL), jnp.float32),
        ],
        compiler_params=pltpu.CompilerParams(
            dimension_semantics=("arbitrary",), vmem_limit_bytes=VMEM_LIMIT),
    )(x, pre, post, wgu, wd)


def _rope_norm_head(xt, gain_col, cos_t, sin_t):
    y = _rms_cols(xt, gain_col)
    x0 = y[:HALF_DIM]
    x1 = y[HALF_DIM:]
    return jnp.concatenate([x0 * cos_t - x1 * sin_t, x0 * sin_t + x1 * cos_t], axis=0)


def _mix_in_kernel(x_ref, pre_ref, w_ref, gq_ref, gk_ref, cos_ref, sin_ref,
                   qa_ref, qb_ref, k_ref, v_ref, kn_ref):
    xn = _rms_rows(x_ref[0], pre_ref[...]).astype(jnp.bfloat16)
    pt = lax.dot_general(w_ref[...], xn, (((1,), (1,)), ((), ())),
                         preferred_element_type=jnp.float32)
    cos_t = cos_ref[...]
    sin_t = sin_ref[...]
    gq = gq_ref[...]
    gk = gk_ref[...]
    off = 0
    for h in range(Q_HEADS):
        r = _rope_norm_head(pt[off:off + HEAD_DIM], gq, cos_t, sin_t)
        qa_ref[0, h * HEAD_DIM:(h + 1) * HEAD_DIM, :] = (r * Q_SCALE).astype(jnp.bfloat16)
        off += HEAD_DIM
    for h in range(KV_HEADS):
        r = _rope_norm_head(pt[off:off + HEAD_DIM], gk, cos_t, sin_t)
        k_ref[0, h] = r.T.astype(jnp.bfloat16)
        kn_ref[0, h] = jnp.sum(r * r, axis=0, keepdims=True)
        off += HEAD_DIM
    v_ref[0, 0, :KV_WIDTH, :] = pt[off:off + KV_WIDTH].astype(jnp.bfloat16)
    off += KV_WIDTH
    qb_ref[0] = (pt[off:off + GROUP_WIDTH] * Q_SCALE).astype(jnp.bfloat16)
    off += GROUP_WIDTH
    for h in range(KV_HEADS):
        k_ref[0, KV_HEADS + h] = pt[off:off + HEAD_DIM].T.astype(jnp.bfloat16)
        off += HEAD_DIM
    v_ref[0, 0, KV_WIDTH:, :] = pt[off:off + KV_WIDTH].astype(jnp.bfloat16)


def _mix_in(x, pre, w_in_t, gq, gk, cos_t, sin_t):
    b, s, _ = x.shape
    n_tiles = s // TOKEN_TILE
    return pl.pallas_call(
        _mix_in_kernel,
        name="mix_in",
        grid=(b, n_tiles),
        in_specs=[
            pl.BlockSpec((1, TOKEN_TILE, D_MODEL), lambda bi, i: (bi, i, 0)),
            pl.BlockSpec((1, D_MODEL), lambda bi, i: (0, 0)),
            pl.BlockSpec((IN_WIDTH, D_MODEL), lambda bi, i: (0, 0)),
            pl.BlockSpec((HEAD_DIM, 1), lambda bi, i: (0, 0)),
            pl.BlockSpec((HEAD_DIM, 1), lambda bi, i: (0, 0)),
            pl.BlockSpec((HALF_DIM, TOKEN_TILE), lambda bi, i: (0, i)),
            pl.BlockSpec((HALF_DIM, TOKEN_TILE), lambda bi, i: (0, i)),
        ],
        out_specs=[
            pl.BlockSpec((1, GROUP_WIDTH, TOKEN_TILE), lambda bi, i: (bi, 0, i)),
            pl.BlockSpec((1, GROUP_WIDTH, TOKEN_TILE), lambda bi, i: (bi, 0, i)),
            pl.BlockSpec((1, 2 * KV_HEADS, TOKEN_TILE, HEAD_DIM), lambda bi, i: (bi, 0, i, 0)),
            pl.BlockSpec((1, 1, 2 * KV_WIDTH, V_CHUNK), lambda bi, i: (bi, i, 0, 0)),
            pl.BlockSpec((1, KV_HEADS, 1, TOKEN_TILE), lambda bi, i: (bi, 0, 0, i)),
        ],
        out_shape=[
            jax.ShapeDtypeStruct((b, GROUP_WIDTH, s), jnp.bfloat16),
            jax.ShapeDtypeStruct((b, GROUP_WIDTH, s), jnp.bfloat16),
            jax.ShapeDtypeStruct((b, 2 * KV_HEADS, s, HEAD_DIM), jnp.bfloat16),
            jax.ShapeDtypeStruct((b, n_tiles, 2 * KV_WIDTH, V_CHUNK), jnp.bfloat16),
            jax.ShapeDtypeStruct((b, KV_HEADS, 1, s), jnp.float32),
        ],
        compiler_params=pltpu.CompilerParams(
            dimension_semantics=("arbitrary", "arbitrary"), vmem_limit_bytes=VMEM_LIMIT),
    )(x, pre, w_in_t, gq, gk, cos_t, sin_t)


def _global_kernel(q_ref, k_ref, v_ref, kn_ref, o_ref, s_ref, smax_ref, p_ref, alpha_ref, m_ref,
                   c_ref, acc_ref, *, n_chunks):
    ones = jnp.ones((ONES_ROWS, V_CHUNK), jnp.bfloat16)
    n_blocks = V_CHUNK // ROW_BLOCK
    acc_ref[...] = jnp.zeros_like(acc_ref)

    kmax2 = jnp.max(kn_ref[0, 0], axis=1, keepdims=True)
    for h in range(HEADS_PER_KV):
        qf = q_ref[0, h * HEAD_DIM:(h + 1) * HEAD_DIM, :].astype(jnp.float32)
        c_ref[h] = BOUND_SLACK * jnp.sqrt(jnp.sum(qf * qf, axis=0, keepdims=True) * kmax2)
    use_fixed_shift = jnp.max(c_ref[...]) <= MAX_FIXED_SHIFT

    def k_rows(chunk, r):
        start = chunk * V_CHUNK + r * ROW_BLOCK
        if not isinstance(start, int):
            start = pl.multiple_of(start, ROW_BLOCK)
        return k_ref[0, 0, pl.ds(start, ROW_BLOCK), :]

    def v_ext(chunk):
        return jnp.concatenate([v_ref[0, chunk], ones], axis=0)

    def values_block(vext, slot, h, r):
        cols = slice((r - 1) * ROW_BLOCK, (r + 1) * ROW_BLOCK)
        return jnp.dot(vext[:, cols], p_ref[slot, h, cols, :], preferred_element_type=jnp.float32)

    def fixed_step(j, parity, do_scores=True, do_values=True):
        cur, other = parity, 1 - parity
        if do_values:
            vext = v_ext(j)
        for h in range(HEADS_PER_KV):
            q_h = q_ref[0, h * HEAD_DIM:(h + 1) * HEAD_DIM, :]
            c_h = c_ref[h]
            pv = None
            for r in range(n_blocks):
                rows = slice(r * ROW_BLOCK, (r + 1) * ROW_BLOCK)
                if do_scores:
                    st = jnp.dot(k_rows(j + 1, r), q_h, preferred_element_type=jnp.float32)
                    p_ref[other, h, rows, :] = jnp.exp2(st - c_h).astype(jnp.bfloat16)
                if do_values and r % 2 == 1:
                    d = values_block(vext, cur, h, r)
                    pv = d if pv is None else pv + d
            if do_values:
                acc_ref[h] += pv

    def online_step(j, parity, do_scores=True, do_softmax=True, do_values=True):
        cur, other = parity, 1 - parity
        if do_values:
            vext = v_ext(j - 1)
        for h in range(HEADS_PER_KV):
            q_h = q_ref[0, h * HEAD_DIM:(h + 1) * HEAD_DIM, :]
            if do_softmax:
                m_old = m_ref[h]
                m_new = jnp.maximum(m_old, smax_ref[cur, h])
                alpha_ref[cur, h] = jnp.exp2(m_old - m_new)
                m_ref[h] = m_new
            blk_max = None
            pv = None
            for r in range(n_blocks):
                rows = slice(r * ROW_BLOCK, (r + 1) * ROW_BLOCK)
                if do_scores:
                    st = jnp.dot(k_rows(j + 1, r), q_h, preferred_element_type=jnp.float32)
                    s_ref[other, h, rows, :] = st
                    part = jnp.max(st.reshape(ROW_BLOCK // 8, 8, Q_TILE), axis=0)
                    blk_max = part if blk_max is None else jnp.maximum(blk_max, part)
                if do_softmax:
                    p_ref[cur, h, rows, :] = jnp.exp2(s_ref[cur, h, rows, :] - m_new).astype(
                        jnp.bfloat16)
                if do_values and r % 2 == 1:
                    d = values_block(vext, other, h, r)
                    pv = d if pv is None else pv + d
            if do_scores:
                smax_ref[other, h] = jnp.max(blk_max, axis=0, keepdims=True)
            if do_values:
                acc_ref[h] = alpha_ref[other, h] * acc_ref[h] + pv

    @pl.when(use_fixed_shift)
    def _():
        fixed_step(-1, 1, do_values=False)

        def steps(t, carry):
            for u in range(FIXED_UNROLL):
                fixed_step(FIXED_UNROLL * t + u, u % 2)
            return carry

        n_loop = (n_chunks - 2) // FIXED_UNROLL
        lax.fori_loop(0, n_loop, steps, 0)
        for j in range(n_loop * FIXED_UNROLL, n_chunks - 2):
            fixed_step(j, j % 2)
        fixed_step(n_chunks - 2, 0)
        fixed_step(n_chunks - 1, 1, do_scores=False)

    @pl.when(jnp.logical_not(use_fixed_shift))
    def _():
        m_ref[...] = jnp.full_like(m_ref, NEG_BIG)
        online_step(-1, 1, do_softmax=False, do_values=False)
        online_step(0, 0, do_values=False)

        def pair(t, carry):
            online_step(2 * t + 1, 1)
            online_step(2 * t + 2, 0)
            return carry

        lax.fori_loop(0, (n_chunks - 2) // 2, pair, 0)
        online_step(n_chunks - 1, 1, do_scores=False)
        online_step(n_chunks, 0, do_scores=False, do_softmax=False)

    for h in range(HEADS_PER_KV):
        a = acc_ref[h]
        o_ref[0, h * HEAD_DIM:(h + 1) * HEAD_DIM, :] = a[:HEAD_DIM] / a[HEAD_DIM:HEAD_DIM + 1]


def _global_attention(q_t, k, v_t, k_norm2):
    b, _, s = q_t.shape
    n_chunks = s // V_CHUNK
    assert n_chunks >= 2 and n_chunks % 2 == 0, "the kv pipeline runs chunks in pairs"
    return pl.pallas_call(
        functools.partial(_global_kernel, n_chunks=n_chunks),
        name="global_attn",
        grid=(b, KV_HEADS, s // Q_TILE),
        in_specs=[
            pl.BlockSpec((1, Q_PER_KV_ROWS, Q_TILE), lambda bi, g, i: (bi, g, i)),
            pl.BlockSpec((1, 1, s, HEAD_DIM), lambda bi, g, i: (bi, g, 0, 0)),
            pl.BlockSpec((1, n_chunks, HEAD_DIM, V_CHUNK), lambda bi, g, i: (bi, 0, g, 0)),
            pl.BlockSpec((1, 1, 1, s), lambda bi, g, i: (bi, g, 0, 0)),
        ],
        out_specs=pl.BlockSpec((1, Q_PER_KV_ROWS, Q_TILE), lambda bi, g, i: (bi, g, i)),
        out_shape=jax.ShapeDtypeStruct((b, GROUP_WIDTH, s), jnp.float32),
        scratch_shapes=[
            pltpu.VMEM((2, HEADS_PER_KV, V_CHUNK, Q_TILE), jnp.float32),
            pltpu.VMEM((2, HEADS_PER_KV, 1, Q_TILE), jnp.float32),
            pltpu.VMEM((2, HEADS_PER_KV, V_CHUNK, Q_TILE), jnp.bfloat16),
            pltpu.VMEM((2, HEADS_PER_KV, 1, Q_TILE), jnp.float32),
            pltpu.VMEM((HEADS_PER_KV, 1, Q_TILE), jnp.float32),
            pltpu.VMEM((HEADS_PER_KV, 1, Q_TILE), jnp.float32),
            pltpu.VMEM((HEADS_PER_KV, HEAD_DIM + ONES_ROWS, Q_TILE), jnp.float32),
        ],
        compiler_params=pltpu.CompilerParams(
            dimension_semantics=("arbitrary", "arbitrary", "arbitrary"),
            vmem_limit_bytes=VMEM_LIMIT),
    )(q_t, k, v_t, k_norm2)


def _window_kernel(q_ref, kp_ref, kc_ref, kn_ref, vp_ref, vc_ref, vn_ref, bias_ref, sink_ref,
                   o_ref):
    i = pl.program_id(2)
    has_prev = i > 0
    has_next = i < pl.num_programs(2) - 1
    kcat = jnp.concatenate([kp_ref[0, 0], kc_ref[0, 0], kn_ref[0, 0]], axis=0)
    vcat = jnp.concatenate([vp_ref[0, 0], vc_ref[0, 0], vn_ref[0, 0]], axis=1)
    bias = bias_ref[0]
    sink = sink_ref[0]
    for sb in range(N_SUB):
        lo = sb * WINDOW
        ks = kcat[lo:lo + BAND]
        vs = vcat[:, lo:lo + BAND]
        qs = jnp.concatenate(
            [q_ref[0, h * HEAD_DIM:(h + 1) * HEAD_DIM, lo:lo + WINDOW] for h in range(HEADS_PER_KV)],
            axis=1)
        st = jnp.dot(ks, qs, preferred_element_type=jnp.float32) + bias
        if sb == 0:
            row = lax.broadcasted_iota(jnp.int32, st.shape, 0)
            st = jnp.where(jnp.logical_or(has_prev, row >= WINDOW), st, NEG_BIG)
        if sb == N_SUB - 1:
            row = lax.broadcasted_iota(jnp.int32, st.shape, 0)
            st = jnp.where(jnp.logical_or(has_next, row < 2 * WINDOW), st, NEG_BIG)
        mx = jnp.maximum(jnp.max(st, axis=0, keepdims=True), sink)
        e = jnp.exp2(st - mx)
        denom = jnp.sum(e, axis=0, keepdims=True) + jnp.exp2(sink - mx)
        ot = jnp.dot(vs, e.astype(jnp.bfloat16), preferred_element_type=jnp.float32) / denom
        for h in range(HEADS_PER_KV):
            o_ref[0, h * HEAD_DIM:(h + 1) * HEAD_DIM, lo:lo + WINDOW] = (
                ot[:, h * WINDOW:(h + 1) * WINDOW])


def _window_attention(q_t, k, v_t, bias, sink):
    b, _, s = q_t.shape
    n_blk = s // WINDOW
    per = WQ_TILE // WINDOW
    vper = V_CHUNK // WINDOW

    def prev_blk(i):
        return jnp.maximum(i * per - 1, 0)

    def next_blk(i):
        return jnp.minimum(i * per + per, n_blk - 1)

    kh = lambda g: KV_HEADS + g
    return pl.pallas_call(
        _window_kernel,
        name="window_attn",
        grid=(b, KV_HEADS, s // WQ_TILE),
        in_specs=[
            pl.BlockSpec((1, Q_PER_KV_ROWS, WQ_TILE), lambda bi, g, i: (bi, g, i)),
            pl.BlockSpec((1, 1, WINDOW, HEAD_DIM), lambda bi, g, i: (bi, kh(g), prev_blk(i), 0)),
            pl.BlockSpec((1, 1, WQ_TILE, HEAD_DIM), lambda bi, g, i: (bi, kh(g), i, 0)),
            pl.BlockSpec((1, 1, WINDOW, HEAD_DIM), lambda bi, g, i: (bi, kh(g), next_blk(i), 0)),
            pl.BlockSpec((1, 1, HEAD_DIM, WINDOW),
                         lambda bi, g, i: (bi, prev_blk(i) // vper, kh(g), prev_blk(i) % vper)),
            pl.BlockSpec((1, 1, HEAD_DIM, WQ_TILE), lambda bi, g, i: (bi, i, kh(g), 0)),
            pl.BlockSpec((1, 1, HEAD_DIM, WINDOW),
                         lambda bi, g, i: (bi, next_blk(i) // vper, kh(g), next_blk(i) % vper)),
            pl.BlockSpec((1, BAND, HEADS_PER_KV * WINDOW), lambda bi, g, i: (g, 0, 0)),
            pl.BlockSpec((1, 1, HEADS_PER_KV * WINDOW), lambda bi, g, i: (g, 0, 0)),
        ],
        out_specs=pl.BlockSpec((1, Q_PER_KV_ROWS, WQ_TILE), lambda bi, g, i: (bi, g, i)),
        out_shape=jax.ShapeDtypeStruct((b, GROUP_WIDTH, s), jnp.float32),
        compiler_params=pltpu.CompilerParams(
            dimension_semantics=("arbitrary", "arbitrary", "arbitrary"),
            vmem_limit_bytes=VMEM_LIMIT),
    )(q_t, k, k, k, v_t, v_t, v_t, bias, sink)


def _mix_out_kernel(x_ref, oa_ref, ob_ref, ga_ref, gb_ref, w_ref, post_ref, o_ref):
    na = _rms_cols(oa_ref[0], ga_ref[...])
    nb = _rms_cols(ob_ref[0], gb_ref[...])
    mt = jnp.concatenate([na, nb], axis=0).astype(jnp.bfloat16)
    ht = jnp.dot(w_ref[...], mt, preferred_element_type=jnp.float32)
    ms = jnp.mean(ht * ht, axis=0, keepdims=True)
    yt = ht * lax.rsqrt(ms + NORM_EPS)
    o_ref[0] = x_ref[0] + yt.T * post_ref[...]


def _mix_out(x, oa_t, ob_t, ga, gb, w_out_t, post):
    b, s, _ = x.shape
    return pl.pallas_call(
        _mix_out_kernel,
        name="mix_out",
        grid=(b, s // TOKEN_TILE),
        in_specs=[
            pl.BlockSpec((1, TOKEN_TILE, D_MODEL), lambda bi, i: (bi, i, 0)),
            pl.BlockSpec((1, GROUP_WIDTH, TOKEN_TILE), lambda bi, i: (bi, 0, i)),
            pl.BlockSpec((1, GROUP_WIDTH, TOKEN_TILE), lambda bi, i: (bi, 0, i)),
            pl.BlockSpec((GROUP_WIDTH, 1), lambda bi, i: (0, 0)),
            pl.BlockSpec((GROUP_WIDTH, 1), lambda bi, i: (0, 0)),
            pl.BlockSpec((D_MODEL, D_MODEL), lambda bi, i: (0, 0)),
            pl.BlockSpec((1, D_MODEL), lambda bi, i: (0, 0)),
        ],
        out_specs=pl.BlockSpec((1, TOKEN_TILE, D_MODEL), lambda bi, i: (bi, i, 0)),
        out_shape=jax.ShapeDtypeStruct((b, s, D_MODEL), jnp.float32),
        compiler_params=pltpu.CompilerParams(
            dimension_semantics=("arbitrary", "arbitrary"), vmem_limit_bytes=VMEM_LIMIT),
    )(x, oa_t, ob_t, ga, gb, w_out_t, post)


_DEINTERLEAVE = np.concatenate([np.arange(0, HEAD_DIM, 2), np.arange(1, HEAD_DIM, 2)])


def _w_in_rows():
    n_rot = (Q_HEADS + KV_HEADS) * HEAD_DIM
    rot = (np.arange(Q_HEADS + KV_HEADS)[:, None] * HEAD_DIM + _DEINTERLEAVE[None, :]).reshape(-1)
    return np.concatenate([rot, np.arange(n_rot, IN_WIDTH)])


def _rope_tables_t(seq_len):
    n_rows = seq_len // GRID_W
    row = jnp.repeat(jnp.arange(n_rows, dtype=jnp.float32), GRID_W)
    col = jnp.tile(jnp.arange(GRID_W, dtype=jnp.float32), n_rows)
    n_freq = HEAD_DIM // 4
    inv_freq = ROPE_THETA ** (-jnp.arange(n_freq, dtype=jnp.float32) / n_freq)
    ang = jnp.concatenate([inv_freq[:, None] * row[None, :], inv_freq[:, None] * col[None, :]], axis=0)
    return jnp.cos(ang), jnp.sin(ang)


def _window_bias():
    slopes = 2.0 ** (-8.0 * np.arange(1, Q_HEADS + 1, dtype=np.float64) / Q_HEADS)
    kpos = np.arange(BAND)[:, None] - WINDOW
    qpos = np.arange(WINDOW)[None, :]
    dist = np.abs(qpos - kpos).astype(np.float64)
    per_head = np.where(dist[None] <= WINDOW, -slopes[:, None, None] * dist[None] * LOG2E, NEG_BIG)
    per_kv = per_head.reshape(KV_HEADS, HEADS_PER_KV, BAND, WINDOW).transpose(0, 2, 1, 3)
    return jnp.asarray(per_kv.reshape(KV_HEADS, BAND, HEADS_PER_KV * WINDOW), jnp.float32)


def _ffn_weights(w_gate, w_up, w_down):
    wg = w_gate.reshape(D_MODEL, N_FF_CHUNKS, FF_CHUNK)
    wu = w_up.reshape(D_MODEL, N_FF_CHUNKS, FF_CHUNK)
    wgu = jnp.concatenate([wg, wu], axis=-1).transpose(1, 0, 2).astype(jnp.bfloat16)
    wd = w_down.reshape(N_FF_CHUNKS, FF_CHUNK, D_MODEL).astype(jnp.bfloat16)
    return wgu, wd


def _trunk(x, layers, bias):
    b, s, _ = x.shape
    cos_t, sin_t = _rope_tables_t(s)
    for p in layers:
        x = _ffn(x.reshape(b * s, D_MODEL), p["ffn1_pre"], p["ffn1_post"], p["ffn1_wgu"],
                 p["ffn1_wd"]).reshape(b, s, D_MODEL)
        qa_t, qb_t, k, v_t, k_norm2 = _mix_in(x, p["mix_pre"], p["w_in_t"], p["gq"], p["gk"],
                                              cos_t, sin_t)
        oa_t = _global_attention(qa_t, k, v_t, k_norm2)
        ob_t = _window_attention(qb_t, k, v_t, bias, p["sink"])
        x = _mix_out(x, oa_t, ob_t, p["ga"], p["gb"], p["w_out_t"], p["mix_post"])
        x = _ffn(x.reshape(b * s, D_MODEL), p["ffn2_pre"], p["ffn2_post"], p["ffn2_wgu"],
                 p["ffn2_wd"]).reshape(b, s, D_MODEL)
    return x


def kernel(x_prompt, x_sample, ffn1_pre, ffn1_post, ffn1_w_gate, ffn1_w_up, ffn1_w_down, mix_pre, mix_post, w_in, a_q_norm, a_k_norm, b_sink, a_out_norm, b_out_norm, w_out, ffn2_pre, ffn2_post, ffn2_w_gate, ffn2_w_up, ffn2_w_down):
    depth = w_in.shape[0]
    rows = _w_in_rows()
    layers = []
    for l in range(depth):
        wgu1, wd1 = _ffn_weights(ffn1_w_gate[l], ffn1_w_up[l], ffn1_w_down[l])
        wgu2, wd2 = _ffn_weights(ffn2_w_gate[l], ffn2_w_up[l], ffn2_w_down[l])
        sink = jnp.repeat(b_sink[l].astype(jnp.float32) * LOG2E, WINDOW)
        layers.append(dict(
            ffn1_pre=ffn1_pre[l][None, :], ffn1_post=ffn1_post[l][None, :], ffn1_wgu=wgu1, ffn1_wd=wd1,
            ffn2_pre=ffn2_pre[l][None, :], ffn2_post=ffn2_post[l][None, :], ffn2_wgu=wgu2, ffn2_wd=wd2,
            mix_pre=mix_pre[l][None, :], mix_post=mix_post[l][None, :],
            w_in_t=w_in[l].T[rows].astype(jnp.bfloat16),
            gq=a_q_norm[l][_DEINTERLEAVE][:, None], gk=a_k_norm[l][_DEINTERLEAVE][:, None],
            sink=sink.reshape(KV_HEADS, 1, HEADS_PER_KV * WINDOW),
            ga=a_out_norm[l][:, None], gb=b_out_norm[l][:, None],
            w_out_t=w_out[l].T.astype(jnp.bfloat16),
        ))
    bias = _window_bias()
    return (_trunk(x_prompt, layers, bias), _trunk(x_sample, layers, bias))
```

```python
import functools
import math

import jax
import jax.numpy as jnp
import numpy as np
from jax import lax
from jax.experimental import pallas as pl
from jax.experimental.pallas import tpu as pltpu

D_MODEL = 1024
HEAD_DIM = 64
HALF_DIM = HEAD_DIM // 2
Q_HEADS = 8
KV_HEADS = 2
HEADS_PER_KV = Q_HEADS // KV_HEADS
GROUP_WIDTH = Q_HEADS * HEAD_DIM
KV_WIDTH = KV_HEADS * HEAD_DIM
Q_PER_KV_ROWS = HEADS_PER_KV * HEAD_DIM
IN_WIDTH = 2 * (GROUP_WIDTH + 2 * KV_WIDTH)
D_FF = 2816
GRID_W = 64
ROPE_THETA = 10000.0
WINDOW = 128
NORM_EPS = 1e-6
FFN_RESID = 0.5
LOG2E = math.log2(math.e)
Q_SCALE = HEAD_DIM ** -0.5 * LOG2E
NEG_BIG = -1e30
BOUND_SLACK = 1.01
MAX_FIXED_SHIFT = 48.0

FF_CHUNK = 256
N_FF_CHUNKS = D_FF // FF_CHUNK
TOKEN_TILE = 512
V_CHUNK = TOKEN_TILE
Q_TILE = 512
ONES_ROWS = 64
ROW_BLOCK = 128
FIXED_UNROLL = 4
WQ_TILE = 512
N_SUB = WQ_TILE // WINDOW
BAND = 3 * WINDOW

VMEM_LIMIT = 56 * 1024 * 1024


def _rms_rows(x, gain_row):
    ms = jnp.mean(x * x, axis=-1, keepdims=True)
    return x * lax.rsqrt(ms + NORM_EPS) * gain_row


def _rms_cols(xt, gain_col):
    ms = jnp.mean(xt * xt, axis=0, keepdims=True)
    return xt * lax.rsqrt(ms + NORM_EPS) * gain_col


def _ffn_kernel(x_ref, pre_ref, post_ref, wgu_ref, wd_ref, o_ref, xn_ref, acc_ref):
    x = x_ref[...]
    xn_ref[...] = _rms_rows(x, pre_ref[...]).astype(jnp.bfloat16)
    acc_ref[...] = jnp.zeros_like(acc_ref)

    for c in range(N_FF_CHUNKS):
        gu = jnp.dot(xn_ref[...], wgu_ref[c], preferred_element_type=jnp.float32)
        g = gu[:, :FF_CHUNK]
        u = gu[:, FF_CHUNK:]
        h = (g * jax.nn.sigmoid(g) * u).astype(jnp.bfloat16)
        acc_ref[...] += jnp.dot(h, wd_ref[c], preferred_element_type=jnp.float32)
    o_ref[...] = x + FFN_RESID * _rms_rows(acc_ref[...], post_ref[...])


def _ffn(x, pre, post, wgu, wd):
    t = x.shape[0]
    const3 = lambda i: (0, 0, 0)
    return pl.pallas_call(
        _ffn_kernel,
        name="ffn",
        grid=(t // TOKEN_TILE,),
        in_specs=[
            pl.BlockSpec((TOKEN_TILE, D_MODEL), lambda i: (i, 0)),
            pl.BlockSpec((1, D_MODEL), lambda i: (0, 0)),
            pl.BlockSpec((1, D_MODEL), lambda i: (0, 0)),
            pl.BlockSpec((N_FF_CHUNKS, D_MODEL, 2 * FF_CHUNK), const3),
            pl.BlockSpec((N_FF_CHUNKS, FF_CHUNK, D_MODEL), const3),
        ],
        out_specs=pl.BlockSpec((TOKEN_TILE, D_MODEL), lambda i: (i, 0)),
        out_shape=jax.ShapeDtypeStruct((t, D_MODEL), jnp.float32),
        scratch_shapes=[
            pltpu.VMEM((TOKEN_TILE, D_MODEL), jnp.bfloat16),
            pltpu.VMEM((TOKEN_TILE, D_MODEL), jnp.float32),
        ],
        compiler_params=pltpu.CompilerParams(
            dimension_semantics=("arbitrary",), vmem_limit_bytes=VMEM_LIMIT),
    )(x, pre, post, wgu, wd)


def _rope_norm_head(xt, gain_col, cos_t, sin_t):
    y = _rms_cols(xt, gain_col)
    x0 = y[:HALF_DIM]
    x1 = y[HALF_DIM:]
    return jnp.concatenate([x0 * cos_t - x1 * sin_t, x0 * sin_t + x1 * cos_t], axis=0)


def _mix_in_kernel(x_ref, pre_ref, w_ref, gq_ref, gk_ref, cos_ref, sin_ref,
                   qa_ref, qb_ref, k_ref, v_ref, kn_ref):
    xn = _rms_rows(x_ref[0], pre_ref[...]).astype(jnp.bfloat16)
    pt = lax.dot_general(w_ref[...], xn, (((1,), (1,)), ((), ())),
                         preferred_element_type=jnp.float32)
    cos_t = cos_ref[...]
    sin_t = sin_ref[...]
    gq = gq_ref[...]
    gk = gk_ref[...]
    off = 0
    for h in range(Q_HEADS):
        r = _rope_norm_head(pt[off:off + HEAD_DIM], gq, cos_t, sin_t)
        qa_ref[0, h * HEAD_DIM:(h + 1) * HEAD_DIM, :] = (r * Q_SCALE).astype(jnp.bfloat16)
        off += HEAD_DIM
    for h in range(KV_HEADS):
        r = _rope_norm_head(pt[off:off + HEAD_DIM], gk, cos_t, sin_t)
        k_ref[0, h] = r.T.astype(jnp.bfloat16)
        kn_ref[0, h] = jnp.sum(r * r, axis=0, keepdims=True)
        off += HEAD_DIM
    v_ref[0, 0, :KV_WIDTH, :] = pt[off:off + KV_WIDTH].astype(jnp.bfloat16)
    off += KV_WIDTH
    qb_ref[0] = (pt[off:off + GROUP_WIDTH] * Q_SCALE).astype(jnp.bfloat16)
    off += GROUP_WIDTH
    for h in range(KV_HEADS):
        k_ref[0, KV_HEADS + h] = pt[off:off + HEAD_DIM].T.astype(jnp.bfloat16)
        off += HEAD_DIM
    v_ref[0, 0, KV_WIDTH:, :] = pt[off:off + KV_WIDTH].astype(jnp.bfloat16)


def _mix_in(x, pre, w_in_t, gq, gk, cos_t, sin_t):
    b, s, _ = x.shape
    n_tiles = s // TOKEN_TILE
    return pl.pallas_call(
        _mix_in_kernel,
        name="mix_in",
        grid=(b, n_tiles),
        in_specs=[
            pl.BlockSpec((1, TOKEN_TILE, D_MODEL), lambda bi, i: (bi, i, 0)),
            pl.BlockSpec((1, D_MODEL), lambda bi, i: (0, 0)),
            pl.BlockSpec((IN_WIDTH, D_MODEL), lambda bi, i: (0, 0)),
            pl.BlockSpec((HEAD_DIM, 1), lambda bi, i: (0, 0)),
            pl.BlockSpec((HEAD_DIM, 1), lambda bi, i: (0, 0)),
            pl.BlockSpec((HALF_DIM, TOKEN_TILE), lambda bi, i: (0, i)),
            pl.BlockSpec((HALF_DIM, TOKEN_TILE), lambda bi, i: (0, i)),
        ],
        out_specs=[
            pl.BlockSpec((1, GROUP_WIDTH, TOKEN_TILE), lambda bi, i: (bi, 0, i)),
            pl.BlockSpec((1, GROUP_WIDTH, TOKEN_TILE), lambda bi, i: (bi, 0, i)),
            pl.BlockSpec((1, 2 * KV_HEADS, TOKEN_TILE, HEAD_DIM), lambda bi, i: (bi, 0, i, 0)),
            pl.BlockSpec((1, 1, 2 * KV_WIDTH, V_CHUNK), lambda bi, i: (bi, i, 0, 0)),
            pl.BlockSpec((1, KV_HEADS, 1, TOKEN_TILE), lambda bi, i: (bi, 0, 0, i)),
        ],
        out_shape=[
            jax.ShapeDtypeStruct((b, GROUP_WIDTH, s), jnp.bfloat16),
            jax.ShapeDtypeStruct((b, GROUP_WIDTH, s), jnp.bfloat16),
            jax.ShapeDtypeStruct((b, 2 * KV_HEADS, s, HEAD_DIM), jnp.bfloat16),
            jax.ShapeDtypeStruct((b, n_tiles, 2 * KV_WIDTH, V_CHUNK), jnp.bfloat16),
            jax.ShapeDtypeStruct((b, KV_HEADS, 1, s), jnp.float32),
        ],
        compiler_params=pltpu.CompilerParams(
            dimension_semantics=("arbitrary", "arbitrary"), vmem_limit_bytes=VMEM_LIMIT),
    )(x, pre, w_in_t, gq, gk, cos_t, sin_t)


def _global_kernel(q_ref, k_ref, v_ref, kn_ref, o_ref, s_ref, smax_ref, p_ref, alpha_ref, m_ref,
                   c_ref, acc_ref, *, n_chunks):
    ones = jnp.ones((ONES_ROWS, V_CHUNK), jnp.bfloat16)
    n_blocks = V_CHUNK // ROW_BLOCK
    acc_ref[...] = jnp.zeros_like(acc_ref)

    kmax2 = jnp.max(kn_ref[0, 0], axis=1, keepdims=True)
    for h in range(HEADS_PER_KV):
        qf = q_ref[0, h * HEAD_DIM:(h + 1) * HEAD_DIM, :].astype(jnp.float32)
        c_ref[h] = BOUND_SLACK * jnp.sqrt(jnp.sum(qf * qf, axis=0, keepdims=True) * kmax2)
    use_fixed_shift = jnp.max(c_ref[...]) <= MAX_FIXED_SHIFT

    def k_rows(chunk, r):
        start = chunk * V_CHUNK + r * ROW_BLOCK
        if not isinstance(start, int):
            start = pl.multiple_of(start, ROW_BLOCK)
        return k_ref[0, 0, pl.ds(start, ROW_BLOCK), :]

    def v_ext(chunk):
        return jnp.concatenate([v_ref[0, chunk], ones], axis=0)

    def values_block(vext, slot, h, r):
        cols = slice((r - 1) * ROW_BLOCK, (r + 1) * ROW_BLOCK)
        return jnp.dot(vext[:, cols], p_ref[slot, h, cols, :], preferred_element_type=jnp.float32)

    def fixed_step(j, parity, do_scores=True, do_values=True):
        cur, other = parity, 1 - parity
        if do_values:
            vext = v_ext(j)
        for h in range(HEADS_PER_KV):
            q_h = q_ref[0, h * HEAD_DIM:(h + 1) * HEAD_DIM, :]
            c_h = c_ref[h]
            pv = None
            for r in range(n_blocks):
                rows = slice(r * ROW_BLOCK, (r + 1) * ROW_BLOCK)
                if do_scores:
                    st = jnp.dot(k_rows(j + 1, r), q_h, preferred_element_type=jnp.float32)
                    p_ref[other, h, rows, :] = jnp.exp2(st - c_h).astype(jnp.bfloat16)
                if do_values and r % 2 == 1:
                    d = values_block(vext, cur, h, r)
                    pv = d if pv is None else pv + d
            if do_values:
                acc_ref[h] += pv

    def online_step(j, parity, do_scores=True, do_softmax=True, do_values=True):
        cur, other = parity, 1 - parity
        if do_values:
            vext = v_ext(j - 1)
        for h in range(HEADS_PER_KV):
            q_h = q_ref[0, h * HEAD_DIM:(h + 1) * HEAD_DIM, :]
            if do_softmax:
                m_old = m_ref[h]
                m_new = jnp.maximum(m_old, smax_ref[cur, h])
                alpha_ref[cur, h] = jnp.exp2(m_old - m_new)
                m_ref[h] = m_new
            blk_max = None
            pv = None
            for r in range(n_blocks):
                rows = slice(r * ROW_BLOCK, (r + 1) * ROW_BLOCK)
                if do_scores:
                    st = jnp.dot(k_rows(j + 1, r), q_h, preferred_element_type=jnp.float32)
                    s_ref[other, h, rows, :] = st
                    part = jnp.max(st.reshape(ROW_BLOCK // 8, 8, Q_TILE), axis=0)
                    blk_max = part if blk_max is None else jnp.maximum(blk_max, part)
                if do_softmax:
                    p_ref[cur, h, rows, :] = jnp.exp2(s_ref[cur, h, rows, :] - m_new).astype(
                        jnp.bfloat16)
                if do_values and r % 2 == 1:
                    d = values_block(vext, other, h, r)
                    pv = d if pv is None else pv + d
            if do_scores:
                smax_ref[other, h] = jnp.max(blk_max, axis=0, keepdims=True)
            if do_values:
                acc_ref[h] = alpha_ref[other, h] * acc_ref[h] + pv

    @pl.when(use_fixed_shift)
    def _():
        fixed_step(-1, 1, do_values=False)

        def steps(t, carry):
            for u in range(FIXED_UNROLL):
                fixed_step(FIXED_UNROLL * t + u, u % 2)
            return carry

        n_loop = (n_chunks - 2) // FIXED_UNROLL
        lax.fori_loop(0, n_loop, steps, 0)
        for j in range(n_loop * FIXED_UNROLL, n_chunks - 2):
            fixed_step(j, j % 2)
        fixed_step(n_chunks - 2, 0)
        fixed_step(n_chunks - 1, 1, do_scores=False)

    @pl.when(jnp.logical_not(use_fixed_shift))
    def _():
        m_ref[...] = jnp.full_like(m_ref, NEG_BIG)
        online_step(-1, 1, do_softmax=False, do_values=False)
        online_step(0, 0, do_values=False)

        def pair(t, carry):
            online_step(2 * t + 1, 1)
            online_step(2 * t + 2, 0)
            return carry

        lax.fori_loop(0, (n_chunks - 2) // 2, pair, 0)
        online_step(n_chunks - 1, 1, do_scores=False)
        online_step(n_chunks, 0, do_scores=False, do_softmax=False)

    for h in range(HEADS_PER_KV):
        a = acc_ref[h]
        o_ref[0, h * HEAD_DIM:(h + 1) * HEAD_DIM, :] = a[:HEAD_DIM] / a[HEAD_DIM:HEAD_DIM + 1]


def _global_attention(q_t, k, v_t, k_norm2):
    b, _, s = q_t.shape
    n_chunks = s // V_CHUNK
    assert n_chunks >= 2 and n_chunks % 2 == 0, "the kv pipeline runs chunks in pairs"
    return pl.pallas_call(
        functools.partial(_global_kernel, n_chunks=n_chunks),
        name="global_attn",
        grid=(b, KV_HEADS, s // Q_TILE),
        in_specs=[
            pl.BlockSpec((1, Q_PER_KV_ROWS, Q_TILE), lambda bi, g, i: (bi, g, i)),
            pl.BlockSpec((1, 1, s, HEAD_DIM), lambda bi, g, i: (bi, g, 0, 0)),
            pl.BlockSpec((1, n_chunks, HEAD_DIM, V_CHUNK), lambda bi, g, i: (bi, 0, g, 0)),
            pl.BlockSpec((1, 1, 1, s), lambda bi, g, i: (bi, g, 0, 0)),
        ],
        out_specs=pl.BlockSpec((1, Q_PER_KV_ROWS, Q_TILE), lambda bi, g, i: (bi, g, i)),
        out_shape=jax.ShapeDtypeStruct((b, GROUP_WIDTH, s), jnp.float32),
        scratch_shapes=[
            pltpu.VMEM((2, HEADS_PER_KV, V_CHUNK, Q_TILE), jnp.float32),
            pltpu.VMEM((2, HEADS_PER_KV, 1, Q_TILE), jnp.float32),
            pltpu.VMEM((2, HEADS_PER_KV, V_CHUNK, Q_TILE), jnp.bfloat16),
            pltpu.VMEM((2, HEADS_PER_KV, 1, Q_TILE), jnp.float32),
            pltpu.VMEM((HEADS_PER_KV, 1, Q_TILE), jnp.float32),
            pltpu.VMEM((HEADS_PER_KV, 1, Q_TILE), jnp.float32),
            pltpu.VMEM((HEADS_PER_KV, HEAD_DIM + ONES_ROWS, Q_TILE), jnp.float32),
        ],
        compiler_params=pltpu.CompilerParams(
            dimension_semantics=("arbitrary", "arbitrary", "arbitrary"),
            vmem_limit_bytes=VMEM_LIMIT),
    )(q_t, k, v_t, k_norm2)


def _window_kernel(q_ref, kp_ref, kc_ref, kn_ref, vp_ref, vc_ref, vn_ref, bias_ref, sink_ref,
                   o_ref):
    i = pl.program_id(2)
    has_prev = i > 0
    has_next = i < pl.num_programs(2) - 1
    kcat = jnp.concatenate([kp_ref[0, 0], kc_ref[0, 0], kn_ref[0, 0]], axis=0)
    vcat = jnp.concatenate([vp_ref[0, 0], vc_ref[0, 0], vn_ref[0, 0]], axis=1)
    bias = bias_ref[0]
    sink = sink_ref[0]
    for sb in range(N_SUB):
        lo = sb * WINDOW
        ks = kcat[lo:lo + BAND]
        vs = vcat[:, lo:lo + BAND]
        qs = jnp.concatenate(
            [q_ref[0, h * HEAD_DIM:(h + 1) * HEAD_DIM, lo:lo + WINDOW] for h in range(HEADS_PER_KV)],
            axis=1)
        st = jnp.dot(ks, qs, preferred_element_type=jnp.float32) + bias
        if sb == 0:
            row = lax.broadcasted_iota(jnp.int32, st.shape, 0)
            st = jnp.where(jnp.logical_or(has_prev, row >= WINDOW), st, NEG_BIG)
        if sb == N_SUB - 1:
            row = lax.broadcasted_iota(jnp.int32, st.shape, 0)
            st = jnp.where(jnp.logical_or(has_next, row < 2 * WINDOW), st, NEG_BIG)
        mx = jnp.maximum(jnp.max(st, axis=0, keepdims=True), sink)
        e = jnp.exp2(st - mx)
        denom = jnp.sum(e, axis=0, keepdims=True) + jnp.exp2(sink - mx)
        ot = jnp.dot(vs, e.astype(jnp.bfloat16), preferred_element_type=jnp.float32) / denom
        for h in range(HEADS_PER_KV):
            o_ref[0, h * HEAD_DIM:(h + 1) * HEAD_DIM, lo:lo + WINDOW] = (
                ot[:, h * WINDOW:(h + 1) * WINDOW])


def _window_attention(q_t, k, v_t, bias, sink):
    b, _, s = q_t.shape
    n_blk = s // WINDOW
    per = WQ_TILE // WINDOW
    vper = V_CHUNK // WINDOW

    def prev_blk(i):
        return jnp.maximum(i * per - 1, 0)

    def next_blk(i):
        return jnp.minimum(i * per + per, n_blk - 1)

    kh = lambda g: KV_HEADS + g
    return pl.pallas_call(
        _window_kernel,
        name="window_attn",
        grid=(b, KV_HEADS, s // WQ_TILE),
        in_specs=[
            pl.BlockSpec((1, Q_PER_KV_ROWS, WQ_TILE), lambda bi, g, i: (bi, g, i)),
            pl.BlockSpec((1, 1, WINDOW, HEAD_DIM), lambda bi, g, i: (bi, kh(g), prev_blk(i), 0)),
            pl.BlockSpec((1, 1, WQ_TILE, HEAD_DIM), lambda bi, g, i: (bi, kh(g), i, 0)),
            pl.BlockSpec((1, 1, WINDOW, HEAD_DIM), lambda bi, g, i: (bi, kh(g), next_blk(i), 0)),
            pl.BlockSpec((1, 1, HEAD_DIM, WINDOW),
                         lambda bi, g, i: (bi, prev_blk(i) // vper, kh(g), prev_blk(i) % vper)),
            pl.BlockSpec((1, 1, HEAD_DIM, WQ_TILE), lambda bi, g, i: (bi, i, kh(g), 0)),
            pl.BlockSpec((1, 1, HEAD_DIM, WINDOW),
                         lambda bi, g, i: (bi, next_blk(i) // vper, kh(g), next_blk(i) % vper)),
            pl.BlockSpec((1, BAND, HEADS_PER_KV * WINDOW), lambda bi, g, i: (g, 0, 0)),
            pl.BlockSpec((1, 1, HEADS_PER_KV * WINDOW), lambda bi, g, i: (g, 0, 0)),
        ],
        out_specs=pl.BlockSpec((1, Q_PER_KV_ROWS, WQ_TILE), lambda bi, g, i: (bi, g, i)),
        out_shape=jax.ShapeDtypeStruct((b, GROUP_WIDTH, s), jnp.float32),
        compiler_params=pltpu.CompilerParams(
            dimension_semantics=("arbitrary", "arbitrary", "arbitrary"),
            vmem_limit_bytes=VMEM_LIMIT),
    )(q_t, k, k, k, v_t, v_t, v_t, bias, sink)


def _mix_out_kernel(x_ref, oa_ref, ob_ref, ga_ref, gb_ref, w_ref, post_ref, o_ref):
    na = _rms_cols(oa_ref[0], ga_ref[...])
    nb = _rms_cols(ob_ref[0], gb_ref[...])
    mt = jnp.concatenate([na, nb], axis=0).astype(jnp.bfloat16)
    ht = jnp.dot(w_ref[...], mt, preferred_element_type=jnp.float32)
    ms = jnp.mean(ht * ht, axis=0, keepdims=True)
    yt = ht * lax.rsqrt(ms + NORM_EPS)
    o_ref[0] = x_ref[0] + yt.T * post_ref[...]


def _mix_out(x, oa_t, ob_t, ga, gb, w_out_t, post):
    b, s, _ = x.shape
    return pl.pallas_call(
        _mix_out_kernel,
        name="mix_out",
        grid=(b, s // TOKEN_TILE),
        in_specs=[
            pl.BlockSpec((1, TOKEN_TILE, D_MODEL), lambda bi, i: (bi, i, 0)),
            pl.BlockSpec((1, GROUP_WIDTH, TOKEN_TILE), lambda bi, i: (bi, 0, i)),
            pl.BlockSpec((1, GROUP_WIDTH, TOKEN_TILE), lambda bi, i: (bi, 0, i)),
            pl.BlockSpec((GROUP_WIDTH, 1), lambda bi, i: (0, 0)),
            pl.BlockSpec((GROUP_WIDTH, 1), lambda bi, i: (0, 0)),
            pl.BlockSpec((D_MODEL, D_MODEL), lambda bi, i: (0, 0)),
            pl.BlockSpec((1, D_MODEL), lambda bi, i: (0, 0)),
        ],
        out_specs=pl.BlockSpec((1, TOKEN_TILE, D_MODEL), lambda bi, i: (bi, i, 0)),
        out_shape=jax.ShapeDtypeStruct((b, s, D_MODEL), jnp.float32),
        compiler_params=pltpu.CompilerParams(
            dimension_semantics=("arbitrary", "arbitrary"), vmem_limit_bytes=VMEM_LIMIT),
    )(x, oa_t, ob_t, ga, gb, w_out_t, post)


_DEINTERLEAVE = np.concatenate([np.arange(0, HEAD_DIM, 2), np.arange(1, HEAD_DIM, 2)])


def _w_in_rows():
    n_rot = (Q_HEADS + KV_HEADS) * HEAD_DIM
    rot = (np.arange(Q_HEADS + KV_HEADS)[:, None] * HEAD_DIM + _DEINTERLEAVE[None, :]).reshape(-1)
    return np.concatenate([rot, np.arange(n_rot, IN_WIDTH)])


def _rope_tables_t(seq_len):
    n_rows = seq_len // GRID_W
    row = jnp.repeat(jnp.arange(n_rows, dtype=jnp.float32), GRID_W)
    col = jnp.tile(jnp.arange(GRID_W, dtype=jnp.float32), n_rows)
    n_freq = HEAD_DIM // 4
    inv_freq = ROPE_THETA ** (-jnp.arange(n_freq, dtype=jnp.float32) / n_freq)
    ang = jnp.concatenate([inv_freq[:, None] * row[None, :], inv_freq[:, None] * col[None, :]], axis=0)
    return jnp.cos(ang), jnp.sin(ang)


def _window_bias():
    slopes = 2.0 ** (-8.0 * np.arange(1, Q_HEADS + 1, dtype=np.float64) / Q_HEADS)
    kpos = np.arange(BAND)[:, None] - WINDOW
    qpos = np.arange(WINDOW)[None, :]
    dist = np.abs(qpos - kpos).astype(np.float64)
    per_head = np.where(dist[None] <= WINDOW, -slopes[:, None, None] * dist[None] * LOG2E, NEG_BIG)
    per_kv = per_head.reshape(KV_HEADS, HEADS_PER_KV, BAND, WINDOW).transpose(0, 2, 1, 3)
    return jnp.asarray(per_kv.reshape(KV_HEADS, BAND, HEADS_PER_KV * WINDOW), jnp.float32)


def _ffn_weights(w_gate, w_up, w_down):
    wg = w_gate.reshape(D_MODEL, N_FF_CHUNKS, FF_CHUNK)
    wu = w_up.reshape(D_MODEL, N_FF_CHUNKS, FF_CHUNK)
    wgu = jnp.concatenate([wg, wu], axis=-1).transpose(1, 0, 2).astype(jnp.bfloat16)
    wd = w_down.reshape(N_FF_CHUNKS, FF_CHUNK, D_MODEL).astype(jnp.bfloat16)
    return wgu, wd


def _trunk(x, layers, bias):
    b, s, _ = x.shape
    cos_t, sin_t = _rope_tables_t(s)
    for p in layers:
        x = _ffn(x.reshape(b * s, D_MODEL), p["ffn1_pre"], p["ffn1_post"], p["ffn1_wgu"],
                 p["ffn1_wd"]).reshape(b, s, D_MODEL)
        qa_t, qb_t, k, v_t, k_norm2 = _mix_in(x, p["mix_pre"], p["w_in_t"], p["gq"], p["gk"],
                                              cos_t, sin_t)
        oa_t = _global_attention(qa_t, k, v_t, k_norm2)
        ob_t = _window_attention(qb_t, k, v_t, bias, p["sink"])
        x = _mix_out(x, oa_t, ob_t, p["ga"], p["gb"], p["w_out_t"], p["mix_post"])
        x = _ffn(x.reshape(b * s, D_MODEL), p["ffn2_pre"], p["ffn2_post"], p["ffn2_wgu"],
                 p["ffn2_wd"]).reshape(b, s, D_MODEL)
    return x


def kernel(x_prompt, x_sample, ffn1_pre, ffn1_post, ffn1_w_gate, ffn1_w_up, ffn1_w_down, mix_pre, mix_post, w_in, a_q_norm, a_k_norm, b_sink, a_out_norm, b_out_norm, w_out, ffn2_pre, ffn2_post, ffn2_w_gate, ffn2_w_up, ffn2_w_down):
    depth = w_in.shape[0]
    rows = _w_in_rows()
    layers = []
    for l in range(depth):
        wgu1, wd1 = _ffn_weights(ffn1_w_gate[l], ffn1_w_up[l], ffn1_w_down[l])
        wgu2, wd2 = _ffn_weights(ffn2_w_gate[l], ffn2_w_up[l], ffn2_w_down[l])
        sink = jnp.repeat(b_sink[l].astype(jnp.float32) * LOG2E, WINDOW)
        layers.append(dict(
            ffn1_pre=ffn1_pre[l][None, :], ffn1_post=ffn1_post[l][None, :], ffn1_wgu=wgu1, ffn1_wd=wd1,
            ffn2_pre=ffn2_pre[l][None, :], ffn2_post=ffn2_post[l][None, :], ffn2_wgu=wgu2, ffn2_wd=wd2,
            mix_pre=mix_pre[l][None, :], mix_post=mix_post[l][None, :],
            w_in_t=w_in[l].T[rows].astype(jnp.bfloat16),
            gq=a_q_norm[l][_DEINTERLEAVE][:, None], gk=a_k_norm[l][_DEINTERLEAVE][:, None],
            sink=sink.reshape(KV_HEADS, 1, HEADS_PER_KV * WINDOW),
            ga=a_out_norm[l][:, None], gb=b_out_norm[l][:, None],
            w_out_t=w_out[l].T.astype(jnp.bfloat16),
        ))
    bias = _window_bias()
    return (_trunk(x_prompt, layers, bias), _trunk(x_sample, layers, bias))
```

```python
import functools
import math

import jax
import jax.numpy as jnp
import numpy as np
from jax import lax
from jax.experimental import pallas as pl
from jax.experimental.pallas import tpu as pltpu

D_MODEL = 1024
HEAD_DIM = 64
HALF_DIM = HEAD_DIM // 2
Q_HEADS = 8
KV_HEADS = 2
HEADS_PER_KV = Q_HEADS // KV_HEADS
GROUP_WIDTH = Q_HEADS * HEAD_DIM
KV_WIDTH = KV_HEADS * HEAD_DIM
Q_PER_KV_ROWS = HEADS_PER_KV * HEAD_DIM
IN_WIDTH = 2 * (GROUP_WIDTH + 2 * KV_WIDTH)
D_FF = 2816
GRID_W = 64
ROPE_THETA = 10000.0
WINDOW = 128
NORM_EPS = 1e-6
FFN_RESID = 0.5
LOG2E = math.log2(math.e)
Q_SCALE = HEAD_DIM ** -0.5 * LOG2E
NEG_BIG = -1e30
BOUND_SLACK = 1.01
MAX_FIXED_SHIFT = 48.0

FF_CHUNK = 256
N_FF_CHUNKS = D_FF // FF_CHUNK
TOKEN_TILE = 512
V_CHUNK = TOKEN_TILE
Q_TILE = 512
ONES_ROWS = 16
ROW_BLOCK = 128
FIXED_UNROLL = 4
WQ_TILE = 512
N_SUB = WQ_TILE // WINDOW
BAND = 3 * WINDOW

VMEM_LIMIT = 56 * 1024 * 1024


def _rms_rows(x, gain_row):
    ms = jnp.mean(x * x, axis=-1, keepdims=True)
    return x * lax.rsqrt(ms + NORM_EPS) * gain_row


def _rms_cols(xt, gain_col):
    ms = jnp.mean(xt * xt, axis=0, keepdims=True)
    return xt * lax.rsqrt(ms + NORM_EPS) * gain_col


def _ffn_kernel(x_ref, pre_ref, post_ref, wgu_ref, wd_ref, o_ref, xn_ref, acc_ref):
    x = x_ref[...]
    xn_ref[...] = _rms_rows(x, pre_ref[...]).astype(jnp.bfloat16)
    acc_ref[...] = jnp.zeros_like(acc_ref)

    for c in range(N_FF_CHUNKS):
        gu = jnp.dot(xn_ref[...], wgu_ref[c], preferred_element_type=jnp.float32)
        g = gu[:, :FF_CHUNK]
        u = gu[:, FF_CHUNK:]
        h = (g * jax.nn.sigmoid(g) * u).astype(jnp.bfloat16)
        acc_ref[...] += jnp.dot(h, wd_ref[c], preferred_element_type=jnp.float32)
    o_ref[...] = x + FFN_RESID * _rms_rows(acc_ref[...], post_ref[...])


def _ffn(x, pre, post, wgu, wd):
    t = x.shape[0]
    const3 = lambda i: (0, 0, 0)
    return pl.pallas_call(
        _ffn_kernel,
        name="ffn",
        grid=(t // TOKEN_TILE,),
        in_specs=[
            pl.BlockSpec((TOKEN_TILE, D_MODEL), lambda i: (i, 0)),
            pl.BlockSpec((1, D_MODEL), lambda i: (0, 0)),
            pl.BlockSpec((1, D_MODEL), lambda i: (0, 0)),
            pl.BlockSpec((N_FF_CHUNKS, D_MODEL, 2 * FF_CHUNK), const3),
            pl.BlockSpec((N_FF_CHUNKS, FF_CHUNK, D_MODEL), const3),
        ],
        out_specs=pl.BlockSpec((TOKEN_TILE, D_MODEL), lambda i: (i, 0)),
        out_shape=jax.ShapeDtypeStruct((t, D_MODEL), jnp.float32),
        scratch_shapes=[
            pltpu.VMEM((TOKEN_TILE, D_MODEL), jnp.bfloat16),
            pltpu.VMEM((TOKEN_TILE, D_MODEL), jnp.float32),
        ],
        compiler_params=pltpu.CompilerParams(
            dimension_semantics=("arbitrary",), vmem_limit_bytes=VMEM_LIMIT),
    )(x, pre, post, wgu, wd)


def _rope_norm_head(xt, gain_col, cos_t, sin_t):
    y = _rms_cols(xt, gain_col)
    x0 = y[:HALF_DIM]
    x1 = y[HALF_DIM:]
    return jnp.concatenate([x0 * cos_t - x1 * sin_t, x0 * sin_t + x1 * cos_t], axis=0)


def _mix_in_kernel(x_ref, pre_ref, w_ref, gq_ref, gk_ref, cos_ref, sin_ref,
                   qa_ref, qb_ref, k_ref, v_ref, kn_ref):
    xn = _rms_rows(x_ref[0], pre_ref[...]).astype(jnp.bfloat16)
    pt = lax.dot_general(w_ref[...], xn, (((1,), (1,)), ((), ())),
                         preferred_element_type=jnp.float32)
    cos_t = cos_ref[...]
    sin_t = sin_ref[...]
    gq = gq_ref[...]
    gk = gk_ref[...]
    off = 0
    for h in range(Q_HEADS):
        r = _rope_norm_head(pt[off:off + HEAD_DIM], gq, cos_t, sin_t)
        qa_ref[0, h * HEAD_DIM:(h + 1) * HEAD_DIM, :] = (r * Q_SCALE).astype(jnp.bfloat16)
        off += HEAD_DIM
    for h in range(KV_HEADS):
        r = _rope_norm_head(pt[off:off + HEAD_DIM], gk, cos_t, sin_t)
        k_ref[0, h] = r.T.astype(jnp.bfloat16)
        kn_ref[0, h] = jnp.sum(r * r, axis=0, keepdims=True)
        off += HEAD_DIM
    v_ref[0, 0, :KV_WIDTH, :] = pt[off:off + KV_WIDTH].astype(jnp.bfloat16)
    off += KV_WIDTH
    qb_ref[0] = (pt[off:off + GROUP_WIDTH] * Q_SCALE).astype(jnp.bfloat16)
    off += GROUP_WIDTH
    for h in range(KV_HEADS):
        k_ref[0, KV_HEADS + h] = pt[off:off + HEAD_DIM].T.astype(jnp.bfloat16)
        off += HEAD_DIM
    v_ref[0, 0, KV_WIDTH:, :] = pt[off:off + KV_WIDTH].astype(jnp.bfloat16)


def _mix_in(x, pre, w_in_t, gq, gk, cos_t, sin_t):
    b, s, _ = x.shape
    n_tiles = s // TOKEN_TILE
    return pl.pallas_call(
        _mix_in_kernel,
        name="mix_in",
        grid=(b, n_tiles),
        in_specs=[
            pl.BlockSpec((1, TOKEN_TILE, D_MODEL), lambda bi, i: (bi, i, 0)),
            pl.BlockSpec((1, D_MODEL), lambda bi, i: (0, 0)),
            pl.BlockSpec((IN_WIDTH, D_MODEL), lambda bi, i: (0, 0)),
            pl.BlockSpec((HEAD_DIM, 1), lambda bi, i: (0, 0)),
            pl.BlockSpec((HEAD_DIM, 1), lambda bi, i: (0, 0)),
            pl.BlockSpec((HALF_DIM, TOKEN_TILE), lambda bi, i: (0, i)),
            pl.BlockSpec((HALF_DIM, TOKEN_TILE), lambda bi, i: (0, i)),
        ],
        out_specs=[
            pl.BlockSpec((1, GROUP_WIDTH, TOKEN_TILE), lambda bi, i: (bi, 0, i)),
            pl.BlockSpec((1, GROUP_WIDTH, TOKEN_TILE), lambda bi, i: (bi, 0, i)),
            pl.BlockSpec((1, 2 * KV_HEADS, TOKEN_TILE, HEAD_DIM), lambda bi, i: (bi, 0, i, 0)),
            pl.BlockSpec((1, 1, 2 * KV_WIDTH, V_CHUNK), lambda bi, i: (bi, i, 0, 0)),
            pl.BlockSpec((1, KV_HEADS, 1, TOKEN_TILE), lambda bi, i: (bi, 0, 0, i)),
        ],
        out_shape=[
            jax.ShapeDtypeStruct((b, GROUP_WIDTH, s), jnp.bfloat16),
            jax.ShapeDtypeStruct((b, GROUP_WIDTH, s), jnp.bfloat16),
            jax.ShapeDtypeStruct((b, 2 * KV_HEADS, s, HEAD_DIM), jnp.bfloat16),
            jax.ShapeDtypeStruct((b, n_tiles, 2 * KV_WIDTH, V_CHUNK), jnp.bfloat16),
            jax.ShapeDtypeStruct((b, KV_HEADS, 1, s), jnp.float32),
        ],
        compiler_params=pltpu.CompilerParams(
            dimension_semantics=("arbitrary", "arbitrary"), vmem_limit_bytes=VMEM_LIMIT),
    )(x, pre, w_in_t, gq, gk, cos_t, sin_t)


def _zero_after(x):
    bits = lax.bitcast_convert_type(x, jnp.uint32)
    bits = lax.shift_right_logical(lax.shift_right_logical(bits, jnp.uint32(16)), jnp.uint32(16))
    return bits.astype(jnp.float32)


def _global_kernel(q_ref, k_ref, v_ref, kn_ref, o_ref, s_ref, smax_ref, p_ref, alpha_ref, m_ref,
                   c_ref, acc_ref, *, n_chunks):
    ones = jnp.ones((ONES_ROWS, V_CHUNK), jnp.bfloat16)
    n_blocks = V_CHUNK // ROW_BLOCK
    acc_ref[...] = jnp.zeros_like(acc_ref)

    kmax2 = jnp.max(kn_ref[0, 0], axis=1, keepdims=True)
    for h in range(HEADS_PER_KV):
        qf = q_ref[0, h * HEAD_DIM:(h + 1) * HEAD_DIM, :].astype(jnp.float32)
        c_ref[h] = BOUND_SLACK * jnp.sqrt(jnp.sum(qf * qf, axis=0, keepdims=True) * kmax2)
    use_fixed_shift = jnp.max(c_ref[...]) <= MAX_FIXED_SHIFT

    def k_rows(chunk, r):
        start = chunk * V_CHUNK + r * ROW_BLOCK
        if not isinstance(start, int):
            start = pl.multiple_of(start, ROW_BLOCK)
        return k_ref[0, 0, pl.ds(start, ROW_BLOCK), :]

    def v_ext(chunk):
        return jnp.concatenate([v_ref[0, chunk], ones], axis=0)

    def values_block(vext, slot, h, r):
        cols = slice((r - 1) * ROW_BLOCK, (r + 1) * ROW_BLOCK)
        return jnp.dot(vext[:, cols], p_ref[slot, h, cols, :], preferred_element_type=jnp.float32)

    def fixed_step(j, parity, do_scores=True, do_values=True):
        cur, other = parity, 1 - parity
        if do_values:
            vext = v_ext(j)
        after_values = None
        for h in range(HEADS_PER_KV):
            q_h = q_ref[0, h * HEAD_DIM:(h + 1) * HEAD_DIM, :]
            c_h = c_ref[h]
            pv = None
            for r in range(n_blocks):
                rows = slice(r * ROW_BLOCK, (r + 1) * ROW_BLOCK)
                if do_scores:
                    st = jnp.dot(k_rows(j + 1, r), q_h, preferred_element_type=jnp.float32)
                    shift = c_h if after_values is None else c_h + after_values
                    p_ref[other, h, rows, :] = jnp.exp2(st - shift).astype(jnp.bfloat16)
                if do_values and r % 2 == 1:
                    d = values_block(vext, cur, h, r)
                    pv = d if pv is None else pv + d
                    after_values = _zero_after(d[HEAD_DIM:HEAD_DIM + 1])
            if do_values:
                acc_ref[h] += pv

    def online_step(j, parity, do_scores=True, do_softmax=True, do_values=True):
        cur, other = parity, 1 - parity
        if do_values:
            vext = v_ext(j - 1)
        for h in range(HEADS_PER_KV):
            q_h = q_ref[0, h * HEAD_DIM:(h + 1) * HEAD_DIM, :]
            if do_softmax:
                m_old = m_ref[h]
                m_new = jnp.maximum(m_old, smax_ref[cur, h])
                alpha_ref[cur, h] = jnp.exp2(m_old - m_new)
                m_ref[h] = m_new
            blk_max = None
            pv = None
            for r in range(n_blocks):
                rows = slice(r * ROW_BLOCK, (r + 1) * ROW_BLOCK)
                if do_scores:
                    st = jnp.dot(k_rows(j + 1, r), q_h, preferred_element_type=jnp.float32)
                    s_ref[other, h, rows, :] = st
                    part = jnp.max(st.reshape(ROW_BLOCK // 8, 8, Q_TILE), axis=0)
                    blk_max = part if blk_max is None else jnp.maximum(blk_max, part)
                if do_softmax:
                    p_ref[cur, h, rows, :] = jnp.exp2(s_ref[cur, h, rows, :] - m_new).astype(
                        jnp.bfloat16)
                if do_values and r % 2 == 1:
                    d = values_block(vext, other, h, r)
                    pv = d if pv is None else pv + d
            if do_scores:
                smax_ref[other, h] = jnp.max(blk_max, axis=0, keepdims=True)
            if do_values:
                acc_ref[h] = alpha_ref[other, h] * acc_ref[h] + pv

    @pl.when(use_fixed_shift)
    def _():
        fixed_step(-1, 1, do_values=False)

        def steps(t, carry):
            for u in range(FIXED_UNROLL):
                fixed_step(FIXED_UNROLL * t + u, u % 2)
            return carry

        n_loop = (n_chunks - 2) // FIXED_UNROLL
        lax.fori_loop(0, n_loop, steps, 0)
        for j in range(n_loop * FIXED_UNROLL, n_chunks - 2):
            fixed_step(j, j % 2)
        fixed_step(n_chunks - 2, 0)
        fixed_step(n_chunks - 1, 1, do_scores=False)

    @pl.when(jnp.logical_not(use_fixed_shift))
    def _():
        m_ref[...] = jnp.full_like(m_ref, NEG_BIG)
        online_step(-1, 1, do_softmax=False, do_values=False)
        online_step(0, 0, do_values=False)

        def pair(t, carry):
            online_step(2 * t + 1, 1)
            online_step(2 * t + 2, 0)
            return carry

        lax.fori_loop(0, (n_chunks - 2) // 2, pair, 0)
        online_step(n_chunks - 1, 1, do_scores=False)
        online_step(n_chunks, 0, do_scores=False, do_softmax=False)

    for h in range(HEADS_PER_KV):
        a = acc_ref[h]
        o_ref[0, h * HEAD_DIM:(h + 1) * HEAD_DIM, :] = a[:HEAD_DIM] / a[HEAD_DIM:HEAD_DIM + 1]


def _global_attention(q_t, k, v_t, k_norm2):
    b, _, s = q_t.shape
    n_chunks = s // V_CHUNK
    assert n_chunks >= 2 and n_chunks % 2 == 0, "the kv pipeline runs chunks in pairs"
    return pl.pallas_call(
        functools.partial(_global_kernel, n_chunks=n_chunks),
        name="global_attn",
        grid=(b, KV_HEADS, s // Q_TILE),
        in_specs=[
            pl.BlockSpec((1, Q_PER_KV_ROWS, Q_TILE), lambda bi, g, i: (bi, g, i)),
            pl.BlockSpec((1, 1, s, HEAD_DIM), lambda bi, g, i: (bi, g, 0, 0)),
            pl.BlockSpec((1, n_chunks, HEAD_DIM, V_CHUNK), lambda bi, g, i: (bi, 0, g, 0)),
            pl.BlockSpec((1, 1, 1, s), lambda bi, g, i: (bi, g, 0, 0)),
        ],
        out_specs=pl.BlockSpec((1, Q_PER_KV_ROWS, Q_TILE), lambda bi, g, i: (bi, g, i)),
        out_shape=jax.ShapeDtypeStruct((b, GROUP_WIDTH, s), jnp.float32),
        scratch_shapes=[
            pltpu.VMEM((2, HEADS_PER_KV, V_CHUNK, Q_TILE), jnp.float32),
            pltpu.VMEM((2, HEADS_PER_KV, 1, Q_TILE), jnp.float32),
            pltpu.VMEM((2, HEADS_PER_KV, V_CHUNK, Q_TILE), jnp.bfloat16),
            pltpu.VMEM((2, HEADS_PER_KV, 1, Q_TILE), jnp.float32),
            pltpu.VMEM((HEADS_PER_KV, 1, Q_TILE), jnp.float32),
            pltpu.VMEM((HEADS_PER_KV, 1, Q_TILE), jnp.float32),
            pltpu.VMEM((HEADS_PER_KV, HEAD_DIM + ONES_ROWS, Q_TILE), jnp.float32),
        ],
        compiler_params=pltpu.CompilerParams(
            dimension_semantics=("arbitrary", "arbitrary", "arbitrary"),
            vmem_limit_bytes=VMEM_LIMIT),
    )(q_t, k, v_t, k_norm2)


def _window_kernel(q_ref, kp_ref, kc_ref, kn_ref, vp_ref, vc_ref, vn_ref, bias_ref, sink_ref,
                   o_ref):
    i = pl.program_id(2)
    has_prev = i > 0
    has_next = i < pl.num_programs(2) - 1
    kcat = jnp.concatenate([kp_ref[0, 0], kc_ref[0, 0], kn_ref[0, 0]], axis=0)
    vcat = jnp.concatenate([vp_ref[0, 0], vc_ref[0, 0], vn_ref[0, 0]], axis=1)
    bias = bias_ref[0]
    sink = sink_ref[0]
    for sb in range(N_SUB):
        lo = sb * WINDOW
        ks = kcat[lo:lo + BAND]
        vs = vcat[:, lo:lo + BAND]
        qs = jnp.concatenate(
            [q_ref[0, h * HEAD_DIM:(h + 1) * HEAD_DIM, lo:lo + WINDOW] for h in range(HEADS_PER_KV)],
            axis=1)
        st = jnp.dot(ks, qs, preferred_element_type=jnp.float32) + bias
        if sb == 0:
            row = lax.broadcasted_iota(jnp.int32, st.shape, 0)
            st = jnp.where(jnp.logical_or(has_prev, row >= WINDOW), st, NEG_BIG)
        if sb == N_SUB - 1:
            row = lax.broadcasted_iota(jnp.int32, st.shape, 0)
            st = jnp.where(jnp.logical_or(has_next, row < 2 * WINDOW), st, NEG_BIG)
        mx = jnp.maximum(jnp.max(st, axis=0, keepdims=True), sink)
        e = jnp.exp2(st - mx)
        denom = jnp.sum(e, axis=0, keepdims=True) + jnp.exp2(sink - mx)
        ot = jnp.dot(vs, e.astype(jnp.bfloat16), preferred_element_type=jnp.float32) / denom
        for h in range(HEADS_PER_KV):
            o_ref[0, h * HEAD_DIM:(h + 1) * HEAD_DIM, lo:lo + WINDOW] = (
                ot[:, h * WINDOW:(h + 1) * WINDOW])


def _window_attention(q_t, k, v_t, bias, sink):
    b, _, s = q_t.shape
    n_blk = s // WINDOW
    per = WQ_TILE // WINDOW
    vper = V_CHUNK // WINDOW

    def prev_blk(i):
        return jnp.maximum(i * per - 1, 0)

    def next_blk(i):
        return jnp.minimum(i * per + per, n_blk - 1)

    kh = lambda g: KV_HEADS + g
    return pl.pallas_call(
        _window_kernel,
        name="window_attn",
        grid=(b, KV_HEADS, s // WQ_TILE),
        in_specs=[
            pl.BlockSpec((1, Q_PER_KV_ROWS, WQ_TILE), lambda bi, g, i: (bi, g, i)),
            pl.BlockSpec((1, 1, WINDOW, HEAD_DIM), lambda bi, g, i: (bi, kh(g), prev_blk(i), 0)),
            pl.BlockSpec((1, 1, WQ_TILE, HEAD_DIM), lambda bi, g, i: (bi, kh(g), i, 0)),
            pl.BlockSpec((1, 1, WINDOW, HEAD_DIM), lambda bi, g, i: (bi, kh(g), next_blk(i), 0)),
            pl.BlockSpec((1, 1, HEAD_DIM, WINDOW),
                         lambda bi, g, i: (bi, prev_blk(i) // vper, kh(g), prev_blk(i) % vper)),
            pl.BlockSpec((1, 1, HEAD_DIM, WQ_TILE), lambda bi, g, i: (bi, i, kh(g), 0)),
            pl.BlockSpec((1, 1, HEAD_DIM, WINDOW),
                         lambda bi, g, i: (bi, next_blk(i) // vper, kh(g), next_blk(i) % vper)),
            pl.BlockSpec((1, BAND, HEADS_PER_KV * WINDOW), lambda bi, g, i: (g, 0, 0)),
            pl.BlockSpec((1, 1, HEADS_PER_KV * WINDOW), lambda bi, g, i: (g, 0, 0)),
        ],
        out_specs=pl.BlockSpec((1, Q_PER_KV_ROWS, WQ_TILE), lambda bi, g, i: (bi, g, i)),
        out_shape=jax.ShapeDtypeStruct((b, GROUP_WIDTH, s), jnp.float32),
        compiler_params=pltpu.CompilerParams(
            dimension_semantics=("arbitrary", "arbitrary", "arbitrary"),
            vmem_limit_bytes=VMEM_LIMIT),
    )(q_t, k, k, k, v_t, v_t, v_t, bias, sink)


def _mix_out_kernel(x_ref, oa_ref, ob_ref, ga_ref, gb_ref, w_ref, post_ref, o_ref):
    na = _rms_cols(oa_ref[0], ga_ref[...])
    nb = _rms_cols(ob_ref[0], gb_ref[...])
    mt = jnp.concatenate([na, nb], axis=0).astype(jnp.bfloat16)
    ht = jnp.dot(w_ref[...], mt, preferred_element_type=jnp.float32)
    ms = jnp.mean(ht * ht, axis=0, keepdims=True)
    yt = ht * lax.rsqrt(ms + NORM_EPS)
    o_ref[0] = x_ref[0] + yt.T * post_ref[...]


def _mix_out(x, oa_t, ob_t, ga, gb, w_out_t, post):
    b, s, _ = x.shape
    return pl.pallas_call(
        _mix_out_kernel,
        name="mix_out",
        grid=(b, s // TOKEN_TILE),
        in_specs=[
            pl.BlockSpec((1, TOKEN_TILE, D_MODEL), lambda bi, i: (bi, i, 0)),
            pl.BlockSpec((1, GROUP_WIDTH, TOKEN_TILE), lambda bi, i: (bi, 0, i)),
            pl.BlockSpec((1, GROUP_WIDTH, TOKEN_TILE), lambda bi, i: (bi, 0, i)),
            pl.BlockSpec((GROUP_WIDTH, 1), lambda bi, i: (0, 0)),
            pl.BlockSpec((GROUP_WIDTH, 1), lambda bi, i: (0, 0)),
            pl.BlockSpec((D_MODEL, D_MODEL), lambda bi, i: (0, 0)),
            pl.BlockSpec((1, D_MODEL), lambda bi, i: (0, 0)),
        ],
        out_specs=pl.BlockSpec((1, TOKEN_TILE, D_MODEL), lambda bi, i: (bi, i, 0)),
        out_shape=jax.ShapeDtypeStruct((b, s, D_MODEL), jnp.float32),
        compiler_params=pltpu.CompilerParams(
            dimension_semantics=("arbitrary", "arbitrary"), vmem_limit_bytes=VMEM_LIMIT),
    )(x, oa_t, ob_t, ga, gb, w_out_t, post)


_DEINTERLEAVE = np.concatenate([np.arange(0, HEAD_DIM, 2), np.arange(1, HEAD_DIM, 2)])


def _w_in_rows():
    n_rot = (Q_HEADS + KV_HEADS) * HEAD_DIM
    rot = (np.arange(Q_HEADS + KV_HEADS)[:, None] * HEAD_DIM + _DEINTERLEAVE[None, :]).reshape(-1)
    return np.concatenate([rot, np.arange(n_rot, IN_WIDTH)])


def _rope_tables_t(seq_len):
    n_rows = seq_len // GRID_W
    row = jnp.repeat(jnp.arange(n_rows, dtype=jnp.float32), GRID_W)
    col = jnp.tile(jnp.arange(GRID_W, dtype=jnp.float32), n_rows)
    n_freq = HEAD_DIM // 4
    inv_freq = ROPE_THETA ** (-jnp.arange(n_freq, dtype=jnp.float32) / n_freq)
    ang = jnp.concatenate([inv_freq[:, None] * row[None, :], inv_freq[:, None] * col[None, :]], axis=0)
    return jnp.cos(ang), jnp.sin(ang)


def _window_bias():
    slopes = 2.0 ** (-8.0 * np.arange(1, Q_HEADS + 1, dtype=np.float64) / Q_HEADS)
    kpos = np.arange(BAND)[:, None] - WINDOW
    qpos = np.arange(WINDOW)[None, :]
    dist = np.abs(qpos - kpos).astype(np.float64)
    per_head = np.where(dist[None] <= WINDOW, -slopes[:, None, None] * dist[None] * LOG2E, NEG_BIG)
    per_kv = per_head.reshape(KV_HEADS, HEADS_PER_KV, BAND, WINDOW).transpose(0, 2, 1, 3)
    return jnp.asarray(per_kv.reshape(KV_HEADS, BAND, HEADS_PER_KV * WINDOW), jnp.float32)


def _ffn_weights(w_gate, w_up, w_down):
    wg = w_gate.reshape(D_MODEL, N_FF_CHUNKS, FF_CHUNK)
    wu = w_up.reshape(D_MODEL, N_FF_CHUNKS, FF_CHUNK)
    wgu = jnp.concatenate([wg, wu], axis=-1).transpose(1, 0, 2).astype(jnp.bfloat16)
    wd = w_down.reshape(N_FF_CHUNKS, FF_CHUNK, D_MODEL).astype(jnp.bfloat16)
    return wgu, wd


def _trunk(x, layers, bias):
    b, s, _ = x.shape
    cos_t, sin_t = _rope_tables_t(s)
    for p in layers:
        x = _ffn(x.reshape(b * s, D_MODEL), p["ffn1_pre"], p["ffn1_post"], p["ffn1_wgu"],
                 p["ffn1_wd"]).reshape(b, s, D_MODEL)
        qa_t, qb_t, k, v_t, k_norm2 = _mix_in(x, p["mix_pre"], p["w_in_t"], p["gq"], p["gk"],
                                              cos_t, sin_t)
        oa_t = _global_attention(qa_t, k, v_t, k_norm2)
        ob_t = _window_attention(qb_t, k, v_t, bias, p["sink"])
        x = _mix_out(x, oa_t, ob_t, p["ga"], p["gb"], p["w_out_t"], p["mix_post"])
        x = _ffn(x.reshape(b * s, D_MODEL), p["ffn2_pre"], p["ffn2_post"], p["ffn2_wgu"],
                 p["ffn2_wd"]).reshape(b, s, D_MODEL)
    return x


def kernel(x_prompt, x_sample, ffn1_pre, ffn1_post, ffn1_w_gate, ffn1_w_up, ffn1_w_down, mix_pre, mix_post, w_in, a_q_norm, a_k_norm, b_sink, a_out_norm, b_out_norm, w_out, ffn2_pre, ffn2_post, ffn2_w_gate, ffn2_w_up, ffn2_w_down):
    depth = w_in.shape[0]
    rows = _w_in_rows()
    layers = []
    for l in range(depth):
        wgu1, wd1 = _ffn_weights(ffn1_w_gate[l], ffn1_w_up[l], ffn1_w_down[l])
        wgu2, wd2 = _ffn_weights(ffn2_w_gate[l], ffn2_w_up[l], ffn2_w_down[l])
        sink = jnp.repeat(b_sink[l].astype(jnp.float32) * LOG2E, WINDOW)
        layers.append(dict(
            ffn1_pre=ffn1_pre[l][None, :], ffn1_post=ffn1_post[l][None, :], ffn1_wgu=wgu1, ffn1_wd=wd1,
            ffn2_pre=ffn2_pre[l][None, :], ffn2_post=ffn2_post[l][None, :], ffn2_wgu=wgu2, ffn2_wd=wd2,
            mix_pre=mix_pre[l][None, :], mix_post=mix_post[l][None, :],
            w_in_t=w_in[l].T[rows].astype(jnp.bfloat16),
            gq=a_q_norm[l][_DEINTERLEAVE][:, None], gk=a_k_norm[l][_DEINTERLEAVE][:, None],
            sink=sink.reshape(KV_HEADS, 1, HEADS_PER_KV * WINDOW),
            ga=a_out_norm[l][:, None], gb=b_out_norm[l][:, None],
            w_out_t=w_out[l].T.astype(jnp.bfloat16),
        ))
    bias = _window_bias()
    return (_trunk(x_prompt, layers, bias), _trunk(x_sample, layers, bias))
```

```python
import functools
import math

import jax
import jax.numpy as jnp
import numpy as np
from jax import lax
from jax.experimental import pallas as pl
from jax.experimental.pallas import tpu as pltpu

D_MODEL = 1024
HEAD_DIM = 64
HALF_DIM = HEAD_DIM // 2
Q_HEADS = 8
KV_HEADS = 2
HEADS_PER_KV = Q_HEADS // KV_HEADS
GROUP_WIDTH = Q_HEADS * HEAD_DIM
KV_WIDTH = KV_HEADS * HEAD_DIM
Q_PER_KV_ROWS = HEADS_PER_KV * HEAD_DIM
GROUP_IN = GROUP_WIDTH + 2 * KV_WIDTH
IN_WIDTH = 2 * GROUP_IN
D_FF = 2816
GRID_W = 64
ROPE_THETA = 10000.0
WINDOW = 128
NORM_EPS = 1e-6
FFN_RESID = 0.5
LOG2E = math.log2(math.e)
Q_SCALE = HEAD_DIM ** -0.5 * LOG2E
NEG_BIG = -1e30
BOUND_SLACK = 1.01
MAX_FIXED_SHIFT = 48.0

FF_CHUNK = 256
N_FF_CHUNKS = D_FF // FF_CHUNK
TOKEN_TILE = 512
V_CHUNK = TOKEN_TILE
Q_TILE = 512
ONES_ROWS = 16
ROW_BLOCK = 128
FIXED_UNROLL = 4
WQ_TILE = 512
N_SUB = WQ_TILE // WINDOW
BAND = 3 * WINDOW

VMEM_LIMIT = 56 * 1024 * 1024


def _rms_rows(x, gain_row):
    ms = jnp.mean(x * x, axis=-1, keepdims=True)
    return x * lax.rsqrt(ms + NORM_EPS) * gain_row


def _rms_cols(xt, gain_col):
    ms = jnp.mean(xt * xt, axis=0, keepdims=True)
    return xt * lax.rsqrt(ms + NORM_EPS) * gain_col


def _zero_after(x):
    bits = lax.bitcast_convert_type(x, jnp.uint32)
    bits = lax.shift_right_logical(lax.shift_right_logical(bits, jnp.uint32(16)), jnp.uint32(16))
    return bits.astype(jnp.float32)


def _ffn_kernel(x_ref, pre_ref, post_ref, wgu_ref, wd_ref, o_ref, xn_ref, acc_ref):
    x = x_ref[...]
    xn_ref[...] = _rms_rows(x, pre_ref[...]).astype(jnp.bfloat16)
    acc_ref[...] = jnp.zeros_like(acc_ref)

    for c in range(N_FF_CHUNKS):
        gu = jnp.dot(xn_ref[...], wgu_ref[c], preferred_element_type=jnp.float32)
        g = gu[:, :FF_CHUNK]
        u = gu[:, FF_CHUNK:]
        h = (g * jax.nn.sigmoid(g) * u).astype(jnp.bfloat16)
        acc_ref[...] += jnp.dot(h, wd_ref[c], preferred_element_type=jnp.float32)
    o_ref[...] = x + FFN_RESID * _rms_rows(acc_ref[...], post_ref[...])


def _ffn(x, pre, post, wgu, wd):
    t = x.shape[0]
    const3 = lambda i: (0, 0, 0)
    return pl.pallas_call(
        _ffn_kernel,
        name="ffn",
        grid=(t // TOKEN_TILE,),
        in_specs=[
            pl.BlockSpec((TOKEN_TILE, D_MODEL), lambda i: (i, 0)),
            pl.BlockSpec((1, D_MODEL), lambda i: (0, 0)),
            pl.BlockSpec((1, D_MODEL), lambda i: (0, 0)),
            pl.BlockSpec((N_FF_CHUNKS, D_MODEL, 2 * FF_CHUNK), const3),
            pl.BlockSpec((N_FF_CHUNKS, FF_CHUNK, D_MODEL), const3),
        ],
        out_specs=pl.BlockSpec((TOKEN_TILE, D_MODEL), lambda i: (i, 0)),
        out_shape=jax.ShapeDtypeStruct((t, D_MODEL), jnp.float32),
        scratch_shapes=[
            pltpu.VMEM((TOKEN_TILE, D_MODEL), jnp.bfloat16),
            pltpu.VMEM((TOKEN_TILE, D_MODEL), jnp.float32),
        ],
        compiler_params=pltpu.CompilerParams(
            dimension_semantics=("arbitrary",), vmem_limit_bytes=VMEM_LIMIT),
    )(x, pre, post, wgu, wd)


def _rope_norm_head(xt, gain_col, cos_t, sin_t):
    y = _rms_cols(xt, gain_col)
    x0 = y[:HALF_DIM]
    x1 = y[HALF_DIM:]
    return jnp.concatenate([x0 * cos_t - x1 * sin_t, x0 * sin_t + x1 * cos_t], axis=0)


def _mix_in_kernel(x_ref, pre_ref, w_ref, gq_ref, gk_ref, cos_ref, sin_ref,
                   qa_ref, qb_ref, k_ref, v_ref, qn_ref, kn_ref):
    xn = _rms_rows(x_ref[0], pre_ref[...]).astype(jnp.bfloat16)
    pt = lax.dot_general(w_ref[...], xn, (((1,), (1,)), ((), ())),
                         preferred_element_type=jnp.float32)
    pa = pt[:GROUP_IN]
    pb = pt[GROUP_IN:]
    cos_t = cos_ref[...]
    sin_t = sin_ref[...]
    gq = gq_ref[...]
    gk = gk_ref[...]
    off = 0
    for h in range(Q_HEADS):
        r = _rope_norm_head(pa[off:off + HEAD_DIM], gq, cos_t, sin_t) * Q_SCALE
        qa_ref[0, h * HEAD_DIM:(h + 1) * HEAD_DIM, :] = r.astype(jnp.bfloat16)
        qn_ref[0, h] = jnp.sum(r * r, axis=0, keepdims=True)
        off += HEAD_DIM
    for h in range(KV_HEADS):
        r = _rope_norm_head(pa[off:off + HEAD_DIM], gk, cos_t, sin_t)
        k_ref[0, h] = r.T.astype(jnp.bfloat16)
        kn_ref[0, h] = jnp.sum(r * r, axis=0, keepdims=True)
        off += HEAD_DIM
    v_ref[0, 0, :KV_WIDTH, :] = pa[off:off + KV_WIDTH].astype(jnp.bfloat16)
    qb_ref[0] = (pb[:GROUP_WIDTH] * Q_SCALE).astype(jnp.bfloat16)
    off = GROUP_WIDTH
    for h in range(KV_HEADS):
        k_ref[0, KV_HEADS + h] = pb[off:off + HEAD_DIM].T.astype(jnp.bfloat16)
        off += HEAD_DIM
    v_ref[0, 0, KV_WIDTH:, :] = pb[off:off + KV_WIDTH].astype(jnp.bfloat16)


def _mix_in(x, pre, w_in_t, gq, gk, cos_t, sin_t):
    b, s, _ = x.shape
    n_tiles = s // TOKEN_TILE
    return pl.pallas_call(
        _mix_in_kernel,
        name="mix_in",
        grid=(b, n_tiles),
        in_specs=[
            pl.BlockSpec((1, TOKEN_TILE, D_MODEL), lambda bi, i: (bi, i, 0)),
            pl.BlockSpec((1, D_MODEL), lambda bi, i: (0, 0)),
            pl.BlockSpec((IN_WIDTH, D_MODEL), lambda bi, i: (0, 0)),
            pl.BlockSpec((HEAD_DIM, 1), lambda bi, i: (0, 0)),
            pl.BlockSpec((HEAD_DIM, 1), lambda bi, i: (0, 0)),
            pl.BlockSpec((HALF_DIM, TOKEN_TILE), lambda bi, i: (0, i)),
            pl.BlockSpec((HALF_DIM, TOKEN_TILE), lambda bi, i: (0, i)),
        ],
        out_specs=[
            pl.BlockSpec((1, GROUP_WIDTH, TOKEN_TILE), lambda bi, i: (bi, 0, i)),
            pl.BlockSpec((1, GROUP_WIDTH, TOKEN_TILE), lambda bi, i: (bi, 0, i)),
            pl.BlockSpec((1, 2 * KV_HEADS, TOKEN_TILE, HEAD_DIM), lambda bi, i: (bi, 0, i, 0)),
            pl.BlockSpec((1, 1, 2 * KV_WIDTH, V_CHUNK), lambda bi, i: (bi, i, 0, 0)),
            pl.BlockSpec((1, Q_HEADS, 1, TOKEN_TILE), lambda bi, i: (bi, 0, 0, i)),
            pl.BlockSpec((1, KV_HEADS, 1, TOKEN_TILE), lambda bi, i: (bi, 0, 0, i)),
        ],
        out_shape=[
            jax.ShapeDtypeStruct((b, GROUP_WIDTH, s), jnp.bfloat16),
            jax.ShapeDtypeStruct((b, GROUP_WIDTH, s), jnp.bfloat16),
            jax.ShapeDtypeStruct((b, 2 * KV_HEADS, s, HEAD_DIM), jnp.bfloat16),
            jax.ShapeDtypeStruct((b, n_tiles, 2 * KV_WIDTH, V_CHUNK), jnp.bfloat16),
            jax.ShapeDtypeStruct((b, Q_HEADS, 1, s), jnp.float32),
            jax.ShapeDtypeStruct((b, KV_HEADS, 1, s), jnp.float32),
        ],
        compiler_params=pltpu.CompilerParams(
            dimension_semantics=("arbitrary", "arbitrary"), vmem_limit_bytes=VMEM_LIMIT),
    )(x, pre, w_in_t, gq, gk, cos_t, sin_t)


def _global_kernel(q_ref, k_ref, v_ref, qn_ref, kn_ref, o_ref, s_ref, smax_ref, p_ref, alpha_ref, m_ref,
                   c_ref, acc_ref, *, n_chunks):
    ones = jnp.ones((ONES_ROWS, V_CHUNK), jnp.bfloat16)
    n_blocks = V_CHUNK // ROW_BLOCK
    acc_ref[...] = jnp.zeros_like(acc_ref)

    kmax2 = jnp.max(kn_ref[0, 0], axis=1, keepdims=True)
    c_ref[...] = BOUND_SLACK * jnp.sqrt(qn_ref[0] * kmax2)
    use_fixed_shift = jnp.max(c_ref[...]) <= MAX_FIXED_SHIFT

    def k_rows(chunk, r):
        start = chunk * V_CHUNK + r * ROW_BLOCK
        if not isinstance(start, int):
            start = pl.multiple_of(start, ROW_BLOCK)
        return k_ref[0, 0, pl.ds(start, ROW_BLOCK), :]

    def v_ext(chunk):
        return jnp.concatenate([v_ref[0, chunk], ones], axis=0)

    def values_block(vext, slot, h, r):
        cols = slice((r - 1) * ROW_BLOCK, (r + 1) * ROW_BLOCK)
        return jnp.dot(vext[:, cols], p_ref[slot, h, cols, :], preferred_element_type=jnp.float32)

    def fixed_step(j, parity, do_scores=True, do_values=True):
        cur, other = parity, 1 - parity
        if do_values:
            vext = v_ext(j)
        after_values = None
        for h in range(HEADS_PER_KV):
            q_h = q_ref[0, h * HEAD_DIM:(h + 1) * HEAD_DIM, :]
            c_h = c_ref[h]
            pv = None
            for r in range(n_blocks):
                rows = slice(r * ROW_BLOCK, (r + 1) * ROW_BLOCK)
                if do_scores:
                    st = jnp.dot(k_rows(j + 1, r), q_h, preferred_element_type=jnp.float32)
                    shift = c_h if after_values is None else c_h + after_values
                    p_ref[other, h, rows, :] = jnp.exp2(st - shift).astype(jnp.bfloat16)
                if do_values and r % 2 == 1:
                    d = values_block(vext, cur, h, r)
                    pv = d if pv is None else pv + d
                    after_values = _zero_after(d[HEAD_DIM:HEAD_DIM + 1])
            if do_values:
                acc_ref[h] += pv

    def online_step(j, parity, do_scores=True, do_softmax=True, do_values=True):
        cur, other = parity, 1 - parity
        if do_values:
            vext = v_ext(j - 1)
        for h in range(HEADS_PER_KV):
            q_h = q_ref[0, h * HEAD_DIM:(h + 1) * HEAD_DIM, :]
            if do_softmax:
                m_old = m_ref[h]
                m_new = jnp.maximum(m_old, smax_ref[cur, h])
                alpha_ref[cur, h] = jnp.exp2(m_old - m_new)
                m_ref[h] = m_new
            blk_max = None
            pv = None
            for r in range(n_blocks):
                rows = slice(r * ROW_BLOCK, (r + 1) * ROW_BLOCK)
                if do_scores:
                    st = jnp.dot(k_rows(j + 1, r), q_h, preferred_element_type=jnp.float32)
                    s_ref[other, h, rows, :] = st
                    part = jnp.max(st.reshape(ROW_BLOCK // 8, 8, Q_TILE), axis=0)
                    blk_max = part if blk_max is None else jnp.maximum(blk_max, part)
                if do_softmax:
                    p_ref[cur, h, rows, :] = jnp.exp2(s_ref[cur, h, rows, :] - m_new).astype(
                        jnp.bfloat16)
                if do_values and r % 2 == 1:
                    d = values_block(vext, other, h, r)
                    pv = d if pv is None else pv + d
            if do_scores:
                smax_ref[other, h] = jnp.max(blk_max, axis=0, keepdims=True)
            if do_values:
                acc_ref[h] = alpha_ref[other, h] * acc_ref[h] + pv

    @pl.when(use_fixed_shift)
    def _():
        fixed_step(-1, 1, do_values=False)

        def steps(t, carry):
            for u in range(FIXED_UNROLL):
                fixed_step(FIXED_UNROLL * t + u, u % 2)
            return carry

        n_loop = (n_chunks - 2) // FIXED_UNROLL
        lax.fori_loop(0, n_loop, steps, 0)
        for j in range(n_loop * FIXED_UNROLL, n_chunks - 2):
            fixed_step(j, j % 2)
        fixed_step(n_chunks - 2, 0)
        fixed_step(n_chunks - 1, 1, do_scores=False)

    @pl.when(jnp.logical_not(use_fixed_shift))
    def _():
        m_ref[...] = jnp.full_like(m_ref, NEG_BIG)
        online_step(-1, 1, do_softmax=False, do_values=False)
        online_step(0, 0, do_values=False)

        def pair(t, carry):
            online_step(2 * t + 1, 1)
            online_step(2 * t + 2, 0)
            return carry

        lax.fori_loop(0, (n_chunks - 2) // 2, pair, 0)
        online_step(n_chunks - 1, 1, do_scores=False)
        online_step(n_chunks, 0, do_scores=False, do_softmax=False)

    for h in range(HEADS_PER_KV):
        a = acc_ref[h]
        o_ref[0, h * HEAD_DIM:(h + 1) * HEAD_DIM, :] = a[:HEAD_DIM] / a[HEAD_DIM:HEAD_DIM + 1]


def _global_attention(q_t, k, v_t, q_norm2, k_norm2):
    b, _, s = q_t.shape
    n_chunks = s // V_CHUNK
    assert n_chunks >= 2 and n_chunks % 2 == 0, "the kv pipeline runs chunks in pairs"
    return pl.pallas_call(
        functools.partial(_global_kernel, n_chunks=n_chunks),
        name="global_attn",
        grid=(b, KV_HEADS, s // Q_TILE),
        in_specs=[
            pl.BlockSpec((1, Q_PER_KV_ROWS, Q_TILE), lambda bi, g, i: (bi, g, i)),
            pl.BlockSpec((1, 1, s, HEAD_DIM), lambda bi, g, i: (bi, g, 0, 0)),
            pl.BlockSpec((1, n_chunks, HEAD_DIM, V_CHUNK), lambda bi, g, i: (bi, 0, g, 0)),
            pl.BlockSpec((1, HEADS_PER_KV, 1, Q_TILE), lambda bi, g, i: (bi, g, 0, i)),
            pl.BlockSpec((1, 1, 1, s), lambda bi, g, i: (bi, g, 0, 0)),
        ],
        out_specs=pl.BlockSpec((1, Q_PER_KV_ROWS, Q_TILE), lambda bi, g, i: (bi, g, i)),
        out_shape=jax.ShapeDtypeStruct((b, GROUP_WIDTH, s), jnp.float32),
        scratch_shapes=[
            pltpu.VMEM((2, HEADS_PER_KV, V_CHUNK, Q_TILE), jnp.float32),
            pltpu.VMEM((2, HEADS_PER_KV, 1, Q_TILE), jnp.float32),
            pltpu.VMEM((2, HEADS_PER_KV, V_CHUNK, Q_TILE), jnp.bfloat16),
            pltpu.VMEM((2, HEADS_PER_KV, 1, Q_TILE), jnp.float32),
            pltpu.VMEM((HEADS_PER_KV, 1, Q_TILE), jnp.float32),
            pltpu.VMEM((HEADS_PER_KV, 1, Q_TILE), jnp.float32),
            pltpu.VMEM((HEADS_PER_KV, HEAD_DIM + ONES_ROWS, Q_TILE), jnp.float32),
        ],
        compiler_params=pltpu.CompilerParams(
            dimension_semantics=("arbitrary", "arbitrary", "arbitrary"),
            vmem_limit_bytes=VMEM_LIMIT),
    )(q_t, k, v_t, q_norm2, k_norm2)


def _window_kernel(q_ref, kp_ref, kc_ref, kn_ref, vp_ref, vc_ref, vn_ref, bias_ref, sink_ref,
                   o_ref, s_ref, mx_ref, p_ref):
    i = pl.program_id(2)
    has_prev = i > 0
    has_next = i < pl.num_programs(2) - 1
    sink = sink_ref[0]
    ones = jnp.ones((ONES_ROWS, BAND), jnp.bfloat16)
    no_prev = jnp.where(has_prev, 0.0, NEG_BIG)
    no_next = jnp.where(has_next, 0.0, NEG_BIG)

    def band(sb, prev_ref, cur_ref, next_ref, axis):
        def cur(lo, hi):
            idx = (0, 0, slice(lo, hi), slice(None)) if axis == 0 else (0, 0, slice(None), slice(lo, hi))
            return cur_ref[idx]
        lo = (sb - 1) * WINDOW
        if sb == 0:
            parts = [prev_ref[0, 0], cur(0, 2 * WINDOW)]
        elif sb == N_SUB - 1:
            parts = [cur(lo, WQ_TILE), next_ref[0, 0]]
        else:
            return cur(lo, lo + BAND)
        return jnp.concatenate(parts, axis=axis)

    def scores(sb):
        lo = sb * WINDOW
        qs = jnp.concatenate(
            [q_ref[0, h * HEAD_DIM:(h + 1) * HEAD_DIM, lo:lo + WINDOW] for h in range(HEADS_PER_KV)],
            axis=1)
        ks = band(sb, kp_ref, kc_ref, kn_ref, 0)
        st = jnp.dot(ks, qs, preferred_element_type=jnp.float32) + bias_ref[0]
        if sb == 0:
            st = jnp.concatenate([st[:WINDOW] + no_prev, st[WINDOW:]], axis=0)
        if sb == N_SUB - 1:
            st = jnp.concatenate([st[:2 * WINDOW], st[2 * WINDOW:] + no_next], axis=0)
        s_ref[sb] = st
        mx_ref[sb] = jnp.maximum(jnp.max(st, axis=0, keepdims=True), sink)

    def probabilities(sb):
        p_ref[sb] = jnp.exp2(s_ref[sb] - mx_ref[sb]).astype(jnp.bfloat16)

    def values(sb):
        lo = sb * WINDOW
        vext = jnp.concatenate([band(sb, vp_ref, vc_ref, vn_ref, 1), ones], axis=0)
        ot = jnp.dot(vext, p_ref[sb], preferred_element_type=jnp.float32)
        denom = ot[HEAD_DIM:HEAD_DIM + 1] + jnp.exp2(sink - mx_ref[sb])
        ot = ot[:HEAD_DIM] / denom
        for h in range(HEADS_PER_KV):
            o_ref[0, h * HEAD_DIM:(h + 1) * HEAD_DIM, lo:lo + WINDOW] = (
                ot[:, h * WINDOW:(h + 1) * WINDOW])

    for t in range(N_SUB + 2):
        if t < N_SUB:
            scores(t)
        if 1 <= t <= N_SUB:
            probabilities(t - 1)
        if t >= 2:
            values(t - 2)


def _window_attention(q_t, k, v_t, bias, sink):
    b, _, s = q_t.shape
    n_blk = s // WINDOW
    per = WQ_TILE // WINDOW
    vper = V_CHUNK // WINDOW

    def prev_blk(i):
        return jnp.maximum(i * per - 1, 0)

    def next_blk(i):
        return jnp.minimum(i * per + per, n_blk - 1)

    kh = lambda g: KV_HEADS + g
    return pl.pallas_call(
        _window_kernel,
        name="window_attn",
        grid=(b, KV_HEADS, s // WQ_TILE),
        in_specs=[
            pl.BlockSpec((1, Q_PER_KV_ROWS, WQ_TILE), lambda bi, g, i: (bi, g, i)),
            pl.BlockSpec((1, 1, WINDOW, HEAD_DIM), lambda bi, g, i: (bi, kh(g), prev_blk(i), 0)),
            pl.BlockSpec((1, 1, WQ_TILE, HEAD_DIM), lambda bi, g, i: (bi, kh(g), i, 0)),
            pl.BlockSpec((1, 1, WINDOW, HEAD_DIM), lambda bi, g, i: (bi, kh(g), next_blk(i), 0)),
            pl.BlockSpec((1, 1, HEAD_DIM, WINDOW),
                         lambda bi, g, i: (bi, prev_blk(i) // vper, kh(g), prev_blk(i) % vper)),
            pl.BlockSpec((1, 1, HEAD_DIM, WQ_TILE), lambda bi, g, i: (bi, i, kh(g), 0)),
            pl.BlockSpec((1, 1, HEAD_DIM, WINDOW),
                         lambda bi, g, i: (bi, next_blk(i) // vper, kh(g), next_blk(i) % vper)),
            pl.BlockSpec((1, BAND, HEADS_PER_KV * WINDOW), lambda bi, g, i: (g, 0, 0)),
            pl.BlockSpec((1, 1, HEADS_PER_KV * WINDOW), lambda bi, g, i: (g, 0, 0)),
        ],
        out_specs=pl.BlockSpec((1, Q_PER_KV_ROWS, WQ_TILE), lambda bi, g, i: (bi, g, i)),
        out_shape=jax.ShapeDtypeStruct((b, GROUP_WIDTH, s), jnp.float32),
        scratch_shapes=[
            pltpu.VMEM((N_SUB, BAND, HEADS_PER_KV * WINDOW), jnp.float32),
            pltpu.VMEM((N_SUB, 1, HEADS_PER_KV * WINDOW), jnp.float32),
            pltpu.VMEM((N_SUB, BAND, HEADS_PER_KV * WINDOW), jnp.bfloat16),
        ],
        compiler_params=pltpu.CompilerParams(
            dimension_semantics=("arbitrary", "arbitrary", "arbitrary"),
            vmem_limit_bytes=VMEM_LIMIT),
    )(q_t, k, k, k, v_t, v_t, v_t, bias, sink)


def _mix_out_kernel(x_ref, oa_ref, ob_ref, ga_ref, gb_ref, w_ref, post_ref, o_ref):
    na = _rms_cols(oa_ref[0], ga_ref[...])
    nb = _rms_cols(ob_ref[0], gb_ref[...])
    mt = jnp.concatenate([na, nb], axis=0).astype(jnp.bfloat16)
    h = lax.dot_general(mt, w_ref[...], (((0,), (0,)), ((), ())),
                        preferred_element_type=jnp.float32)
    o_ref[0] = x_ref[0] + _rms_rows(h, post_ref[...])


def _mix_out(x, oa_t, ob_t, ga, gb, w_out, post):
    b, s, _ = x.shape
    return pl.pallas_call(
        _mix_out_kernel,
        name="mix_out",
        grid=(b, s // TOKEN_TILE),
        in_specs=[
            pl.BlockSpec((1, TOKEN_TILE, D_MODEL), lambda bi, i: (bi, i, 0)),
            pl.BlockSpec((1, GROUP_WIDTH, TOKEN_TILE), lambda bi, i: (bi, 0, i)),
            pl.BlockSpec((1, GROUP_WIDTH, TOKEN_TILE), lambda bi, i: (bi, 0, i)),
            pl.BlockSpec((GROUP_WIDTH, 1), lambda bi, i: (0, 0)),
            pl.BlockSpec((GROUP_WIDTH, 1), lambda bi, i: (0, 0)),
            pl.BlockSpec((D_MODEL, D_MODEL), lambda bi, i: (0, 0)),
            pl.BlockSpec((1, D_MODEL), lambda bi, i: (0, 0)),
        ],
        out_specs=pl.BlockSpec((1, TOKEN_TILE, D_MODEL), lambda bi, i: (bi, i, 0)),
        out_shape=jax.ShapeDtypeStruct((b, s, D_MODEL), jnp.float32),
        compiler_params=pltpu.CompilerParams(
            dimension_semantics=("arbitrary", "arbitrary"), vmem_limit_bytes=VMEM_LIMIT),
    )(x, oa_t, ob_t, ga, gb, w_out, post)


_DEINTERLEAVE = np.concatenate([np.arange(0, HEAD_DIM, 2), np.arange(1, HEAD_DIM, 2)])


def _w_in_rows():
    n_rot = (Q_HEADS + KV_HEADS) * HEAD_DIM
    rot = (np.arange(Q_HEADS + KV_HEADS)[:, None] * HEAD_DIM + _DEINTERLEAVE[None, :]).reshape(-1)
    return np.concatenate([rot, np.arange(n_rot, IN_WIDTH)])


def _rope_tables_t(seq_len):
    n_rows = seq_len // GRID_W
    row = jnp.repeat(jnp.arange(n_rows, dtype=jnp.float32), GRID_W)
    col = jnp.tile(jnp.arange(GRID_W, dtype=jnp.float32), n_rows)
    n_freq = HEAD_DIM // 4
    inv_freq = ROPE_THETA ** (-jnp.arange(n_freq, dtype=jnp.float32) / n_freq)
    ang = jnp.concatenate([inv_freq[:, None] * row[None, :], inv_freq[:, None] * col[None, :]], axis=0)
    return jnp.cos(ang), jnp.sin(ang)


def _window_bias():
    slopes = 2.0 ** (-8.0 * np.arange(1, Q_HEADS + 1, dtype=np.float64) / Q_HEADS)
    kpos = np.arange(BAND)[:, None] - WINDOW
    qpos = np.arange(WINDOW)[None, :]
    dist = np.abs(qpos - kpos).astype(np.float64)
    per_head = np.where(dist[None] <= WINDOW, -slopes[:, None, None] * dist[None] * LOG2E, NEG_BIG)
    per_kv = per_head.reshape(KV_HEADS, HEADS_PER_KV, BAND, WINDOW).transpose(0, 2, 1, 3)
    return jnp.asarray(per_kv.reshape(KV_HEADS, BAND, HEADS_PER_KV * WINDOW), jnp.float32)


def _ffn_weights(w_gate, w_up, w_down):
    wg = w_gate.reshape(D_MODEL, N_FF_CHUNKS, FF_CHUNK)
    wu = w_up.reshape(D_MODEL, N_FF_CHUNKS, FF_CHUNK)
    wgu = jnp.concatenate([wg, wu], axis=-1).transpose(1, 0, 2).astype(jnp.bfloat16)
    wd = w_down.reshape(N_FF_CHUNKS, FF_CHUNK, D_MODEL).astype(jnp.bfloat16)
    return wgu, wd


def _trunk(x, layers, bias):
    b, s, _ = x.shape
    cos_t, sin_t = _rope_tables_t(s)
    for p in layers:
        x = _ffn(x.reshape(b * s, D_MODEL), p["ffn1_pre"], p["ffn1_post"], p["ffn1_wgu"],
                 p["ffn1_wd"]).reshape(b, s, D_MODEL)
        qa_t, qb_t, k, v_t, q_norm2, k_norm2 = _mix_in(x, p["mix_pre"], p["w_in_t"], p["gq"],
                                                       p["gk"], cos_t, sin_t)
        oa_t = _global_attention(qa_t, k, v_t, q_norm2, k_norm2)
        ob_t = _window_attention(qb_t, k, v_t, bias, p["sink"])
        x = _mix_out(x, oa_t, ob_t, p["ga"], p["gb"], p["w_out"], p["mix_post"])
        x = _ffn(x.reshape(b * s, D_MODEL), p["ffn2_pre"], p["ffn2_post"], p["ffn2_wgu"],
                 p["ffn2_wd"]).reshape(b, s, D_MODEL)
    return x


def kernel(x_prompt, x_sample, ffn1_pre, ffn1_post, ffn1_w_gate, ffn1_w_up, ffn1_w_down, mix_pre, mix_post, w_in, a_q_norm, a_k_norm, b_sink, a_out_norm, b_out_norm, w_out, ffn2_pre, ffn2_post, ffn2_w_gate, ffn2_w_up, ffn2_w_down):
    depth = w_in.shape[0]
    rows = _w_in_rows()
    layers = []
    for l in range(depth):
        wgu1, wd1 = _ffn_weights(ffn1_w_gate[l], ffn1_w_up[l], ffn1_w_down[l])
        wgu2, wd2 = _ffn_weights(ffn2_w_gate[l], ffn2_w_up[l], ffn2_w_down[l])
        sink = jnp.repeat(b_sink[l].astype(jnp.float32) * LOG2E, WINDOW)
        layers.append(dict(
            ffn1_pre=ffn1_pre[l][None, :], ffn1_post=ffn1_post[l][None, :], ffn1_wgu=wgu1, ffn1_wd=wd1,
            ffn2_pre=ffn2_pre[l][None, :], ffn2_post=ffn2_post[l][None, :], ffn2_wgu=wgu2, ffn2_wd=wd2,
            mix_pre=mix_pre[l][None, :], mix_post=mix_post[l][None, :],
            w_in_t=w_in[l].T[rows].astype(jnp.bfloat16),
            gq=a_q_norm[l][_DEINTERLEAVE][:, None], gk=a_k_norm[l][_DEINTERLEAVE][:, None],
            sink=sink.reshape(KV_HEADS, 1, HEADS_PER_KV * WINDOW),
            ga=a_out_norm[l][:, None], gb=b_out_norm[l][:, None],
            w_out=w_out[l].astype(jnp.bfloat16),
        ))
    bias = _window_bias()
    return (_trunk(x_prompt, layers, bias), _trunk(x_sample, layers, bias))
```

```python
import functools
import math

import jax
import jax.numpy as jnp
import numpy as np
from jax import lax
from jax.experimental import pallas as pl
from jax.experimental.pallas import tpu as pltpu

D_MODEL = 1024
HEAD_DIM = 64
HALF_DIM = HEAD_DIM // 2
Q_HEADS = 8
KV_HEADS = 2
HEADS_PER_KV = Q_HEADS // KV_HEADS
GROUP_WIDTH = Q_HEADS * HEAD_DIM
KV_WIDTH = KV_HEADS * HEAD_DIM
Q_PER_KV_ROWS = HEADS_PER_KV * HEAD_DIM
GROUP_IN = GROUP_WIDTH + 2 * KV_WIDTH
IN_WIDTH = 2 * GROUP_IN
D_FF = 2816
GRID_W = 64
ROPE_THETA = 10000.0
WINDOW = 128
NORM_EPS = 1e-6
FFN_RESID = 0.5
LOG2E = math.log2(math.e)
Q_SCALE = HEAD_DIM ** -0.5 * LOG2E
NEG_BIG = -1e30
BOUND_SLACK = 1.01
MAX_FIXED_SHIFT = 48.0

FF_CHUNK = 256
N_FF_CHUNKS = D_FF // FF_CHUNK
TOKEN_TILE = 512
V_CHUNK = TOKEN_TILE
Q_TILE = 512
ONES_ROWS = 16
ROW_BLOCK = 128
FIXED_UNROLL = 10
WQ_TILE = 512
N_SUB = WQ_TILE // WINDOW
BAND = 3 * WINDOW

VMEM_LIMIT = 56 * 1024 * 1024


def _rms_rows(x, gain_row):
    ms = jnp.mean(x * x, axis=-1, keepdims=True)
    return x * lax.rsqrt(ms + NORM_EPS) * gain_row


def _rms_cols(xt, gain_col):
    ms = jnp.mean(xt * xt, axis=0, keepdims=True)
    return xt * lax.rsqrt(ms + NORM_EPS) * gain_col


def _zero_after(x):
    bits = lax.bitcast_convert_type(x, jnp.uint32)
    bits = lax.shift_right_logical(lax.shift_right_logical(bits, jnp.uint32(16)), jnp.uint32(16))
    return bits.astype(jnp.float32)


def _ffn_kernel(x_ref, pre_ref, post_ref, wgu_ref, wd_ref, o_ref, xn_ref, acc_ref):
    x = x_ref[...]
    xn_ref[...] = _rms_rows(x, pre_ref[...]).astype(jnp.bfloat16)
    acc_ref[...] = jnp.zeros_like(acc_ref)

    for c in range(N_FF_CHUNKS):
        gu = jnp.dot(xn_ref[...], wgu_ref[c], preferred_element_type=jnp.float32)
        g = gu[:, :FF_CHUNK]
        u = gu[:, FF_CHUNK:]
        h = (g * jax.nn.sigmoid(g) * u).astype(jnp.bfloat16)
        acc_ref[...] += jnp.dot(h, wd_ref[c], preferred_element_type=jnp.float32)
    o_ref[...] = x + FFN_RESID * _rms_rows(acc_ref[...], post_ref[...])


def _ffn(x, pre, post, wgu, wd):
    t = x.shape[0]
    const3 = lambda i: (0, 0, 0)
    return pl.pallas_call(
        _ffn_kernel,
        name="ffn",
        grid=(t // TOKEN_TILE,),
        in_specs=[
            pl.BlockSpec((TOKEN_TILE, D_MODEL), lambda i: (i, 0)),
            pl.BlockSpec((1, D_MODEL), lambda i: (0, 0)),
            pl.BlockSpec((1, D_MODEL), lambda i: (0, 0)),
            pl.BlockSpec((N_FF_CHUNKS, D_MODEL, 2 * FF_CHUNK), const3),
            pl.BlockSpec((N_FF_CHUNKS, FF_CHUNK, D_MODEL), const3),
        ],
        out_specs=pl.BlockSpec((TOKEN_TILE, D_MODEL), lambda i: (i, 0)),
        out_shape=jax.ShapeDtypeStruct((t, D_MODEL), jnp.float32),
        scratch_shapes=[
            pltpu.VMEM((TOKEN_TILE, D_MODEL), jnp.bfloat16),
            pltpu.VMEM((TOKEN_TILE, D_MODEL), jnp.float32),
        ],
        compiler_params=pltpu.CompilerParams(
            dimension_semantics=("arbitrary",), vmem_limit_bytes=VMEM_LIMIT),
    )(x, pre, post, wgu, wd)


def _rope_norm_head(xt, gain_col, cos_t, sin_t):
    y = _rms_cols(xt, gain_col)
    x0 = y[:HALF_DIM]
    x1 = y[HALF_DIM:]
    return jnp.concatenate([x0 * cos_t - x1 * sin_t, x0 * sin_t + x1 * cos_t], axis=0)


def _mix_in_kernel(x_ref, pre_ref, w_ref, gq_ref, gk_ref, cos_ref, sin_ref,
                   qa_ref, qb_ref, k_ref, v_ref, qn_ref, kn_ref):
    xn = _rms_rows(x_ref[0], pre_ref[...]).astype(jnp.bfloat16)
    pt = lax.dot_general(w_ref[...], xn, (((1,), (1,)), ((), ())),
                         preferred_element_type=jnp.float32)
    pa = pt[:GROUP_IN]
    pb = pt[GROUP_IN:]
    cos_t = cos_ref[...]
    sin_t = sin_ref[...]
    gq = gq_ref[...]
    gk = gk_ref[...]
    off = 0
    for h in range(Q_HEADS):
        r = _rope_norm_head(pa[off:off + HEAD_DIM], gq, cos_t, sin_t) * Q_SCALE
        qa_ref[0, h * HEAD_DIM:(h + 1) * HEAD_DIM, :] = r.astype(jnp.bfloat16)
        qn_ref[0, h] = jnp.sum(r * r, axis=0, keepdims=True)
        off += HEAD_DIM
    for h in range(KV_HEADS):
        r = _rope_norm_head(pa[off:off + HEAD_DIM], gk, cos_t, sin_t)
        k_ref[0, h] = r.T.astype(jnp.bfloat16)
        kn_ref[0, h] = jnp.sum(r * r, axis=0, keepdims=True)
        off += HEAD_DIM
    v_ref[0, 0, :KV_WIDTH, :] = pa[off:off + KV_WIDTH].astype(jnp.bfloat16)
    qb_ref[0] = (pb[:GROUP_WIDTH] * Q_SCALE).astype(jnp.bfloat16)
    off = GROUP_WIDTH
    for h in range(KV_HEADS):
        k_ref[0, KV_HEADS + h] = pb[off:off + HEAD_DIM].T.astype(jnp.bfloat16)
        off += HEAD_DIM
    v_ref[0, 0, KV_WIDTH:, :] = pb[off:off + KV_WIDTH].astype(jnp.bfloat16)


def _mix_in(x, pre, w_in_t, gq, gk, cos_t, sin_t):
    b, s, _ = x.shape
    n_tiles = s // TOKEN_TILE
    return pl.pallas_call(
        _mix_in_kernel,
        name="mix_in",
        grid=(b, n_tiles),
        in_specs=[
            pl.BlockSpec((1, TOKEN_TILE, D_MODEL), lambda bi, i: (bi, i, 0)),
            pl.BlockSpec((1, D_MODEL), lambda bi, i: (0, 0)),
            pl.BlockSpec((IN_WIDTH, D_MODEL), lambda bi, i: (0, 0)),
            pl.BlockSpec((HEAD_DIM, 1), lambda bi, i: (0, 0)),
            pl.BlockSpec((HEAD_DIM, 1), lambda bi, i: (0, 0)),
            pl.BlockSpec((HALF_DIM, TOKEN_TILE), lambda bi, i: (0, i)),
            pl.BlockSpec((HALF_DIM, TOKEN_TILE), lambda bi, i: (0, i)),
        ],
        out_specs=[
            pl.BlockSpec((1, GROUP_WIDTH, TOKEN_TILE), lambda bi, i: (bi, 0, i)),
            pl.BlockSpec((1, GROUP_WIDTH, TOKEN_TILE), lambda bi, i: (bi, 0, i)),
            pl.BlockSpec((1, 2 * KV_HEADS, TOKEN_TILE, HEAD_DIM), lambda bi, i: (bi, 0, i, 0)),
            pl.BlockSpec((1, 1, 2 * KV_WIDTH, V_CHUNK), lambda bi, i: (bi, i, 0, 0)),
            pl.BlockSpec((1, Q_HEADS, 1, TOKEN_TILE), lambda bi, i: (bi, 0, 0, i)),
            pl.BlockSpec((1, KV_HEADS, 1, TOKEN_TILE), lambda bi, i: (bi, 0, 0, i)),
        ],
        out_shape=[
            jax.ShapeDtypeStruct((b, GROUP_WIDTH, s), jnp.bfloat16),
            jax.ShapeDtypeStruct((b, GROUP_WIDTH, s), jnp.bfloat16),
            jax.ShapeDtypeStruct((b, 2 * KV_HEADS, s, HEAD_DIM), jnp.bfloat16),
            jax.ShapeDtypeStruct((b, n_tiles, 2 * KV_WIDTH, V_CHUNK), jnp.bfloat16),
            jax.ShapeDtypeStruct((b, Q_HEADS, 1, s), jnp.float32),
            jax.ShapeDtypeStruct((b, KV_HEADS, 1, s), jnp.float32),
        ],
        compiler_params=pltpu.CompilerParams(
            dimension_semantics=("arbitrary", "arbitrary"), vmem_limit_bytes=VMEM_LIMIT),
    )(x, pre, w_in_t, gq, gk, cos_t, sin_t)


def _global_kernel(q_ref, k_ref, v_ref, qn_ref, kn_ref, o_ref, s_ref, smax_ref, p_ref, alpha_ref, m_ref,
                   c_ref, acc_ref, *, n_chunks):
    ones = jnp.ones((ONES_ROWS, V_CHUNK), jnp.bfloat16)
    n_blocks = V_CHUNK // ROW_BLOCK
    acc_ref[...] = jnp.zeros_like(acc_ref)

    kmax2 = jnp.max(kn_ref[0, 0], axis=1, keepdims=True)
    c_ref[...] = BOUND_SLACK * jnp.sqrt(qn_ref[0] * kmax2)
    use_fixed_shift = jnp.max(c_ref[...]) <= MAX_FIXED_SHIFT

    def k_rows(chunk, r):
        start = chunk * V_CHUNK + r * ROW_BLOCK
        if not isinstance(start, int):
            start = pl.multiple_of(start, ROW_BLOCK)
        return k_ref[0, 0, pl.ds(start, ROW_BLOCK), :]

    def v_ext(chunk):
        return jnp.concatenate([v_ref[0, chunk], ones], axis=0)

    def values_block(vext, slot, h, r):
        cols = slice((r - 1) * ROW_BLOCK, (r + 1) * ROW_BLOCK)
        return jnp.dot(vext[:, cols], p_ref[slot, h, cols, :], preferred_element_type=jnp.float32)

    def fixed_step(j, parity, do_scores=True, do_values=True):
        cur, other = parity, 1 - parity
        if do_values:
            vext = v_ext(j)
        after_values = None
        for h in range(HEADS_PER_KV):
            q_h = q_ref[0, h * HEAD_DIM:(h + 1) * HEAD_DIM, :]
            c_h = c_ref[h]
            pv = None
            for r in range(n_blocks):
                rows = slice(r * ROW_BLOCK, (r + 1) * ROW_BLOCK)
                if do_scores:
                    st = jnp.dot(k_rows(j + 1, r), q_h, preferred_element_type=jnp.float32)
                    shift = c_h if after_values is None else c_h + after_values
                    p_ref[other, h, rows, :] = jnp.exp2(st - shift).astype(jnp.bfloat16)
                if do_values and r % 2 == 1:
                    d = values_block(vext, cur, h, r)
                    pv = d if pv is None else pv + d
                    after_values = _zero_after(d[HEAD_DIM:HEAD_DIM + 1])
            if do_values:
                acc_ref[h] += pv

    def online_step(j, parity, do_scores=True, do_softmax=True, do_values=True):
        cur, other = parity, 1 - parity
        if do_values:
            vext = v_ext(j - 1)
        for h in range(HEADS_PER_KV):
            q_h = q_ref[0, h * HEAD_DIM:(h + 1) * HEAD_DIM, :]
            if do_softmax:
                m_old = m_ref[h]
                m_new = jnp.maximum(m_old, smax_ref[cur, h])
                alpha_ref[cur, h] = jnp.exp2(m_old - m_new)
                m_ref[h] = m_new
            blk_max = None
            pv = None
            for r in range(n_blocks):
                rows = slice(r * ROW_BLOCK, (r + 1) * ROW_BLOCK)
                if do_scores:
                    st = jnp.dot(k_rows(j + 1, r), q_h, preferred_element_type=jnp.float32)
                    s_ref[other, h, rows, :] = st
                    part = jnp.max(st.reshape(ROW_BLOCK // 8, 8, Q_TILE), axis=0)
                    blk_max = part if blk_max is None else jnp.maximum(blk_max, part)
                if do_softmax:
                    p_ref[cur, h, rows, :] = jnp.exp2(s_ref[cur, h, rows, :] - m_new).astype(
                        jnp.bfloat16)
                if do_values and r % 2 == 1:
                    d = values_block(vext, other, h, r)
                    pv = d if pv is None else pv + d
            if do_scores:
                smax_ref[other, h] = jnp.max(blk_max, axis=0, keepdims=True)
            if do_values:
                acc_ref[h] = alpha_ref[other, h] * acc_ref[h] + pv

    @pl.when(use_fixed_shift)
    def _():
        fixed_step(-1, 1, do_values=False)

        def steps(t, carry):
            for u in range(FIXED_UNROLL):
                fixed_step(FIXED_UNROLL * t + u, u % 2)
            return carry

        n_loop = (n_chunks - 2) // FIXED_UNROLL
        lax.fori_loop(0, n_loop, steps, 0)
        for j in range(n_loop * FIXED_UNROLL, n_chunks - 2):
            fixed_step(j, j % 2)
        fixed_step(n_chunks - 2, 0)
        fixed_step(n_chunks - 1, 1, do_scores=False)

    @pl.when(jnp.logical_not(use_fixed_shift))
    def _():
        m_ref[...] = jnp.full_like(m_ref, NEG_BIG)
        online_step(-1, 1, do_softmax=False, do_values=False)
        online_step(0, 0, do_values=False)

        def pair(t, carry):
            online_step(2 * t + 1, 1)
            online_step(2 * t + 2, 0)
            return carry

        lax.fori_loop(0, (n_chunks - 2) // 2, pair, 0)
        online_step(n_chunks - 1, 1, do_scores=False)
        online_step(n_chunks, 0, do_scores=False, do_softmax=False)

    for h in range(HEADS_PER_KV):
        a = acc_ref[h]
        o_ref[0, h * HEAD_DIM:(h + 1) * HEAD_DIM, :] = a[:HEAD_DIM] / a[HEAD_DIM:HEAD_DIM + 1]


def _global_attention(q_t, k, v_t, q_norm2, k_norm2):
    b, _, s = q_t.shape
    n_chunks = s // V_CHUNK
    assert n_chunks >= 2 and n_chunks % 2 == 0, "the kv pipeline runs chunks in pairs"
    return pl.pallas_call(
        functools.partial(_global_kernel, n_chunks=n_chunks),
        name="global_attn",
        grid=(b, KV_HEADS, s // Q_TILE),
        in_specs=[
            pl.BlockSpec((1, Q_PER_KV_ROWS, Q_TILE), lambda bi, g, i: (bi, g, i)),
            pl.BlockSpec((1, 1, s, HEAD_DIM), lambda bi, g, i: (bi, g, 0, 0)),
            pl.BlockSpec((1, n_chunks, HEAD_DIM, V_CHUNK), lambda bi, g, i: (bi, 0, g, 0)),
            pl.BlockSpec((1, HEADS_PER_KV, 1, Q_TILE), lambda bi, g, i: (bi, g, 0, i)),
            pl.BlockSpec((1, 1, 1, s), lambda bi, g, i: (bi, g, 0, 0)),
        ],
        out_specs=pl.BlockSpec((1, Q_PER_KV_ROWS, Q_TILE), lambda bi, g, i: (bi, g, i)),
        out_shape=jax.ShapeDtypeStruct((b, GROUP_WIDTH, s), jnp.float32),
        scratch_shapes=[
            pltpu.VMEM((2, HEADS_PER_KV, V_CHUNK, Q_TILE), jnp.float32),
            pltpu.VMEM((2, HEADS_PER_KV, 1, Q_TILE), jnp.float32),
            pltpu.VMEM((2, HEADS_PER_KV, V_CHUNK, Q_TILE), jnp.bfloat16),
            pltpu.VMEM((2, HEADS_PER_KV, 1, Q_TILE), jnp.float32),
            pltpu.VMEM((HEADS_PER_KV, 1, Q_TILE), jnp.float32),
            pltpu.VMEM((HEADS_PER_KV, 1, Q_TILE), jnp.float32),
            pltpu.VMEM((HEADS_PER_KV, HEAD_DIM + ONES_ROWS, Q_TILE), jnp.float32),
        ],
        compiler_params=pltpu.CompilerParams(
            dimension_semantics=("arbitrary", "arbitrary", "arbitrary"),
            vmem_limit_bytes=VMEM_LIMIT),
    )(q_t, k, v_t, q_norm2, k_norm2)


def _window_kernel(q_ref, kp_ref, kc_ref, kn_ref, vp_ref, vc_ref, vn_ref, bias_ref, sink_ref,
                   o_ref, s_ref, mx_ref, p_ref):
    i = pl.program_id(2)
    has_prev = i > 0
    has_next = i < pl.num_programs(2) - 1
    sink = sink_ref[0]
    ones = jnp.ones((ONES_ROWS, BAND), jnp.bfloat16)
    no_prev = jnp.where(has_prev, 0.0, NEG_BIG)
    no_next = jnp.where(has_next, 0.0, NEG_BIG)

    def band(sb, prev_ref, cur_ref, next_ref, axis):
        def cur(lo, hi):
            idx = (0, 0, slice(lo, hi), slice(None)) if axis == 0 else (0, 0, slice(None), slice(lo, hi))
            return cur_ref[idx]
        lo = (sb - 1) * WINDOW
        if sb == 0:
            parts = [prev_ref[0, 0], cur(0, 2 * WINDOW)]
        elif sb == N_SUB - 1:
            parts = [cur(lo, WQ_TILE), next_ref[0, 0]]
        else:
            return cur(lo, lo + BAND)
        return jnp.concatenate(parts, axis=axis)

    def scores(sb):
        lo = sb * WINDOW
        qs = jnp.concatenate(
            [q_ref[0, h * HEAD_DIM:(h + 1) * HEAD_DIM, lo:lo + WINDOW] for h in range(HEADS_PER_KV)],
            axis=1)
        ks = band(sb, kp_ref, kc_ref, kn_ref, 0)
        st = jnp.dot(ks, qs, preferred_element_type=jnp.float32) + bias_ref[0]
        if sb == 0:
            st = jnp.concatenate([st[:WINDOW] + no_prev, st[WINDOW:]], axis=0)
        if sb == N_SUB - 1:
            st = jnp.concatenate([st[:2 * WINDOW], st[2 * WINDOW:] + no_next], axis=0)
        s_ref[sb] = st
        mx_ref[sb] = jnp.maximum(jnp.max(st, axis=0, keepdims=True), sink)

    def probabilities(sb):
        p_ref[sb] = jnp.exp2(s_ref[sb] - mx_ref[sb]).astype(jnp.bfloat16)

    def values(sb):
        lo = sb * WINDOW
        vext = jnp.concatenate([band(sb, vp_ref, vc_ref, vn_ref, 1), ones], axis=0)
        ot = jnp.dot(vext, p_ref[sb], preferred_element_type=jnp.float32)
        denom = ot[HEAD_DIM:HEAD_DIM + 1] + jnp.exp2(sink - mx_ref[sb])
        ot = ot[:HEAD_DIM] / denom
        for h in range(HEADS_PER_KV):
            o_ref[0, h * HEAD_DIM:(h + 1) * HEAD_DIM, lo:lo + WINDOW] = (
                ot[:, h * WINDOW:(h + 1) * WINDOW])

    for t in range(N_SUB + 2):
        if t < N_SUB:
            scores(t)
        if 1 <= t <= N_SUB:
            probabilities(t - 1)
        if t >= 2:
            values(t - 2)


def _window_attention(q_t, k, v_t, bias, sink):
    b, _, s = q_t.shape
    n_blk = s // WINDOW
    per = WQ_TILE // WINDOW
    vper = V_CHUNK // WINDOW

    def prev_blk(i):
        return jnp.maximum(i * per - 1, 0)

    def next_blk(i):
        return jnp.minimum(i * per + per, n_blk - 1)

    kh = lambda g: KV_HEADS + g
    return pl.pallas_call(
        _window_kernel,
        name="window_attn",
        grid=(b, KV_HEADS, s // WQ_TILE),
        in_specs=[
            pl.BlockSpec((1, Q_PER_KV_ROWS, WQ_TILE), lambda bi, g, i: (bi, g, i)),
            pl.BlockSpec((1, 1, WINDOW, HEAD_DIM), lambda bi, g, i: (bi, kh(g), prev_blk(i), 0)),
            pl.BlockSpec((1, 1, WQ_TILE, HEAD_DIM), lambda bi, g, i: (bi, kh(g), i, 0)),
            pl.BlockSpec((1, 1, WINDOW, HEAD_DIM), lambda bi, g, i: (bi, kh(g), next_blk(i), 0)),
            pl.BlockSpec((1, 1, HEAD_DIM, WINDOW),
                         lambda bi, g, i: (bi, prev_blk(i) // vper, kh(g), prev_blk(i) % vper)),
            pl.BlockSpec((1, 1, HEAD_DIM, WQ_TILE), lambda bi, g, i: (bi, i, kh(g), 0)),
            pl.BlockSpec((1, 1, HEAD_DIM, WINDOW),
                         lambda bi, g, i: (bi, next_blk(i) // vper, kh(g), next_blk(i) % vper)),
            pl.BlockSpec((1, BAND, HEADS_PER_KV * WINDOW), lambda bi, g, i: (g, 0, 0)),
            pl.BlockSpec((1, 1, HEADS_PER_KV * WINDOW), lambda bi, g, i: (g, 0, 0)),
        ],
        out_specs=pl.BlockSpec((1, Q_PER_KV_ROWS, WQ_TILE), lambda bi, g, i: (bi, g, i)),
        out_shape=jax.ShapeDtypeStruct((b, GROUP_WIDTH, s), jnp.float32),
        scratch_shapes=[
            pltpu.VMEM((N_SUB, BAND, HEADS_PER_KV * WINDOW), jnp.float32),
            pltpu.VMEM((N_SUB, 1, HEADS_PER_KV * WINDOW), jnp.float32),
            pltpu.VMEM((N_SUB, BAND, HEADS_PER_KV * WINDOW), jnp.bfloat16),
        ],
        compiler_params=pltpu.CompilerParams(
            dimension_semantics=("arbitrary", "arbitrary", "arbitrary"),
            vmem_limit_bytes=VMEM_LIMIT),
    )(q_t, k, k, k, v_t, v_t, v_t, bias, sink)


def _mix_out_kernel(x_ref, oa_ref, ob_ref, ga_ref, gb_ref, w_ref, post_ref, o_ref):
    na = _rms_cols(oa_ref[0], ga_ref[...])
    nb = _rms_cols(ob_ref[0], gb_ref[...])
    mt = jnp.concatenate([na, nb], axis=0).astype(jnp.bfloat16)
    h = lax.dot_general(mt, w_ref[...], (((0,), (0,)), ((), ())),
                        preferred_element_type=jnp.float32)
    o_ref[0] = x_ref[0] + _rms_rows(h, post_ref[...])


def _mix_out(x, oa_t, ob_t, ga, gb, w_out, post):
    b, s, _ = x.shape
    return pl.pallas_call(
        _mix_out_kernel,
        name="mix_out",
        grid=(b, s // TOKEN_TILE),
        in_specs=[
            pl.BlockSpec((1, TOKEN_TILE, D_MODEL), lambda bi, i: (bi, i, 0)),
            pl.BlockSpec((1, GROUP_WIDTH, TOKEN_TILE), lambda bi, i: (bi, 0, i)),
            pl.BlockSpec((1, GROUP_WIDTH, TOKEN_TILE), lambda bi, i: (bi, 0, i)),
            pl.BlockSpec((GROUP_WIDTH, 1), lambda bi, i: (0, 0)),
            pl.BlockSpec((GROUP_WIDTH, 1), lambda bi, i: (0, 0)),
            pl.BlockSpec((D_MODEL, D_MODEL), lambda bi, i: (0, 0)),
            pl.BlockSpec((1, D_MODEL), lambda bi, i: (0, 0)),
        ],
        out_specs=pl.BlockSpec((1, TOKEN_TILE, D_MODEL), lambda bi, i: (bi, i, 0)),
        out_shape=jax.ShapeDtypeStruct((b, s, D_MODEL), jnp.float32),
        compiler_params=pltpu.CompilerParams(
            dimension_semantics=("arbitrary", "arbitrary"), vmem_limit_bytes=VMEM_LIMIT),
    )(x, oa_t, ob_t, ga, gb, w_out, post)


_DEINTERLEAVE = np.concatenate([np.arange(0, HEAD_DIM, 2), np.arange(1, HEAD_DIM, 2)])


def _w_in_rows():
    n_rot = (Q_HEADS + KV_HEADS) * HEAD_DIM
    rot = (np.arange(Q_HEADS + KV_HEADS)[:, None] * HEAD_DIM + _DEINTERLEAVE[None, :]).reshape(-1)
    return np.concatenate([rot, np.arange(n_rot, IN_WIDTH)])


def _rope_tables_t(seq_len):
    n_rows = seq_len // GRID_W
    row = jnp.repeat(jnp.arange(n_rows, dtype=jnp.float32), GRID_W)
    col = jnp.tile(jnp.arange(GRID_W, dtype=jnp.float32), n_rows)
    n_freq = HEAD_DIM // 4
    inv_freq = ROPE_THETA ** (-jnp.arange(n_freq, dtype=jnp.float32) / n_freq)
    ang = jnp.concatenate([inv_freq[:, None] * row[None, :], inv_freq[:, None] * col[None, :]], axis=0)
    return jnp.cos(ang), jnp.sin(ang)


def _window_bias():
    slopes = 2.0 ** (-8.0 * np.arange(1, Q_HEADS + 1, dtype=np.float64) / Q_HEADS)
    kpos = np.arange(BAND)[:, None] - WINDOW
    qpos = np.arange(WINDOW)[None, :]
    dist = np.abs(qpos - kpos).astype(np.float64)
    per_head = np.where(dist[None] <= WINDOW, -slopes[:, None, None] * dist[None] * LOG2E, NEG_BIG)
    per_kv = per_head.reshape(KV_HEADS, HEADS_PER_KV, BAND, WINDOW).transpose(0, 2, 1, 3)
    return jnp.asarray(per_kv.reshape(KV_HEADS, BAND, HEADS_PER_KV * WINDOW), jnp.float32)


def _ffn_weights(w_gate, w_up, w_down):
    wg = w_gate.reshape(D_MODEL, N_FF_CHUNKS, FF_CHUNK)
    wu = w_up.reshape(D_MODEL, N_FF_CHUNKS, FF_CHUNK)
    wgu = jnp.concatenate([wg, wu], axis=-1).transpose(1, 0, 2).astype(jnp.bfloat16)
    wd = w_down.reshape(N_FF_CHUNKS, FF_CHUNK, D_MODEL).astype(jnp.bfloat16)
    return wgu, wd


def _trunk(x, layers, bias):
    b, s, _ = x.shape
    cos_t, sin_t = _rope_tables_t(s)
    for p in layers:
        x = _ffn(x.reshape(b * s, D_MODEL), p["ffn1_pre"], p["ffn1_post"], p["ffn1_wgu"],
                 p["ffn1_wd"]).reshape(b, s, D_MODEL)
        qa_t, qb_t, k, v_t, q_norm2, k_norm2 = _mix_in(x, p["mix_pre"], p["w_in_t"], p["gq"],
                                                       p["gk"], cos_t, sin_t)
        oa_t = _global_attention(qa_t, k, v_t, q_norm2, k_norm2)
        ob_t = _window_attention(qb_t, k, v_t, bias, p["sink"])
        x = _mix_out(x, oa_t, ob_t, p["ga"], p["gb"], p["w_out"], p["mix_post"])
        x = _ffn(x.reshape(b * s, D_MODEL), p["ffn2_pre"], p["ffn2_post"], p["ffn2_wgu"],
                 p["ffn2_wd"]).reshape(b, s, D_MODEL)
    return x


def kernel(x_prompt, x_sample, ffn1_pre, ffn1_post, ffn1_w_gate, ffn1_w_up, ffn1_w_down, mix_pre, mix_post, w_in, a_q_norm, a_k_norm, b_sink, a_out_norm, b_out_norm, w_out, ffn2_pre, ffn2_post, ffn2_w_gate, ffn2_w_up, ffn2_w_down):
    depth = w_in.shape[0]
    rows = _w_in_rows()
    layers = []
    for l in range(depth):
        wgu1, wd1 = _ffn_weights(ffn1_w_gate[l], ffn1_w_up[l], ffn1_w_down[l])
        wgu2, wd2 = _ffn_weights(ffn2_w_gate[l], ffn2_w_up[l], ffn2_w_down[l])
        sink = jnp.repeat(b_sink[l].astype(jnp.float32) * LOG2E, WINDOW)
        layers.append(dict(
            ffn1_pre=ffn1_pre[l][None, :], ffn1_post=ffn1_post[l][None, :], ffn1_wgu=wgu1, ffn1_wd=wd1,
            ffn2_pre=ffn2_pre[l][None, :], ffn2_post=ffn2_post[l][None, :], ffn2_wgu=wgu2, ffn2_wd=wd2,
            mix_pre=mix_pre[l][None, :], mix_post=mix_post[l][None, :],
            w_in_t=w_in[l].T[rows].astype(jnp.bfloat16),
            gq=a_q_norm[l][_DEINTERLEAVE][:, None], gk=a_k_norm[l][_DEINTERLEAVE][:, None],
            sink=sink.reshape(KV_HEADS, 1, HEADS_PER_KV * WINDOW),
            ga=a_out_norm[l][:, None], gb=b_out_norm[l][:, None],
            w_out=w_out[l].astype(jnp.bfloat16),
        ))
    bias = _window_bias()
    return (_trunk(x_prompt, layers, bias), _trunk(x_sample, layers, bias))
```

```python
import functools
import math

import jax
import jax.numpy as jnp
import numpy as np
from jax import lax
from jax.experimental import pallas as pl
from jax.experimental.pallas import tpu as pltpu

D_MODEL = 1024
HEAD_DIM = 64
HALF_DIM = HEAD_DIM // 2
Q_HEADS = 8
KV_HEADS = 2
HEADS_PER_KV = Q_HEADS // KV_HEADS
GROUP_WIDTH = Q_HEADS * HEAD_DIM
KV_WIDTH = KV_HEADS * HEAD_DIM
Q_PER_KV_ROWS = HEADS_PER_KV * HEAD_DIM
GROUP_IN = GROUP_WIDTH + 2 * KV_WIDTH
IN_WIDTH = 2 * GROUP_IN
D_FF = 2816
GRID_W = 64
ROPE_THETA = 10000.0
WINDOW = 128
NORM_EPS = 1e-6
FFN_RESID = 0.5
LOG2E = math.log2(math.e)
Q_SCALE = HEAD_DIM ** -0.5 * LOG2E
NEG_BIG = -1e30
BOUND_SLACK = 1.01
MAX_FIXED_SHIFT = 48.0

FF_CHUNK = 256
N_FF_CHUNKS = D_FF // FF_CHUNK
TOKEN_TILE = 512
V_CHUNK = TOKEN_TILE
Q_TILE = 512
ONES_ROWS = 16
ROW_BLOCK = 128
FIXED_UNROLL = 10
WQ_TILE = 512
N_SUB = WQ_TILE // WINDOW
BAND = 3 * WINDOW

VMEM_LIMIT = 56 * 1024 * 1024
RESIDENT = pl.Buffered(1)


def _rms_rows(x, gain_row):
    ms = jnp.mean(x * x, axis=-1, keepdims=True)
    return x * lax.rsqrt(ms + NORM_EPS) * gain_row


def _rms_cols(xt, gain_col):
    ms = jnp.mean(xt * xt, axis=0, keepdims=True)
    return xt * lax.rsqrt(ms + NORM_EPS) * gain_col


def _zero_after(x):
    bits = lax.bitcast_convert_type(x, jnp.uint32)
    bits = lax.shift_right_logical(lax.shift_right_logical(bits, jnp.uint32(16)), jnp.uint32(16))
    return bits.astype(jnp.float32)


def _ffn_branch(x, pre_ref, post_ref, wgu_ref, wd_ref, xn_ref, acc_ref):
    xn_ref[...] = _rms_rows(x, pre_ref[...]).astype(jnp.bfloat16)
    acc_ref[...] = jnp.zeros_like(acc_ref)

    for c in range(N_FF_CHUNKS):
        gu = jnp.dot(xn_ref[...], wgu_ref[c], preferred_element_type=jnp.float32)
        g = gu[:, :FF_CHUNK]
        u = gu[:, FF_CHUNK:]
        h = (g * jax.nn.sigmoid(g) * u).astype(jnp.bfloat16)
        acc_ref[...] += jnp.dot(h, wd_ref[c], preferred_element_type=jnp.float32)
    return FFN_RESID * _rms_rows(acc_ref[...], post_ref[...])


def _rope_norm_head(xt, gain_col, cos_t, sin_t):
    y = _rms_cols(xt, gain_col)
    x0 = y[:HALF_DIM]
    x1 = y[HALF_DIM:]
    return jnp.concatenate([x0 * cos_t - x1 * sin_t, x0 * sin_t + x1 * cos_t], axis=0)


def _ffn_mix_in_kernel(x_ref, pre1_ref, post1_ref, wgu_ref, wd_ref,
                       pre_ref, w_ref, gq_ref, gk_ref, cos_ref, sin_ref,
                       x1_ref, qa_ref, qb_ref, k_ref, v_ref, qn_ref, kn_ref, xn_ref, acc_ref):
    x = x_ref[0]
    x1 = x + _ffn_branch(x, pre1_ref, post1_ref, wgu_ref, wd_ref, xn_ref, acc_ref)
    x1_ref[0] = x1
    xn = _rms_rows(x1, pre_ref[...]).astype(jnp.bfloat16)
    pt = lax.dot_general(w_ref[...], xn, (((1,), (1,)), ((), ())),
                         preferred_element_type=jnp.float32)
    pa = pt[:GROUP_IN]
    pb = pt[GROUP_IN:]
    cos_t = cos_ref[...]
    sin_t = sin_ref[...]
    gq = gq_ref[...]
    gk = gk_ref[...]
    off = 0
    for h in range(Q_HEADS):
        r = _rope_norm_head(pa[off:off + HEAD_DIM], gq, cos_t, sin_t) * Q_SCALE
        qa_ref[0, h * HEAD_DIM:(h + 1) * HEAD_DIM, :] = r.astype(jnp.bfloat16)
        qn_ref[0, h] = jnp.sum(r * r, axis=0, keepdims=True)
        off += HEAD_DIM
    for h in range(KV_HEADS):
        r = _rope_norm_head(pa[off:off + HEAD_DIM], gk, cos_t, sin_t)
        k_ref[0, h] = r.T.astype(jnp.bfloat16)
        kn_ref[0, h] = jnp.sum(r * r, axis=0, keepdims=True)
        off += HEAD_DIM
    v_ref[0, 0, :KV_WIDTH, :] = pa[off:off + KV_WIDTH].astype(jnp.bfloat16)
    qb_ref[0] = (pb[:GROUP_WIDTH] * Q_SCALE).astype(jnp.bfloat16)
    off = GROUP_WIDTH
    for h in range(KV_HEADS):
        k_ref[0, KV_HEADS + h] = pb[off:off + HEAD_DIM].T.astype(jnp.bfloat16)
        off += HEAD_DIM
    v_ref[0, 0, KV_WIDTH:, :] = pb[off:off + KV_WIDTH].astype(jnp.bfloat16)


def _ffn_mix_in(x, pre1, post1, wgu, wd, pre, w_in_t, gq, gk, cos_t, sin_t):
    b, s, _ = x.shape
    n_tiles = s // TOKEN_TILE
    vec = lambda bi, i: (0, 0)
    const3 = lambda bi, i: (0, 0, 0)
    return pl.pallas_call(
        _ffn_mix_in_kernel,
        name="ffn_mix_in",
        grid=(b, n_tiles),
        in_specs=[
            pl.BlockSpec((1, TOKEN_TILE, D_MODEL), lambda bi, i: (bi, i, 0)),
            pl.BlockSpec((1, D_MODEL), vec),
            pl.BlockSpec((1, D_MODEL), vec),
            pl.BlockSpec((N_FF_CHUNKS, D_MODEL, 2 * FF_CHUNK), const3, pipeline_mode=RESIDENT),
            pl.BlockSpec((N_FF_CHUNKS, FF_CHUNK, D_MODEL), const3, pipeline_mode=RESIDENT),
            pl.BlockSpec((1, D_MODEL), vec),
            pl.BlockSpec((IN_WIDTH, D_MODEL), vec, pipeline_mode=RESIDENT),
            pl.BlockSpec((HEAD_DIM, 1), vec),
            pl.BlockSpec((HEAD_DIM, 1), vec),
            pl.BlockSpec((HALF_DIM, TOKEN_TILE), lambda bi, i: (0, i)),
            pl.BlockSpec((HALF_DIM, TOKEN_TILE), lambda bi, i: (0, i)),
        ],
        out_specs=[
            pl.BlockSpec((1, TOKEN_TILE, D_MODEL), lambda bi, i: (bi, i, 0)),
            pl.BlockSpec((1, GROUP_WIDTH, TOKEN_TILE), lambda bi, i: (bi, 0, i)),
            pl.BlockSpec((1, GROUP_WIDTH, TOKEN_TILE), lambda bi, i: (bi, 0, i)),
            pl.BlockSpec((1, 2 * KV_HEADS, TOKEN_TILE, HEAD_DIM), lambda bi, i: (bi, 0, i, 0)),
            pl.BlockSpec((1, 1, 2 * KV_WIDTH, V_CHUNK), lambda bi, i: (bi, i, 0, 0)),
            pl.BlockSpec((1, Q_HEADS, 1, TOKEN_TILE), lambda bi, i: (bi, 0, 0, i)),
            pl.BlockSpec((1, KV_HEADS, 1, TOKEN_TILE), lambda bi, i: (bi, 0, 0, i)),
        ],
        out_shape=[
            jax.ShapeDtypeStruct((b, s, D_MODEL), jnp.float32),
            jax.ShapeDtypeStruct((b, GROUP_WIDTH, s), jnp.bfloat16),
            jax.ShapeDtypeStruct((b, GROUP_WIDTH, s), jnp.bfloat16),
            jax.ShapeDtypeStruct((b, 2 * KV_HEADS, s, HEAD_DIM), jnp.bfloat16),
            jax.ShapeDtypeStruct((b, n_tiles, 2 * KV_WIDTH, V_CHUNK), jnp.bfloat16),
            jax.ShapeDtypeStruct((b, Q_HEADS, 1, s), jnp.float32),
            jax.ShapeDtypeStruct((b, KV_HEADS, 1, s), jnp.float32),
        ],
        scratch_shapes=[
            pltpu.VMEM((TOKEN_TILE, D_MODEL), jnp.bfloat16),
            pltpu.VMEM((TOKEN_TILE, D_MODEL), jnp.float32),
        ],
        compiler_params=pltpu.CompilerParams(
            dimension_semantics=("arbitrary", "arbitrary"), vmem_limit_bytes=VMEM_LIMIT),
    )(x, pre1, post1, wgu, wd, pre, w_in_t, gq, gk, cos_t, sin_t)


def _global_kernel(q_ref, k_ref, v_ref, qn_ref, kn_ref, o_ref, s_ref, smax_ref, p_ref, alpha_ref, m_ref,
                   c_ref, acc_ref, *, n_chunks):
    ones = jnp.ones((ONES_ROWS, V_CHUNK), jnp.bfloat16)
    n_blocks = V_CHUNK // ROW_BLOCK
    acc_ref[...] = jnp.zeros_like(acc_ref)

    kmax2 = jnp.max(kn_ref[0, 0], axis=1, keepdims=True)
    c_ref[...] = BOUND_SLACK * jnp.sqrt(qn_ref[0] * kmax2)
    use_fixed_shift = jnp.max(c_ref[...]) <= MAX_FIXED_SHIFT

    def k_rows(chunk, r):
        start = chunk * V_CHUNK + r * ROW_BLOCK
        if not isinstance(start, int):
            start = pl.multiple_of(start, ROW_BLOCK)
        return k_ref[0, 0, pl.ds(start, ROW_BLOCK), :]

    def v_ext(chunk):
        return jnp.concatenate([v_ref[0, chunk], ones], axis=0)

    def values_block(vext, slot, h, r):
        cols = slice((r - 1) * ROW_BLOCK, (r + 1) * ROW_BLOCK)
        return jnp.dot(vext[:, cols], p_ref[slot, h, cols, :], preferred_element_type=jnp.float32)

    def fixed_step(j, parity, do_scores=True, do_values=True):
        cur, other = parity, 1 - parity
        if do_values:
            vext = v_ext(j)
        after_values = None
        for h in range(HEADS_PER_KV):
            q_h = q_ref[0, h * HEAD_DIM:(h + 1) * HEAD_DIM, :]
            c_h = c_ref[h]
            pv = None
            for r in range(n_blocks):
                rows = slice(r * ROW_BLOCK, (r + 1) * ROW_BLOCK)
                if do_scores:
                    st = jnp.dot(k_rows(j + 1, r), q_h, preferred_element_type=jnp.float32)
                    shift = c_h if after_values is None else c_h + after_values
                    p_ref[other, h, rows, :] = jnp.exp2(st - shift).astype(jnp.bfloat16)
                if do_values and r % 2 == 1:
                    d = values_block(vext, cur, h, r)
                    pv = d if pv is None else pv + d
                    after_values = _zero_after(d[HEAD_DIM:HEAD_DIM + 1])
            if do_values:
                acc_ref[h] += pv

    def online_step(j, parity, do_scores=True, do_softmax=True, do_values=True):
        cur, other = parity, 1 - parity
        if do_values:
            vext = v_ext(j - 1)
        for h in range(HEADS_PER_KV):
            q_h = q_ref[0, h * HEAD_DIM:(h + 1) * HEAD_DIM, :]
            if do_softmax:
                m_old = m_ref[h]
                m_new = jnp.maximum(m_old, smax_ref[cur, h])
                alpha_ref[cur, h] = jnp.exp2(m_old - m_new)
                m_ref[h] = m_new
            blk_max = None
            pv = None
            for r in range(n_blocks):
                rows = slice(r * ROW_BLOCK, (r + 1) * ROW_BLOCK)
                if do_scores:
                    st = jnp.dot(k_rows(j + 1, r), q_h, preferred_element_type=jnp.float32)
                    s_ref[other, h, rows, :] = st
                    part = jnp.max(st.reshape(ROW_BLOCK // 8, 8, Q_TILE), axis=0)
                    blk_max = part if blk_max is None else jnp.maximum(blk_max, part)
                if do_softmax:
                    p_ref[cur, h, rows, :] = jnp.exp2(s_ref[cur, h, rows, :] - m_new).astype(
                        jnp.bfloat16)
                if do_values and r % 2 == 1:
                    d = values_block(vext, other, h, r)
                    pv = d if pv is None else pv + d
            if do_scores:
                smax_ref[other, h] = jnp.max(blk_max, axis=0, keepdims=True)
            if do_values:
                acc_ref[h] = alpha_ref[other, h] * acc_ref[h] + pv

    @pl.when(use_fixed_shift)
    def _():
        fixed_step(-1, 1, do_values=False)

        def steps(t, carry):
            for u in range(FIXED_UNROLL):
                fixed_step(FIXED_UNROLL * t + u, u % 2)
            return carry

        n_loop = (n_chunks - 2) // FIXED_UNROLL
        lax.fori_loop(0, n_loop, steps, 0)
        for j in range(n_loop * FIXED_UNROLL, n_chunks - 2):
            fixed_step(j, j % 2)
        fixed_step(n_chunks - 2, 0)
        fixed_step(n_chunks - 1, 1, do_scores=False)

    @pl.when(jnp.logical_not(use_fixed_shift))
    def _():
        m_ref[...] = jnp.full_like(m_ref, NEG_BIG)
        online_step(-1, 1, do_softmax=False, do_values=False)
        online_step(0, 0, do_values=False)

        def pair(t, carry):
            online_step(2 * t + 1, 1)
            online_step(2 * t + 2, 0)
            return carry

        lax.fori_loop(0, (n_chunks - 2) // 2, pair, 0)
        online_step(n_chunks - 1, 1, do_scores=False)
        online_step(n_chunks, 0, do_scores=False, do_softmax=False)

    for h in range(HEADS_PER_KV):
        a = acc_ref[h]
        o_ref[0, h * HEAD_DIM:(h + 1) * HEAD_DIM, :] = a[:HEAD_DIM] / a[HEAD_DIM:HEAD_DIM + 1]


def _global_attention(q_t, k, v_t, q_norm2, k_norm2):
    b, _, s = q_t.shape
    n_chunks = s // V_CHUNK
    assert n_chunks >= 2 and n_chunks % 2 == 0, "the kv pipeline runs chunks in pairs"
    return pl.pallas_call(
        functools.partial(_global_kernel, n_chunks=n_chunks),
        name="global_attn",
        grid=(b, KV_HEADS, s // Q_TILE),
        in_specs=[
            pl.BlockSpec((1, Q_PER_KV_ROWS, Q_TILE), lambda bi, g, i: (bi, g, i)),
            pl.BlockSpec((1, 1, s, HEAD_DIM), lambda bi, g, i: (bi, g, 0, 0)),
            pl.BlockSpec((1, n_chunks, HEAD_DIM, V_CHUNK), lambda bi, g, i: (bi, 0, g, 0)),
            pl.BlockSpec((1, HEADS_PER_KV, 1, Q_TILE), lambda bi, g, i: (bi, g, 0, i)),
            pl.BlockSpec((1, 1, 1, s), lambda bi, g, i: (bi, g, 0, 0)),
        ],
        out_specs=pl.BlockSpec((1, Q_PER_KV_ROWS, Q_TILE), lambda bi, g, i: (bi, g, i)),
        out_shape=jax.ShapeDtypeStruct((b, GROUP_WIDTH, s), jnp.float32),
        scratch_shapes=[
            pltpu.VMEM((2, HEADS_PER_KV, V_CHUNK, Q_TILE), jnp.float32),
            pltpu.VMEM((2, HEADS_PER_KV, 1, Q_TILE), jnp.float32),
            pltpu.VMEM((2, HEADS_PER_KV, V_CHUNK, Q_TILE), jnp.bfloat16),
            pltpu.VMEM((2, HEADS_PER_KV, 1, Q_TILE), jnp.float32),
            pltpu.VMEM((HEADS_PER_KV, 1, Q_TILE), jnp.float32),
            pltpu.VMEM((HEADS_PER_KV, 1, Q_TILE), jnp.float32),
            pltpu.VMEM((HEADS_PER_KV, HEAD_DIM + ONES_ROWS, Q_TILE), jnp.float32),
        ],
        compiler_params=pltpu.CompilerParams(
            dimension_semantics=("arbitrary", "arbitrary", "arbitrary"),
            vmem_limit_bytes=VMEM_LIMIT),
    )(q_t, k, v_t, q_norm2, k_norm2)


def _window_kernel(q_ref, kp_ref, kc_ref, kn_ref, vp_ref, vc_ref, vn_ref, bias_ref, sink_ref,
                   o_ref, s_ref, mx_ref, p_ref):
    i = pl.program_id(2)
    has_prev = i > 0
    has_next = i < pl.num_programs(2) - 1
    sink = sink_ref[0]
    ones = jnp.ones((ONES_ROWS, BAND), jnp.bfloat16)
    no_prev = jnp.where(has_prev, 0.0, NEG_BIG)
    no_next = jnp.where(has_next, 0.0, NEG_BIG)

    def band(sb, prev_ref, cur_ref, next_ref, axis):
        def cur(lo, hi):
            idx = (0, 0, slice(lo, hi), slice(None)) if axis == 0 else (0, 0, slice(None), slice(lo, hi))
            return cur_ref[idx]
        lo = (sb - 1) * WINDOW
        if sb == 0:
            parts = [prev_ref[0, 0], cur(0, 2 * WINDOW)]
        elif sb == N_SUB - 1:
            parts = [cur(lo, WQ_TILE), next_ref[0, 0]]
        else:
            return cur(lo, lo + BAND)
        return jnp.concatenate(parts, axis=axis)

    def scores(sb):
        lo = sb * WINDOW
        qs = jnp.concatenate(
            [q_ref[0, h * HEAD_DIM:(h + 1) * HEAD_DIM, lo:lo + WINDOW] for h in range(HEADS_PER_KV)],
            axis=1)
        ks = band(sb, kp_ref, kc_ref, kn_ref, 0)
        st = jnp.dot(ks, qs, preferred_element_type=jnp.float32) + bias_ref[0]
        if sb == 0:
            st = jnp.concatenate([st[:WINDOW] + no_prev, st[WINDOW:]], axis=0)
        if sb == N_SUB - 1:
            st = jnp.concatenate([st[:2 * WINDOW], st[2 * WINDOW:] + no_next], axis=0)
        s_ref[sb] = st
        mx_ref[sb] = jnp.maximum(jnp.max(st, axis=0, keepdims=True), sink)

    def probabilities(sb):
        p_ref[sb] = jnp.exp2(s_ref[sb] - mx_ref[sb]).astype(jnp.bfloat16)

    def values(sb):
        lo = sb * WINDOW
        vext = jnp.concatenate([band(sb, vp_ref, vc_ref, vn_ref, 1), ones], axis=0)
        ot = jnp.dot(vext, p_ref[sb], preferred_element_type=jnp.float32)
        denom = ot[HEAD_DIM:HEAD_DIM + 1] + jnp.exp2(sink - mx_ref[sb])
        ot = ot[:HEAD_DIM] / denom
        for h in range(HEADS_PER_KV):
            o_ref[0, h * HEAD_DIM:(h + 1) * HEAD_DIM, lo:lo + WINDOW] = (
                ot[:, h * WINDOW:(h + 1) * WINDOW])

    for t in range(N_SUB + 2):
        if t < N_SUB:
            scores(t)
        if 1 <= t <= N_SUB:
            probabilities(t - 1)
        if t >= 2:
            values(t - 2)


def _window_attention(q_t, k, v_t, bias, sink):
    b, _, s = q_t.shape
    n_blk = s // WINDOW
    per = WQ_TILE // WINDOW
    vper = V_CHUNK // WINDOW

    def prev_blk(i):
        return jnp.maximum(i * per - 1, 0)

    def next_blk(i):
        return jnp.minimum(i * per + per, n_blk - 1)

    kh = lambda g: KV_HEADS + g
    return pl.pallas_call(
        _window_kernel,
        name="window_attn",
        grid=(b, KV_HEADS, s // WQ_TILE),
        in_specs=[
            pl.BlockSpec((1, Q_PER_KV_ROWS, WQ_TILE), lambda bi, g, i: (bi, g, i)),
            pl.BlockSpec((1, 1, WINDOW, HEAD_DIM), lambda bi, g, i: (bi, kh(g), prev_blk(i), 0)),
            pl.BlockSpec((1, 1, WQ_TILE, HEAD_DIM), lambda bi, g, i: (bi, kh(g), i, 0)),
            pl.BlockSpec((1, 1, WINDOW, HEAD_DIM), lambda bi, g, i: (bi, kh(g), next_blk(i), 0)),
            pl.BlockSpec((1, 1, HEAD_DIM, WINDOW),
                         lambda bi, g, i: (bi, prev_blk(i) // vper, kh(g), prev_blk(i) % vper)),
            pl.BlockSpec((1, 1, HEAD_DIM, WQ_TILE), lambda bi, g, i: (bi, i, kh(g), 0)),
            pl.BlockSpec((1, 1, HEAD_DIM, WINDOW),
                         lambda bi, g, i: (bi, next_blk(i) // vper, kh(g), next_blk(i) % vper)),
            pl.BlockSpec((1, BAND, HEADS_PER_KV * WINDOW), lambda bi, g, i: (g, 0, 0)),
            pl.BlockSpec((1, 1, HEADS_PER_KV * WINDOW), lambda bi, g, i: (g, 0, 0)),
        ],
        out_specs=pl.BlockSpec((1, Q_PER_KV_ROWS, WQ_TILE), lambda bi, g, i: (bi, g, i)),
        out_shape=jax.ShapeDtypeStruct((b, GROUP_WIDTH, s), jnp.float32),
        scratch_shapes=[
            pltpu.VMEM((N_SUB, BAND, HEADS_PER_KV * WINDOW), jnp.float32),
            pltpu.VMEM((N_SUB, 1, HEADS_PER_KV * WINDOW), jnp.float32),
            pltpu.VMEM((N_SUB, BAND, HEADS_PER_KV * WINDOW), jnp.bfloat16),
        ],
        compiler_params=pltpu.CompilerParams(
            dimension_semantics=("arbitrary", "arbitrary", "arbitrary"),
            vmem_limit_bytes=VMEM_LIMIT),
    )(q_t, k, k, k, v_t, v_t, v_t, bias, sink)


def _mix_out_ffn_kernel(x_ref, oa_ref, ob_ref, ga_ref, gb_ref, w_ref, post_ref,
                        pre2_ref, post2_ref, wgu_ref, wd_ref, o_ref, xn_ref, acc_ref):
    na = _rms_cols(oa_ref[0], ga_ref[...])
    nb = _rms_cols(ob_ref[0], gb_ref[...])
    mt = jnp.concatenate([na, nb], axis=0).astype(jnp.bfloat16)
    h = lax.dot_general(mt, w_ref[...], (((0,), (0,)), ((), ())),
                        preferred_element_type=jnp.float32)
    x1 = x_ref[0] + _rms_rows(h, post_ref[...])
    o_ref[0] = x1
    o_ref[0] += _ffn_branch(x1, pre2_ref, post2_ref, wgu_ref, wd_ref, xn_ref, acc_ref)


def _mix_out_ffn(x, oa_t, ob_t, ga, gb, w_out, post, pre2, post2, wgu, wd):
    b, s, _ = x.shape
    vec = lambda bi, i: (0, 0)
    const3 = lambda bi, i: (0, 0, 0)
    return pl.pallas_call(
        _mix_out_ffn_kernel,
        name="mix_out_ffn",
        grid=(b, s // TOKEN_TILE),
        in_specs=[
            pl.BlockSpec((1, TOKEN_TILE, D_MODEL), lambda bi, i: (bi, i, 0)),
            pl.BlockSpec((1, GROUP_WIDTH, TOKEN_TILE), lambda bi, i: (bi, 0, i)),
            pl.BlockSpec((1, GROUP_WIDTH, TOKEN_TILE), lambda bi, i: (bi, 0, i)),
            pl.BlockSpec((GROUP_WIDTH, 1), vec),
            pl.BlockSpec((GROUP_WIDTH, 1), vec),
            pl.BlockSpec((D_MODEL, D_MODEL), vec, pipeline_mode=RESIDENT),
            pl.BlockSpec((1, D_MODEL), vec),
            pl.BlockSpec((1, D_MODEL), vec),
            pl.BlockSpec((1, D_MODEL), vec),
            pl.BlockSpec((N_FF_CHUNKS, D_MODEL, 2 * FF_CHUNK), const3, pipeline_mode=RESIDENT),
            pl.BlockSpec((N_FF_CHUNKS, FF_CHUNK, D_MODEL), const3, pipeline_mode=RESIDENT),
        ],
        out_specs=pl.BlockSpec((1, TOKEN_TILE, D_MODEL), lambda bi, i: (bi, i, 0)),
        out_shape=jax.ShapeDtypeStruct((b, s, D_MODEL), jnp.float32),
        scratch_shapes=[
            pltpu.VMEM((TOKEN_TILE, D_MODEL), jnp.bfloat16),
            pltpu.VMEM((TOKEN_TILE, D_MODEL), jnp.float32),
        ],
        compiler_params=pltpu.CompilerParams(
            dimension_semantics=("arbitrary", "arbitrary"), vmem_limit_bytes=VMEM_LIMIT),
    )(x, oa_t, ob_t, ga, gb, w_out, post, pre2, post2, wgu, wd)


_DEINTERLEAVE = np.concatenate([np.arange(0, HEAD_DIM, 2), np.arange(1, HEAD_DIM, 2)])


def _w_in_rows():
    n_rot = (Q_HEADS + KV_HEADS) * HEAD_DIM
    rot = (np.arange(Q_HEADS + KV_HEADS)[:, None] * HEAD_DIM + _DEINTERLEAVE[None, :]).reshape(-1)
    return np.concatenate([rot, np.arange(n_rot, IN_WIDTH)])


def _rope_tables_t(seq_len):
    n_rows = seq_len // GRID_W
    row = jnp.repeat(jnp.arange(n_rows, dtype=jnp.float32), GRID_W)
    col = jnp.tile(jnp.arange(GRID_W, dtype=jnp.float32), n_rows)
    n_freq = HEAD_DIM // 4
    inv_freq = ROPE_THETA ** (-jnp.arange(n_freq, dtype=jnp.float32) / n_freq)
    ang = jnp.concatenate([inv_freq[:, None] * row[None, :], inv_freq[:, None] * col[None, :]], axis=0)
    return jnp.cos(ang), jnp.sin(ang)


def _window_bias():
    slopes = 2.0 ** (-8.0 * np.arange(1, Q_HEADS + 1, dtype=np.float64) / Q_HEADS)
    kpos = np.arange(BAND)[:, None] - WINDOW
    qpos = np.arange(WINDOW)[None, :]
    dist = np.abs(qpos - kpos).astype(np.float64)
    per_head = np.where(dist[None] <= WINDOW, -slopes[:, None, None] * dist[None] * LOG2E, NEG_BIG)
    per_kv = per_head.reshape(KV_HEADS, HEADS_PER_KV, BAND, WINDOW).transpose(0, 2, 1, 3)
    return jnp.asarray(per_kv.reshape(KV_HEADS, BAND, HEADS_PER_KV * WINDOW), jnp.float32)


def _ffn_weights(w_gate, w_up, w_down):
    wg = w_gate.reshape(D_MODEL, N_FF_CHUNKS, FF_CHUNK)
    wu = w_up.reshape(D_MODEL, N_FF_CHUNKS, FF_CHUNK)
    wgu = jnp.concatenate([wg, wu], axis=-1).transpose(1, 0, 2).astype(jnp.bfloat16)
    wd = w_down.reshape(N_FF_CHUNKS, FF_CHUNK, D_MODEL).astype(jnp.bfloat16)
    return wgu, wd


def _trunk(x, layers, bias):
    b, s, _ = x.shape
    cos_t, sin_t = _rope_tables_t(s)
    for p in layers:
        x, qa_t, qb_t, k, v_t, q_norm2, k_norm2 = _ffn_mix_in(
            x, p["ffn1_pre"], p["ffn1_post"], p["ffn1_wgu"], p["ffn1_wd"],
            p["mix_pre"], p["w_in_t"], p["gq"], p["gk"], cos_t, sin_t)
        oa_t = _global_attention(qa_t, k, v_t, q_norm2, k_norm2)
        ob_t = _window_attention(qb_t, k, v_t, bias, p["sink"])
        x = _mix_out_ffn(x, oa_t, ob_t, p["ga"], p["gb"], p["w_out"], p["mix_post"],
                         p["ffn2_pre"], p["ffn2_post"], p["ffn2_wgu"], p["ffn2_wd"])
    return x


def kernel(x_prompt, x_sample, ffn1_pre, ffn1_post, ffn1_w_gate, ffn1_w_up, ffn1_w_down, mix_pre, mix_post, w_in, a_q_norm, a_k_norm, b_sink, a_out_norm, b_out_norm, w_out, ffn2_pre, ffn2_post, ffn2_w_gate, ffn2_w_up, ffn2_w_down):
    depth = w_in.shape[0]
    rows = _w_in_rows()
    layers = []
    for l in range(depth):
        wgu1, wd1 = _ffn_weights(ffn1_w_gate[l], ffn1_w_up[l], ffn1_w_down[l])
        wgu2, wd2 = _ffn_weights(ffn2_w_gate[l], ffn2_w_up[l], ffn2_w_down[l])
        sink = jnp.repeat(b_sink[l].astype(jnp.float32) * LOG2E, WINDOW)
        layers.append(dict(
            ffn1_pre=ffn1_pre[l][None, :], ffn1_post=ffn1_post[l][None, :], ffn1_wgu=wgu1, ffn1_wd=wd1,
            ffn2_pre=ffn2_pre[l][None, :], ffn2_post=ffn2_post[l][None, :], ffn2_wgu=wgu2, ffn2_wd=wd2,
            mix_pre=mix_pre[l][None, :], mix_post=mix_post[l][None, :],
            w_in_t=w_in[l].T[rows].astype(jnp.bfloat16),
            gq=a_q_norm[l][_DEINTERLEAVE][:, None], gk=a_k_norm[l][_DEINTERLEAVE][:, None],
            sink=sink.reshape(KV_HEADS, 1, HEADS_PER_KV * WINDOW),
            ga=a_out_norm[l][:, None], gb=b_out_norm[l][:, None],
            w_out=w_out[l].astype(jnp.bfloat16),
        ))
    bias = _window_bias()
    return (_trunk(x_prompt, layers, bias), _trunk(x_sample, layers, bias))
```

```python
import functools
import math

import jax
import jax.numpy as jnp
import numpy as np
from jax import lax
from jax.experimental import pallas as pl
from jax.experimental.pallas import tpu as pltpu

D_MODEL = 1024
HEAD_DIM = 64
HALF_DIM = HEAD_DIM // 2
Q_HEADS = 8
KV_HEADS = 2
HEADS_PER_KV = Q_HEADS // KV_HEADS
GROUP_WIDTH = Q_HEADS * HEAD_DIM
KV_WIDTH = KV_HEADS * HEAD_DIM
Q_PER_KV_ROWS = HEADS_PER_KV * HEAD_DIM
GROUP_IN = GROUP_WIDTH + 2 * KV_WIDTH
IN_WIDTH = 2 * GROUP_IN
D_FF = 2816
GRID_W = 64
ROPE_THETA = 10000.0
WINDOW = 128
NORM_EPS = 1e-6
FFN_RESID = 0.5
LOG2E = math.log2(math.e)
Q_SCALE = HEAD_DIM ** -0.5 * LOG2E
NEG_BIG = -1e30
BOUND_SLACK = 1.01
MAX_FIXED_SHIFT = 48.0

FF_CHUNK = 256
N_FF_CHUNKS = D_FF // FF_CHUNK
TOKEN_TILE = 512
V_CHUNK = TOKEN_TILE
Q_TILE = 512
ONES_ROWS = 16
ROW_BLOCK = 128
FIXED_UNROLL = 10
WQ_TILE = 512
N_SUB = WQ_TILE // WINDOW
BAND = 3 * WINDOW

VMEM_LIMIT = 56 * 1024 * 1024
RESIDENT = pl.Buffered(1)


def _rms_rows(x, gain_row):
    ms = jnp.mean(x * x, axis=-1, keepdims=True)
    return x * lax.rsqrt(ms + NORM_EPS) * gain_row


def _rms_cols(xt, gain_col):
    ms = jnp.mean(xt * xt, axis=0, keepdims=True)
    return xt * lax.rsqrt(ms + NORM_EPS) * gain_col


def _zero_after(x):
    bits = lax.bitcast_convert_type(x, jnp.uint32)
    bits = lax.shift_right_logical(lax.shift_right_logical(bits, jnp.uint32(16)), jnp.uint32(16))
    return bits.astype(jnp.float32)


def _ffn_branch(x, pre_ref, post_ref, wg_ref, wu_ref, wd_ref, xn_ref, acc_ref):
    xn_ref[...] = _rms_rows(x, pre_ref[...]).astype(jnp.bfloat16)
    acc_ref[...] = jnp.zeros_like(acc_ref)

    for c in range(N_FF_CHUNKS):
        cols = slice(c * FF_CHUNK, (c + 1) * FF_CHUNK)
        g = jnp.dot(xn_ref[...], wg_ref[:, cols], preferred_element_type=jnp.float32)
        u = jnp.dot(xn_ref[...], wu_ref[:, cols], preferred_element_type=jnp.float32)
        h = (g * jax.nn.sigmoid(g) * u).astype(jnp.bfloat16)
        acc_ref[...] += jnp.dot(h, wd_ref[cols, :], preferred_element_type=jnp.float32)
    return FFN_RESID * _rms_rows(acc_ref[...], post_ref[...])


def _rope_norm_head(xt, gain_col, cos_t, sin_t):
    y = _rms_cols(xt, gain_col)
    x0 = y[:HALF_DIM]
    x1 = y[HALF_DIM:]
    return jnp.concatenate([x0 * cos_t - x1 * sin_t, x0 * sin_t + x1 * cos_t], axis=0)


def _ffn_mix_in_kernel(x_ref, pre1_ref, post1_ref, wg_ref, wu_ref, wd_ref,
                       pre_ref, w_ref, gq_ref, gk_ref, cos_ref, sin_ref,
                       x1_ref, qa_ref, qb_ref, k_ref, v_ref, qn_ref, kn_ref, xn_ref, acc_ref):
    x = x_ref[0]
    x1 = x + _ffn_branch(x, pre1_ref, post1_ref, wg_ref, wu_ref, wd_ref, xn_ref, acc_ref)
    x1_ref[0] = x1
    xn = _rms_rows(x1, pre_ref[...]).astype(jnp.bfloat16)
    pt = lax.dot_general(w_ref[...], xn, (((1,), (1,)), ((), ())),
                         preferred_element_type=jnp.float32)
    pa = pt[:GROUP_IN]
    pb = pt[GROUP_IN:]
    cos_t = cos_ref[...]
    sin_t = sin_ref[...]
    gq = gq_ref[...]
    gk = gk_ref[...]
    off = 0
    for h in range(Q_HEADS):
        r = _rope_norm_head(pa[off:off + HEAD_DIM], gq, cos_t, sin_t) * Q_SCALE
        qa_ref[0, h * HEAD_DIM:(h + 1) * HEAD_DIM, :] = r.astype(jnp.bfloat16)
        qn_ref[0, h] = jnp.sum(r * r, axis=0, keepdims=True)
        off += HEAD_DIM
    for h in range(KV_HEADS):
        r = _rope_norm_head(pa[off:off + HEAD_DIM], gk, cos_t, sin_t)
        k_ref[0, h] = r.T.astype(jnp.bfloat16)
        kn_ref[0, h] = jnp.sum(r * r, axis=0, keepdims=True)
        off += HEAD_DIM
    v_ref[0, 0, :KV_WIDTH, :] = pa[off:off + KV_WIDTH].astype(jnp.bfloat16)
    qb_ref[0] = (pb[:GROUP_WIDTH] * Q_SCALE).astype(jnp.bfloat16)
    off = GROUP_WIDTH
    for h in range(KV_HEADS):
        k_ref[0, KV_HEADS + h] = pb[off:off + HEAD_DIM].T.astype(jnp.bfloat16)
        off += HEAD_DIM
    v_ref[0, 0, KV_WIDTH:, :] = pb[off:off + KV_WIDTH].astype(jnp.bfloat16)


def _ffn_mix_in(x, pre1, post1, wg, wu, wd, pre, w_in_t, gq, gk, cos_t, sin_t):
    b, s, _ = x.shape
    n_tiles = s // TOKEN_TILE
    vec = lambda bi, i: (0, 0)
    return pl.pallas_call(
        _ffn_mix_in_kernel,
        name="ffn_mix_in",
        grid=(b, n_tiles),
        in_specs=[
            pl.BlockSpec((1, TOKEN_TILE, D_MODEL), lambda bi, i: (bi, i, 0)),
            pl.BlockSpec((1, D_MODEL), vec),
            pl.BlockSpec((1, D_MODEL), vec),
            pl.BlockSpec((D_MODEL, D_FF), vec, pipeline_mode=RESIDENT),
            pl.BlockSpec((D_MODEL, D_FF), vec, pipeline_mode=RESIDENT),
            pl.BlockSpec((D_FF, D_MODEL), vec, pipeline_mode=RESIDENT),
            pl.BlockSpec((1, D_MODEL), vec),
            pl.BlockSpec((IN_WIDTH, D_MODEL), vec, pipeline_mode=RESIDENT),
            pl.BlockSpec((HEAD_DIM, 1), vec),
            pl.BlockSpec((HEAD_DIM, 1), vec),
            pl.BlockSpec((HALF_DIM, TOKEN_TILE), lambda bi, i: (0, i)),
            pl.BlockSpec((HALF_DIM, TOKEN_TILE), lambda bi, i: (0, i)),
        ],
        out_specs=[
            pl.BlockSpec((1, TOKEN_TILE, D_MODEL), lambda bi, i: (bi, i, 0)),
            pl.BlockSpec((1, GROUP_WIDTH, TOKEN_TILE), lambda bi, i: (bi, 0, i)),
            pl.BlockSpec((1, GROUP_WIDTH, TOKEN_TILE), lambda bi, i: (bi, 0, i)),
            pl.BlockSpec((1, 2 * KV_HEADS, TOKEN_TILE, HEAD_DIM), lambda bi, i: (bi, 0, i, 0)),
            pl.BlockSpec((1, 1, 2 * KV_WIDTH, V_CHUNK), lambda bi, i: (bi, i, 0, 0)),
            pl.BlockSpec((1, Q_HEADS, 1, TOKEN_TILE), lambda bi, i: (bi, 0, 0, i)),
            pl.BlockSpec((1, KV_HEADS, 1, TOKEN_TILE), lambda bi, i: (bi, 0, 0, i)),
        ],
        out_shape=[
            jax.ShapeDtypeStruct((b, s, D_MODEL), jnp.float32),
            jax.ShapeDtypeStruct((b, GROUP_WIDTH, s), jnp.bfloat16),
            jax.ShapeDtypeStruct((b, GROUP_WIDTH, s), jnp.bfloat16),
            jax.ShapeDtypeStruct((b, 2 * KV_HEADS, s, HEAD_DIM), jnp.bfloat16),
            jax.ShapeDtypeStruct((b, n_tiles, 2 * KV_WIDTH, V_CHUNK), jnp.bfloat16),
            jax.ShapeDtypeStruct((b, Q_HEADS, 1, s), jnp.float32),
            jax.ShapeDtypeStruct((b, KV_HEADS, 1, s), jnp.float32),
        ],
        scratch_shapes=[
            pltpu.VMEM((TOKEN_TILE, D_MODEL), jnp.bfloat16),
            pltpu.VMEM((TOKEN_TILE, D_MODEL), jnp.float32),
        ],
        compiler_params=pltpu.CompilerParams(
            dimension_semantics=("arbitrary", "arbitrary"), vmem_limit_bytes=VMEM_LIMIT),
    )(x, pre1, post1, wg, wu, wd, pre, w_in_t, gq, gk, cos_t, sin_t)


def _global_kernel(q_ref, k_ref, v_ref, qn_ref, kn_ref, o_ref, s_ref, smax_ref, p_ref, alpha_ref, m_ref,
                   c_ref, acc_ref, *, n_chunks):
    ones = jnp.ones((ONES_ROWS, V_CHUNK), jnp.bfloat16)
    n_blocks = V_CHUNK // ROW_BLOCK
    acc_ref[...] = jnp.zeros_like(acc_ref)

    kmax2 = jnp.max(kn_ref[0, 0], axis=1, keepdims=True)
    c_ref[...] = BOUND_SLACK * jnp.sqrt(qn_ref[0] * kmax2)
    use_fixed_shift = jnp.max(c_ref[...]) <= MAX_FIXED_SHIFT

    def k_rows(chunk, r):
        start = chunk * V_CHUNK + r * ROW_BLOCK
        if not isinstance(start, int):
            start = pl.multiple_of(start, ROW_BLOCK)
        return k_ref[0, 0, pl.ds(start, ROW_BLOCK), :]

    def v_ext(chunk):
        return jnp.concatenate([v_ref[0, chunk], ones], axis=0)

    def values_block(vext, slot, h, r):
        cols = slice((r - 1) * ROW_BLOCK, (r + 1) * ROW_BLOCK)
        return jnp.dot(vext[:, cols], p_ref[slot, h, cols, :], preferred_element_type=jnp.float32)

    def fixed_step(j, parity, do_scores=True, do_values=True):
        cur, other = parity, 1 - parity
        if do_values:
            vext = v_ext(j)
        after_values = None
        for h in range(HEADS_PER_KV):
            q_h = q_ref[0, h * HEAD_DIM:(h + 1) * HEAD_DIM, :]
            c_h = c_ref[h]
            pv = None
            for r in range(n_blocks):
                rows = slice(r * ROW_BLOCK, (r + 1) * ROW_BLOCK)
                if do_scores:
                    st = jnp.dot(k_rows(j + 1, r), q_h, preferred_element_type=jnp.float32)
                    shift = c_h if after_values is None else c_h + after_values
                    p_ref[other, h, rows, :] = jnp.exp2(st - shift).astype(jnp.bfloat16)
                if do_values and r % 2 == 1:
                    d = values_block(vext, cur, h, r)
                    pv = d if pv is None else pv + d
                    after_values = _zero_after(d[HEAD_DIM:HEAD_DIM + 1])
            if do_values:
                acc_ref[h] += pv

    def online_step(j, parity, do_scores=True, do_softmax=True, do_values=True):
        cur, other = parity, 1 - parity
        if do_values:
            vext = v_ext(j - 1)
        for h in range(HEADS_PER_KV):
            q_h = q_ref[0, h * HEAD_DIM:(h + 1) * HEAD_DIM, :]
            if do_softmax:
                m_old = m_ref[h]
                m_new = jnp.maximum(m_old, smax_ref[cur, h])
                alpha_ref[cur, h] = jnp.exp2(m_old - m_new)
                m_ref[h] = m_new
            blk_max = None
            pv = None
            for r in range(n_blocks):
                rows = slice(r * ROW_BLOCK, (r + 1) * ROW_BLOCK)
                if do_scores:
                    st = jnp.dot(k_rows(j + 1, r), q_h, preferred_element_type=jnp.float32)
                    s_ref[other, h, rows, :] = st
                    part = jnp.max(st.reshape(ROW_BLOCK // 8, 8, Q_TILE), axis=0)
                    blk_max = part if blk_max is None else jnp.maximum(blk_max, part)
                if do_softmax:
                    p_ref[cur, h, rows, :] = jnp.exp2(s_ref[cur, h, rows, :] - m_new).astype(
                        jnp.bfloat16)
                if do_values and r % 2 == 1:
                    d = values_block(vext, other, h, r)
                    pv = d if pv is None else pv + d
            if do_scores:
                smax_ref[other, h] = jnp.max(blk_max, axis=0, keepdims=True)
            if do_values:
                acc_ref[h] = alpha_ref[other, h] * acc_ref[h] + pv

    @pl.when(use_fixed_shift)
    def _():
        fixed_step(-1, 1, do_values=False)

        def steps(t, carry):
            for u in range(FIXED_UNROLL):
                fixed_step(FIXED_UNROLL * t + u, u % 2)
            return carry

        n_loop = (n_chunks - 2) // FIXED_UNROLL
        lax.fori_loop(0, n_loop, steps, 0)
        for j in range(n_loop * FIXED_UNROLL, n_chunks - 2):
            fixed_step(j, j % 2)
        fixed_step(n_chunks - 2, 0)
        fixed_step(n_chunks - 1, 1, do_scores=False)

    @pl.when(jnp.logical_not(use_fixed_shift))
    def _():
        m_ref[...] = jnp.full_like(m_ref, NEG_BIG)
        online_step(-1, 1, do_softmax=False, do_values=False)
        online_step(0, 0, do_values=False)

        def pair(t, carry):
            online_step(2 * t + 1, 1)
            online_step(2 * t + 2, 0)
            return carry

        lax.fori_loop(0, (n_chunks - 2) // 2, pair, 0)
        online_step(n_chunks - 1, 1, do_scores=False)
        online_step(n_chunks, 0, do_scores=False, do_softmax=False)

    for h in range(HEADS_PER_KV):
        a = acc_ref[h]
        o_ref[0, h * HEAD_DIM:(h + 1) * HEAD_DIM, :] = a[:HEAD_DIM] / a[HEAD_DIM:HEAD_DIM + 1]


def _global_attention(q_t, k, v_t, q_norm2, k_norm2):
    b, _, s = q_t.shape
    n_chunks = s // V_CHUNK
    assert n_chunks >= 2 and n_chunks % 2 == 0, "the kv pipeline runs chunks in pairs"
    return pl.pallas_call(
        functools.partial(_global_kernel, n_chunks=n_chunks),
        name="global_attn",
        grid=(b, KV_HEADS, s // Q_TILE),
        in_specs=[
            pl.BlockSpec((1, Q_PER_KV_ROWS, Q_TILE), lambda bi, g, i: (bi, g, i)),
            pl.BlockSpec((1, 1, s, HEAD_DIM), lambda bi, g, i: (bi, g, 0, 0)),
            pl.BlockSpec((1, n_chunks, HEAD_DIM, V_CHUNK), lambda bi, g, i: (bi, 0, g, 0)),
            pl.BlockSpec((1, HEADS_PER_KV, 1, Q_TILE), lambda bi, g, i: (bi, g, 0, i)),
            pl.BlockSpec((1, 1, 1, s), lambda bi, g, i: (bi, g, 0, 0)),
        ],
        out_specs=pl.BlockSpec((1, Q_PER_KV_ROWS, Q_TILE), lambda bi, g, i: (bi, g, i)),
        out_shape=jax.ShapeDtypeStruct((b, GROUP_WIDTH, s), jnp.float32),
        scratch_shapes=[
            pltpu.VMEM((2, HEADS_PER_KV, V_CHUNK, Q_TILE), jnp.float32),
            pltpu.VMEM((2, HEADS_PER_KV, 1, Q_TILE), jnp.float32),
            pltpu.VMEM((2, HEADS_PER_KV, V_CHUNK, Q_TILE), jnp.bfloat16),
            pltpu.VMEM((2, HEADS_PER_KV, 1, Q_TILE), jnp.float32),
            pltpu.VMEM((HEADS_PER_KV, 1, Q_TILE), jnp.float32),
            pltpu.VMEM((HEADS_PER_KV, 1, Q_TILE), jnp.float32),
            pltpu.VMEM((HEADS_PER_KV, HEAD_DIM + ONES_ROWS, Q_TILE), jnp.float32),
        ],
        compiler_params=pltpu.CompilerParams(
            dimension_semantics=("arbitrary", "arbitrary", "arbitrary"),
            vmem_limit_bytes=VMEM_LIMIT),
    )(q_t, k, v_t, q_norm2, k_norm2)


def _window_kernel(q_ref, kp_ref, kc_ref, kn_ref, vp_ref, vc_ref, vn_ref, bias_ref, sink_ref,
                   o_ref, s_ref, mx_ref, p_ref):
    i = pl.program_id(2)
    has_prev = i > 0
    has_next = i < pl.num_programs(2) - 1
    sink = sink_ref[0]
    ones = jnp.ones((ONES_ROWS, BAND), jnp.bfloat16)
    no_prev = jnp.where(has_prev, 0.0, NEG_BIG)
    no_next = jnp.where(has_next, 0.0, NEG_BIG)

    def band(sb, prev_ref, cur_ref, next_ref, axis):
        def cur(lo, hi):
            idx = (0, 0, slice(lo, hi), slice(None)) if axis == 0 else (0, 0, slice(None), slice(lo, hi))
            return cur_ref[idx]
        lo = (sb - 1) * WINDOW
        if sb == 0:
            parts = [prev_ref[0, 0], cur(0, 2 * WINDOW)]
        elif sb == N_SUB - 1:
            parts = [cur(lo, WQ_TILE), next_ref[0, 0]]
        else:
            return cur(lo, lo + BAND)
        return jnp.concatenate(parts, axis=axis)

    def scores(sb):
        lo = sb * WINDOW
        qs = jnp.concatenate(
            [q_ref[0, h * HEAD_DIM:(h + 1) * HEAD_DIM, lo:lo + WINDOW] for h in range(HEADS_PER_KV)],
            axis=1)
        ks = band(sb, kp_ref, kc_ref, kn_ref, 0)
        st = jnp.dot(ks, qs, preferred_element_type=jnp.float32) + bias_ref[0]
        if sb == 0:
            st = jnp.concatenate([st[:WINDOW] + no_prev, st[WINDOW:]], axis=0)
        if sb == N_SUB - 1:
            st = jnp.concatenate([st[:2 * WINDOW], st[2 * WINDOW:] + no_next], axis=0)
        s_ref[sb] = st
        mx_ref[sb] = jnp.maximum(jnp.max(st, axis=0, keepdims=True), sink)

    def probabilities(sb):
        p_ref[sb] = jnp.exp2(s_ref[sb] - mx_ref[sb]).astype(jnp.bfloat16)

    def values(sb):
        lo = sb * WINDOW
        vext = jnp.concatenate([band(sb, vp_ref, vc_ref, vn_ref, 1), ones], axis=0)
        ot = jnp.dot(vext, p_ref[sb], preferred_element_type=jnp.float32)
        denom = ot[HEAD_DIM:HEAD_DIM + 1] + jnp.exp2(sink - mx_ref[sb])
        ot = ot[:HEAD_DIM] / denom
        for h in range(HEADS_PER_KV):
            o_ref[0, h * HEAD_DIM:(h + 1) * HEAD_DIM, lo:lo + WINDOW] = (
                ot[:, h * WINDOW:(h + 1) * WINDOW])

    for t in range(N_SUB + 2):
        if t < N_SUB:
            scores(t)
        if 1 <= t <= N_SUB:
            probabilities(t - 1)
        if t >= 2:
            values(t - 2)


def _window_attention(q_t, k, v_t, bias, sink):
    b, _, s = q_t.shape
    n_blk = s // WINDOW
    per = WQ_TILE // WINDOW
    vper = V_CHUNK // WINDOW

    def prev_blk(i):
        return jnp.maximum(i * per - 1, 0)

    def next_blk(i):
        return jnp.minimum(i * per + per, n_blk - 1)

    kh = lambda g: KV_HEADS + g
    return pl.pallas_call(
        _window_kernel,
        name="window_attn",
        grid=(b, KV_HEADS, s // WQ_TILE),
        in_specs=[
            pl.BlockSpec((1, Q_PER_KV_ROWS, WQ_TILE), lambda bi, g, i: (bi, g, i)),
            pl.BlockSpec((1, 1, WINDOW, HEAD_DIM), lambda bi, g, i: (bi, kh(g), prev_blk(i), 0)),
            pl.BlockSpec((1, 1, WQ_TILE, HEAD_DIM), lambda bi, g, i: (bi, kh(g), i, 0)),
            pl.BlockSpec((1, 1, WINDOW, HEAD_DIM), lambda bi, g, i: (bi, kh(g), next_blk(i), 0)),
            pl.BlockSpec((1, 1, HEAD_DIM, WINDOW),
                         lambda bi, g, i: (bi, prev_blk(i) // vper, kh(g), prev_blk(i) % vper)),
            pl.BlockSpec((1, 1, HEAD_DIM, WQ_TILE), lambda bi, g, i: (bi, i, kh(g), 0)),
            pl.BlockSpec((1, 1, HEAD_DIM, WINDOW),
                         lambda bi, g, i: (bi, next_blk(i) // vper, kh(g), next_blk(i) % vper)),
            pl.BlockSpec((1, BAND, HEADS_PER_KV * WINDOW), lambda bi, g, i: (g, 0, 0)),
            pl.BlockSpec((1, 1, HEADS_PER_KV * WINDOW), lambda bi, g, i: (g, 0, 0)),
        ],
        out_specs=pl.BlockSpec((1, Q_PER_KV_ROWS, WQ_TILE), lambda bi, g, i: (bi, g, i)),
        out_shape=jax.ShapeDtypeStruct((b, GROUP_WIDTH, s), jnp.float32),
        scratch_shapes=[
            pltpu.VMEM((N_SUB, BAND, HEADS_PER_KV * WINDOW), jnp.float32),
            pltpu.VMEM((N_SUB, 1, HEADS_PER_KV * WINDOW), jnp.float32),
            pltpu.VMEM((N_SUB, BAND, HEADS_PER_KV * WINDOW), jnp.bfloat16),
        ],
        compiler_params=pltpu.CompilerParams(
            dimension_semantics=("arbitrary", "arbitrary", "arbitrary"),
            vmem_limit_bytes=VMEM_LIMIT),
    )(q_t, k, k, k, v_t, v_t, v_t, bias, sink)


def _mix_out_ffn_kernel(x_ref, oa_ref, ob_ref, ga_ref, gb_ref, w_ref, post_ref,
                        pre2_ref, post2_ref, wg_ref, wu_ref, wd_ref, o_ref, xn_ref, acc_ref):
    na = _rms_cols(oa_ref[0], ga_ref[...])
    nb = _rms_cols(ob_ref[0], gb_ref[...])
    mt = jnp.concatenate([na, nb], axis=0).astype(jnp.bfloat16)
    h = lax.dot_general(mt, w_ref[...], (((0,), (0,)), ((), ())),
                        preferred_element_type=jnp.float32)
    x1 = x_ref[0] + _rms_rows(h, post_ref[...])
    o_ref[0] = x1
    o_ref[0] += _ffn_branch(x1, pre2_ref, post2_ref, wg_ref, wu_ref, wd_ref, xn_ref, acc_ref)


def _mix_out_ffn(x, oa_t, ob_t, ga, gb, w_out, post, pre2, post2, wg, wu, wd):
    b, s, _ = x.shape
    vec = lambda bi, i: (0, 0)
    return pl.pallas_call(
        _mix_out_ffn_kernel,
        name="mix_out_ffn",
        grid=(b, s // TOKEN_TILE),
        in_specs=[
            pl.BlockSpec((1, TOKEN_TILE, D_MODEL), lambda bi, i: (bi, i, 0)),
            pl.BlockSpec((1, GROUP_WIDTH, TOKEN_TILE), lambda bi, i: (bi, 0, i)),
            pl.BlockSpec((1, GROUP_WIDTH, TOKEN_TILE), lambda bi, i: (bi, 0, i)),
            pl.BlockSpec((GROUP_WIDTH, 1), vec),
            pl.BlockSpec((GROUP_WIDTH, 1), vec),
            pl.BlockSpec((D_MODEL, D_MODEL), vec, pipeline_mode=RESIDENT),
            pl.BlockSpec((1, D_MODEL), vec),
            pl.BlockSpec((1, D_MODEL), vec),
            pl.BlockSpec((1, D_MODEL), vec),
            pl.BlockSpec((D_MODEL, D_FF), vec, pipeline_mode=RESIDENT),
            pl.BlockSpec((D_MODEL, D_FF), vec, pipeline_mode=RESIDENT),
            pl.BlockSpec((D_FF, D_MODEL), vec, pipeline_mode=RESIDENT),
        ],
        out_specs=pl.BlockSpec((1, TOKEN_TILE, D_MODEL), lambda bi, i: (bi, i, 0)),
        out_shape=jax.ShapeDtypeStruct((b, s, D_MODEL), jnp.float32),
        scratch_shapes=[
            pltpu.VMEM((TOKEN_TILE, D_MODEL), jnp.bfloat16),
            pltpu.VMEM((TOKEN_TILE, D_MODEL), jnp.float32),
        ],
        compiler_params=pltpu.CompilerParams(
            dimension_semantics=("arbitrary", "arbitrary"), vmem_limit_bytes=VMEM_LIMIT),
    )(x, oa_t, ob_t, ga, gb, w_out, post, pre2, post2, wg, wu, wd)


_DEINTERLEAVE = np.concatenate([np.arange(0, HEAD_DIM, 2), np.arange(1, HEAD_DIM, 2)])


def _w_in_rows():
    n_rot = (Q_HEADS + KV_HEADS) * HEAD_DIM
    rot = (np.arange(Q_HEADS + KV_HEADS)[:, None] * HEAD_DIM + _DEINTERLEAVE[None, :]).reshape(-1)
    return np.concatenate([rot, np.arange(n_rot, IN_WIDTH)])


def _rope_tables_t(seq_len):
    n_rows = seq_len // GRID_W
    row = jnp.repeat(jnp.arange(n_rows, dtype=jnp.float32), GRID_W)
    col = jnp.tile(jnp.arange(GRID_W, dtype=jnp.float32), n_rows)
    n_freq = HEAD_DIM // 4
    inv_freq = ROPE_THETA ** (-jnp.arange(n_freq, dtype=jnp.float32) / n_freq)
    ang = jnp.concatenate([inv_freq[:, None] * row[None, :], inv_freq[:, None] * col[None, :]], axis=0)
    return jnp.cos(ang), jnp.sin(ang)


def _window_bias():
    slopes = 2.0 ** (-8.0 * np.arange(1, Q_HEADS + 1, dtype=np.float64) / Q_HEADS)
    kpos = np.arange(BAND)[:, None] - WINDOW
    qpos = np.arange(WINDOW)[None, :]
    dist = np.abs(qpos - kpos).astype(np.float64)
    per_head = np.where(dist[None] <= WINDOW, -slopes[:, None, None] * dist[None] * LOG2E, NEG_BIG)
    per_kv = per_head.reshape(KV_HEADS, HEADS_PER_KV, BAND, WINDOW).transpose(0, 2, 1, 3)
    return jnp.asarray(per_kv.reshape(KV_HEADS, BAND, HEADS_PER_KV * WINDOW), jnp.float32)


def _trunk(x, layers, bias):
    b, s, _ = x.shape
    cos_t, sin_t = _rope_tables_t(s)
    for p in layers:
        x, qa_t, qb_t, k, v_t, q_norm2, k_norm2 = _ffn_mix_in(
            x, p["ffn1_pre"], p["ffn1_post"], p["ffn1_wg"], p["ffn1_wu"], p["ffn1_wd"],
            p["mix_pre"], p["w_in_t"], p["gq"], p["gk"], cos_t, sin_t)
        oa_t = _global_attention(qa_t, k, v_t, q_norm2, k_norm2)
        ob_t = _window_attention(qb_t, k, v_t, bias, p["sink"])
        x = _mix_out_ffn(x, oa_t, ob_t, p["ga"], p["gb"], p["w_out"], p["mix_post"],
                         p["ffn2_pre"], p["ffn2_post"], p["ffn2_wg"], p["ffn2_wu"], p["ffn2_wd"])
    return x


def kernel(x_prompt, x_sample, ffn1_pre, ffn1_post, ffn1_w_gate, ffn1_w_up, ffn1_w_down, mix_pre, mix_post, w_in, a_q_norm, a_k_norm, b_sink, a_out_norm, b_out_norm, w_out, ffn2_pre, ffn2_post, ffn2_w_gate, ffn2_w_up, ffn2_w_down):
    depth = w_in.shape[0]
    rows = _w_in_rows()
    layers = []
    for l in range(depth):
        bf16 = lambda w: w[l].astype(jnp.bfloat16)
        sink = jnp.repeat(b_sink[l].astype(jnp.float32) * LOG2E, WINDOW)
        layers.append(dict(
            ffn1_pre=ffn1_pre[l][None, :], ffn1_post=ffn1_post[l][None, :],
            ffn1_wg=bf16(ffn1_w_gate), ffn1_wu=bf16(ffn1_w_up), ffn1_wd=bf16(ffn1_w_down),
            ffn2_pre=ffn2_pre[l][None, :], ffn2_post=ffn2_post[l][None, :],
            ffn2_wg=bf16(ffn2_w_gate), ffn2_wu=bf16(ffn2_w_up), ffn2_wd=bf16(ffn2_w_down),
            mix_pre=mix_pre[l][None, :], mix_post=mix_post[l][None, :],
            w_in_t=w_in[l].T[rows].astype(jnp.bfloat16),
            gq=a_q_norm[l][_DEINTERLEAVE][:, None], gk=a_k_norm[l][_DEINTERLEAVE][:, None],
            sink=sink.reshape(KV_HEADS, 1, HEADS_PER_KV * WINDOW),
            ga=a_out_norm[l][:, None], gb=b_out_norm[l][:, None],
            w_out=w_out[l].astype(jnp.bfloat16),
        ))
    bias = _window_bias()
    return (_trunk(x_prompt, layers, bias), _trunk(x_sample, layers, bias))
```

```python
import functools
import math

import jax
import jax.numpy as jnp
import numpy as np
from jax import lax
from jax.experimental import pallas as pl
from jax.experimental.pallas import tpu as pltpu

D_MODEL = 1024
HEAD_DIM = 64
HALF_DIM = HEAD_DIM // 2
Q_HEADS = 8
KV_HEADS = 2
HEADS_PER_KV = Q_HEADS // KV_HEADS
GROUP_WIDTH = Q_HEADS * HEAD_DIM
KV_WIDTH = KV_HEADS * HEAD_DIM
Q_PER_KV_ROWS = HEADS_PER_KV * HEAD_DIM
GROUP_IN = GROUP_WIDTH + 2 * KV_WIDTH
IN_WIDTH = 2 * GROUP_IN
D_FF = 2816
GRID_W = 64
ROPE_THETA = 10000.0
WINDOW = 128
NORM_EPS = 1e-6
FFN_RESID = 0.5
LOG2E = math.log2(math.e)
Q_SCALE = HEAD_DIM ** -0.5 * LOG2E
NEG_BIG = -1e30
BOUND_SLACK = 1.01
MAX_FIXED_SHIFT = 48.0

FF_CHUNK = 256
N_FF_CHUNKS = D_FF // FF_CHUNK
TOKEN_TILE = 512
V_CHUNK = TOKEN_TILE
Q_TILE = 512
ONES_ROWS = 16
ROW_BLOCK = 128
FIXED_UNROLL = 10
WQ_TILE = 512
N_SUB = WQ_TILE // WINDOW
BAND = 3 * WINDOW

VMEM_LIMIT = 56 * 1024 * 1024
RESIDENT = pl.Buffered(1)


def _rms_rows(x, gain_row):
    ms = jnp.mean(x * x, axis=-1, keepdims=True)
    return x * lax.rsqrt(ms + NORM_EPS) * gain_row


def _rms_cols(xt, gain_col):
    ms = jnp.mean(xt * xt, axis=0, keepdims=True)
    return xt * lax.rsqrt(ms + NORM_EPS) * gain_col


def _zero_after(x):
    bits = lax.bitcast_convert_type(x, jnp.uint32)
    bits = lax.shift_right_logical(lax.shift_right_logical(bits, jnp.uint32(16)), jnp.uint32(16))
    return bits.astype(jnp.float32)


def _ffn_branch(x, pre_ref, post_ref, wg_ref, wu_ref, wd_ref, xn_ref, acc_ref):
    xn_ref[...] = _rms_rows(x, pre_ref[...]).astype(jnp.bfloat16)
    acc_ref[...] = jnp.zeros_like(acc_ref)

    for c in range(N_FF_CHUNKS):
        cols = slice(c * FF_CHUNK, (c + 1) * FF_CHUNK)
        g = jnp.dot(xn_ref[...], wg_ref[:, cols], preferred_element_type=jnp.float32)
        u = jnp.dot(xn_ref[...], wu_ref[:, cols], preferred_element_type=jnp.float32)
        h = (g * jax.nn.sigmoid(g) * u).astype(jnp.bfloat16)
        acc_ref[...] += jnp.dot(h, wd_ref[cols, :], preferred_element_type=jnp.float32)
    return FFN_RESID * _rms_rows(acc_ref[...], post_ref[...])


def _rope_norm_head(xt, gain_col, cos_t, sin_t):
    y = _rms_cols(xt, gain_col)
    x0 = y[:HALF_DIM]
    x1 = y[HALF_DIM:]
    return jnp.concatenate([x0 * cos_t - x1 * sin_t, x0 * sin_t + x1 * cos_t], axis=0)


def _ffn_mix_in_kernel(x_ref, pre1_ref, post1_ref, wg_ref, wu_ref, wd_ref,
                       pre_ref, w_ref, gq_ref, gk_ref, cos_ref, sin_ref,
                       x1_ref, qa_ref, qb_ref, k_ref, v_ref, qn_ref, kn_ref, xn_ref, acc_ref):
    x = x_ref[0]
    x1 = x + _ffn_branch(x, pre1_ref, post1_ref, wg_ref, wu_ref, wd_ref, xn_ref, acc_ref)
    x1_ref[0] = x1
    xn = _rms_rows(x1, pre_ref[...]).astype(jnp.bfloat16)
    pt = lax.dot_general(w_ref[...], xn, (((1,), (1,)), ((), ())),
                         preferred_element_type=jnp.float32)
    pa = pt[:GROUP_IN]
    pb = pt[GROUP_IN:]
    cos_t = cos_ref[...]
    sin_t = sin_ref[...]
    gq = gq_ref[...]
    gk = gk_ref[...]
    off = 0
    for h in range(Q_HEADS):
        r = _rope_norm_head(pa[off:off + HEAD_DIM], gq, cos_t, sin_t) * Q_SCALE
        qa_ref[0, h * HEAD_DIM:(h + 1) * HEAD_DIM, :] = r.astype(jnp.bfloat16)
        qn_ref[0, h] = jnp.sum(r * r, axis=0, keepdims=True)
        off += HEAD_DIM
    for h in range(KV_HEADS):
        r = _rope_norm_head(pa[off:off + HEAD_DIM], gk, cos_t, sin_t)
        k_ref[0, h] = r.T.astype(jnp.bfloat16)
        kn_ref[0, h] = jnp.sum(r * r, axis=0, keepdims=True)
        off += HEAD_DIM
    v_ref[0, 0, :KV_WIDTH, :] = pa[off:off + KV_WIDTH].astype(jnp.bfloat16)
    qb_ref[0] = (pb[:GROUP_WIDTH] * Q_SCALE).astype(jnp.bfloat16)
    off = GROUP_WIDTH
    for h in range(KV_HEADS):
        k_ref[0, KV_HEADS + h] = pb[off:off + HEAD_DIM].T.astype(jnp.bfloat16)
        off += HEAD_DIM
    v_ref[0, 0, KV_WIDTH:, :] = pb[off:off + KV_WIDTH].astype(jnp.bfloat16)


def _ffn_mix_in(x, pre1, post1, wg, wu, wd, pre, w_in_t, gq, gk, cos_t, sin_t):
    b, s, _ = x.shape
    n_tiles = s // TOKEN_TILE
    vec = lambda bi, i: (0, 0)
    return pl.pallas_call(
        _ffn_mix_in_kernel,
        name="ffn_mix_in",
        grid=(b, n_tiles),
        in_specs=[
            pl.BlockSpec((1, TOKEN_TILE, D_MODEL), lambda bi, i: (bi, i, 0)),
            pl.BlockSpec((1, D_MODEL), vec),
            pl.BlockSpec((1, D_MODEL), vec),
            pl.BlockSpec((D_MODEL, D_FF), vec, pipeline_mode=RESIDENT),
            pl.BlockSpec((D_MODEL, D_FF), vec, pipeline_mode=RESIDENT),
            pl.BlockSpec((D_FF, D_MODEL), vec, pipeline_mode=RESIDENT),
            pl.BlockSpec((1, D_MODEL), vec),
            pl.BlockSpec((IN_WIDTH, D_MODEL), vec, pipeline_mode=RESIDENT),
            pl.BlockSpec((HEAD_DIM, 1), vec),
            pl.BlockSpec((HEAD_DIM, 1), vec),
            pl.BlockSpec((HALF_DIM, TOKEN_TILE), lambda bi, i: (0, i)),
            pl.BlockSpec((HALF_DIM, TOKEN_TILE), lambda bi, i: (0, i)),
        ],
        out_specs=[
            pl.BlockSpec((1, TOKEN_TILE, D_MODEL), lambda bi, i: (bi, i, 0)),
            pl.BlockSpec((1, GROUP_WIDTH, TOKEN_TILE), lambda bi, i: (bi, 0, i)),
            pl.BlockSpec((1, GROUP_WIDTH, TOKEN_TILE), lambda bi, i: (bi, 0, i)),
            pl.BlockSpec((1, 2 * KV_HEADS, TOKEN_TILE, HEAD_DIM), lambda bi, i: (bi, 0, i, 0)),
            pl.BlockSpec((1, 1, 2 * KV_WIDTH, V_CHUNK), lambda bi, i: (bi, i, 0, 0)),
            pl.BlockSpec((1, Q_HEADS, 1, TOKEN_TILE), lambda bi, i: (bi, 0, 0, i)),
            pl.BlockSpec((1, KV_HEADS, 1, TOKEN_TILE), lambda bi, i: (bi, 0, 0, i)),
        ],
        out_shape=[
            jax.ShapeDtypeStruct((b, s, D_MODEL), jnp.float32),
            jax.ShapeDtypeStruct((b, GROUP_WIDTH, s), jnp.bfloat16),
            jax.ShapeDtypeStruct((b, GROUP_WIDTH, s), jnp.bfloat16),
            jax.ShapeDtypeStruct((b, 2 * KV_HEADS, s, HEAD_DIM), jnp.bfloat16),
            jax.ShapeDtypeStruct((b, n_tiles, 2 * KV_WIDTH, V_CHUNK), jnp.bfloat16),
            jax.ShapeDtypeStruct((b, Q_HEADS, 1, s), jnp.float32),
            jax.ShapeDtypeStruct((b, KV_HEADS, 1, s), jnp.float32),
        ],
        scratch_shapes=[
            pltpu.VMEM((TOKEN_TILE, D_MODEL), jnp.bfloat16),
            pltpu.VMEM((TOKEN_TILE, D_MODEL), jnp.float32),
        ],
        compiler_params=pltpu.CompilerParams(
            dimension_semantics=("arbitrary", "arbitrary"), vmem_limit_bytes=VMEM_LIMIT),
    )(x, pre1, post1, wg, wu, wd, pre, w_in_t, gq, gk, cos_t, sin_t)


def _global_kernel(q_ref, k_ref, v_ref, qn_ref, kn_ref, o_ref, s_ref, smax_ref, p_ref, alpha_ref, m_ref,
                   c_ref, l_ref, acc_ref, *, n_chunks):
    ones = jnp.ones((ONES_ROWS, V_CHUNK), jnp.bfloat16)
    n_blocks = V_CHUNK // ROW_BLOCK
    acc_ref[...] = jnp.zeros_like(acc_ref)

    kmax2 = jnp.max(kn_ref[0, 0], axis=1, keepdims=True)
    c_ref[...] = BOUND_SLACK * jnp.sqrt(qn_ref[0] * kmax2)
    use_fixed_shift = jnp.max(c_ref[...]) <= MAX_FIXED_SHIFT

    def k_rows(chunk, r):
        start = chunk * V_CHUNK + r * ROW_BLOCK
        if not isinstance(start, int):
            start = pl.multiple_of(start, ROW_BLOCK)
        return k_ref[0, 0, pl.ds(start, ROW_BLOCK), :]

    def v_ext(chunk):
        return jnp.concatenate([v_ref[0, chunk], ones], axis=0)

    def values_block(vext, slot, h, r):
        cols = slice((r - 1) * ROW_BLOCK, (r + 1) * ROW_BLOCK)
        return jnp.dot(vext[:, cols], p_ref[slot, h, cols, :], preferred_element_type=jnp.float32)

    def fixed_step(j, parity, do_scores=True, do_values=True):
        cur, other = parity, 1 - parity
        if do_values:
            vt = v_ref[0, j]
        after_values = None
        for h in range(HEADS_PER_KV):
            q_h = q_ref[0, h * HEAD_DIM:(h + 1) * HEAD_DIM, :]
            c_h = c_ref[h]
            pv = None
            row_sum = None
            for r in range(n_blocks):
                rows = slice(r * ROW_BLOCK, (r + 1) * ROW_BLOCK)
                if do_scores:
                    st = jnp.dot(k_rows(j + 1, r), q_h, preferred_element_type=jnp.float32)
                    shift = c_h if after_values is None else c_h + after_values
                    p = jnp.exp2(st - shift)
                    p_ref[other, h, rows, :] = p.astype(jnp.bfloat16)
                    part = jnp.sum(p.reshape(ROW_BLOCK // 8, 8, Q_TILE), axis=0)
                    row_sum = part if row_sum is None else row_sum + part
                if do_values and r % 2 == 1:
                    d = values_block(vt, cur, h, r)
                    pv = d if pv is None else pv + d
                    after_values = _zero_after(d[:1])
            if do_scores:
                l_ref[h] += row_sum
            if do_values:
                acc_ref[h, :HEAD_DIM, :] += pv

    def online_step(j, parity, do_scores=True, do_softmax=True, do_values=True):
        cur, other = parity, 1 - parity
        if do_values:
            vext = v_ext(j - 1)
        for h in range(HEADS_PER_KV):
            q_h = q_ref[0, h * HEAD_DIM:(h + 1) * HEAD_DIM, :]
            if do_softmax:
                m_old = m_ref[h]
                m_new = jnp.maximum(m_old, smax_ref[cur, h])
                alpha_ref[cur, h] = jnp.exp2(m_old - m_new)
                m_ref[h] = m_new
            blk_max = None
            pv = None
            for r in range(n_blocks):
                rows = slice(r * ROW_BLOCK, (r + 1) * ROW_BLOCK)
                if do_scores:
                    st = jnp.dot(k_rows(j + 1, r), q_h, preferred_element_type=jnp.float32)
                    s_ref[other, h, rows, :] = st
                    part = jnp.max(st.reshape(ROW_BLOCK // 8, 8, Q_TILE), axis=0)
                    blk_max = part if blk_max is None else jnp.maximum(blk_max, part)
                if do_softmax:
                    p_ref[cur, h, rows, :] = jnp.exp2(s_ref[cur, h, rows, :] - m_new).astype(
                        jnp.bfloat16)
                if do_values and r % 2 == 1:
                    d = values_block(vext, other, h, r)
                    pv = d if pv is None else pv + d
            if do_scores:
                smax_ref[other, h] = jnp.max(blk_max, axis=0, keepdims=True)
            if do_values:
                acc_ref[h] = alpha_ref[other, h] * acc_ref[h] + pv

    @pl.when(use_fixed_shift)
    def _():
        l_ref[...] = jnp.zeros_like(l_ref)
        fixed_step(-1, 1, do_values=False)

        def steps(t, carry):
            for u in range(FIXED_UNROLL):
                fixed_step(FIXED_UNROLL * t + u, u % 2)
            return carry

        n_loop = (n_chunks - 2) // FIXED_UNROLL
        lax.fori_loop(0, n_loop, steps, 0)
        for j in range(n_loop * FIXED_UNROLL, n_chunks - 2):
            fixed_step(j, j % 2)
        fixed_step(n_chunks - 2, 0)
        fixed_step(n_chunks - 1, 1, do_scores=False)
        for h in range(HEADS_PER_KV):
            acc_ref[h, HEAD_DIM:HEAD_DIM + 1, :] = jnp.sum(l_ref[h], axis=0, keepdims=True)

    @pl.when(jnp.logical_not(use_fixed_shift))
    def _():
        m_ref[...] = jnp.full_like(m_ref, NEG_BIG)
        online_step(-1, 1, do_softmax=False, do_values=False)
        online_step(0, 0, do_values=False)

        def pair(t, carry):
            online_step(2 * t + 1, 1)
            online_step(2 * t + 2, 0)
            return carry

        lax.fori_loop(0, (n_chunks - 2) // 2, pair, 0)
        online_step(n_chunks - 1, 1, do_scores=False)
        online_step(n_chunks, 0, do_scores=False, do_softmax=False)

    for h in range(HEADS_PER_KV):
        a = acc_ref[h]
        o_ref[0, h * HEAD_DIM:(h + 1) * HEAD_DIM, :] = a[:HEAD_DIM] / a[HEAD_DIM:HEAD_DIM + 1]


def _global_attention(q_t, k, v_t, q_norm2, k_norm2):
    b, _, s = q_t.shape
    n_chunks = s // V_CHUNK
    assert n_chunks >= 2 and n_chunks % 2 == 0, "the kv pipeline runs chunks in pairs"
    return pl.pallas_call(
        functools.partial(_global_kernel, n_chunks=n_chunks),
        name="global_attn",
        grid=(b, KV_HEADS, s // Q_TILE),
        in_specs=[
            pl.BlockSpec((1, Q_PER_KV_ROWS, Q_TILE), lambda bi, g, i: (bi, g, i)),
            pl.BlockSpec((1, 1, s, HEAD_DIM), lambda bi, g, i: (bi, g, 0, 0)),
            pl.BlockSpec((1, n_chunks, HEAD_DIM, V_CHUNK), lambda bi, g, i: (bi, 0, g, 0)),
            pl.BlockSpec((1, HEADS_PER_KV, 1, Q_TILE), lambda bi, g, i: (bi, g, 0, i)),
            pl.BlockSpec((1, 1, 1, s), lambda bi, g, i: (bi, g, 0, 0)),
        ],
        out_specs=pl.BlockSpec((1, Q_PER_KV_ROWS, Q_TILE), lambda bi, g, i: (bi, g, i)),
        out_shape=jax.ShapeDtypeStruct((b, GROUP_WIDTH, s), jnp.float32),
        scratch_shapes=[
            pltpu.VMEM((2, HEADS_PER_KV, V_CHUNK, Q_TILE), jnp.float32),
            pltpu.VMEM((2, HEADS_PER_KV, 1, Q_TILE), jnp.float32),
            pltpu.VMEM((2, HEADS_PER_KV, V_CHUNK, Q_TILE), jnp.bfloat16),
            pltpu.VMEM((2, HEADS_PER_KV, 1, Q_TILE), jnp.float32),
            pltpu.VMEM((HEADS_PER_KV, 1, Q_TILE), jnp.float32),
            pltpu.VMEM((HEADS_PER_KV, 1, Q_TILE), jnp.float32),
            pltpu.VMEM((HEADS_PER_KV, 8, Q_TILE), jnp.float32),
            pltpu.VMEM((HEADS_PER_KV, HEAD_DIM + ONES_ROWS, Q_TILE), jnp.float32),
        ],
        compiler_params=pltpu.CompilerParams(
            dimension_semantics=("arbitrary", "arbitrary", "arbitrary"),
            vmem_limit_bytes=VMEM_LIMIT),
    )(q_t, k, v_t, q_norm2, k_norm2)


def _window_kernel(q_ref, kp_ref, kc_ref, kn_ref, vp_ref, vc_ref, vn_ref, bias_ref, sink_ref,
                   o_ref, s_ref, mx_ref, p_ref):
    i = pl.program_id(2)
    has_prev = i > 0
    has_next = i < pl.num_programs(2) - 1
    sink = sink_ref[0]
    ones = jnp.ones((ONES_ROWS, BAND), jnp.bfloat16)
    no_prev = jnp.where(has_prev, 0.0, NEG_BIG)
    no_next = jnp.where(has_next, 0.0, NEG_BIG)

    def band(sb, prev_ref, cur_ref, next_ref, axis):
        def cur(lo, hi):
            idx = (0, 0, slice(lo, hi), slice(None)) if axis == 0 else (0, 0, slice(None), slice(lo, hi))
            return cur_ref[idx]
        lo = (sb - 1) * WINDOW
        if sb == 0:
            parts = [prev_ref[0, 0], cur(0, 2 * WINDOW)]
        elif sb == N_SUB - 1:
            parts = [cur(lo, WQ_TILE), next_ref[0, 0]]
        else:
            return cur(lo, lo + BAND)
        return jnp.concatenate(parts, axis=axis)

    def scores(sb):
        lo = sb * WINDOW
        qs = jnp.concatenate(
            [q_ref[0, h * HEAD_DIM:(h + 1) * HEAD_DIM, lo:lo + WINDOW] for h in range(HEADS_PER_KV)],
            axis=1)
        ks = band(sb, kp_ref, kc_ref, kn_ref, 0)
        st = jnp.dot(ks, qs, preferred_element_type=jnp.float32) + bias_ref[0]
        if sb == 0:
            st = jnp.concatenate([st[:WINDOW] + no_prev, st[WINDOW:]], axis=0)
        if sb == N_SUB - 1:
            st = jnp.concatenate([st[:2 * WINDOW], st[2 * WINDOW:] + no_next], axis=0)
        s_ref[sb] = st
        mx_ref[sb] = jnp.maximum(jnp.max(st, axis=0, keepdims=True), sink)

    def probabilities(sb):
        p_ref[sb] = jnp.exp2(s_ref[sb] - mx_ref[sb]).astype(jnp.bfloat16)

    def values(sb):
        lo = sb * WINDOW
        vext = jnp.concatenate([band(sb, vp_ref, vc_ref, vn_ref, 1), ones], axis=0)
        ot = jnp.dot(vext, p_ref[sb], preferred_element_type=jnp.float32)
        denom = ot[HEAD_DIM:HEAD_DIM + 1] + jnp.exp2(sink - mx_ref[sb])
        ot = ot[:HEAD_DIM] / denom
        for h in range(HEADS_PER_KV):
            o_ref[0, h * HEAD_DIM:(h + 1) * HEAD_DIM, lo:lo + WINDOW] = (
                ot[:, h * WINDOW:(h + 1) * WINDOW])

    for t in range(N_SUB + 2):
        if t < N_SUB:
            scores(t)
        if 1 <= t <= N_SUB:
            probabilities(t - 1)
        if t >= 2:
            values(t - 2)


def _window_attention(q_t, k, v_t, bias, sink):
    b, _, s = q_t.shape
    n_blk = s // WINDOW
    per = WQ_TILE // WINDOW
    vper = V_CHUNK // WINDOW

    def prev_blk(i):
        return jnp.maximum(i * per - 1, 0)

    def next_blk(i):
        return jnp.minimum(i * per + per, n_blk - 1)

    kh = lambda g: KV_HEADS + g
    return pl.pallas_call(
        _window_kernel,
        name="window_attn",
        grid=(b, KV_HEADS, s // WQ_TILE),
        in_specs=[
            pl.BlockSpec((1, Q_PER_KV_ROWS, WQ_TILE), lambda bi, g, i: (bi, g, i)),
            pl.BlockSpec((1, 1, WINDOW, HEAD_DIM), lambda bi, g, i: (bi, kh(g), prev_blk(i), 0)),
            pl.BlockSpec((1, 1, WQ_TILE, HEAD_DIM), lambda bi, g, i: (bi, kh(g), i, 0)),
            pl.BlockSpec((1, 1, WINDOW, HEAD_DIM), lambda bi, g, i: (bi, kh(g), next_blk(i), 0)),
            pl.BlockSpec((1, 1, HEAD_DIM, WINDOW),
                         lambda bi, g, i: (bi, prev_blk(i) // vper, kh(g), prev_blk(i) % vper)),
            pl.BlockSpec((1, 1, HEAD_DIM, WQ_TILE), lambda bi, g, i: (bi, i, kh(g), 0)),
            pl.BlockSpec((1, 1, HEAD_DIM, WINDOW),
                         lambda bi, g, i: (bi, next_blk(i) // vper, kh(g), next_blk(i) % vper)),
            pl.BlockSpec((1, BAND, HEADS_PER_KV * WINDOW), lambda bi, g, i: (g, 0, 0)),
            pl.BlockSpec((1, 1, HEADS_PER_KV * WINDOW), lambda bi, g, i: (g, 0, 0)),
        ],
        out_specs=pl.BlockSpec((1, Q_PER_KV_ROWS, WQ_TILE), lambda bi, g, i: (bi, g, i)),
        out_shape=jax.ShapeDtypeStruct((b, GROUP_WIDTH, s), jnp.float32),
        scratch_shapes=[
            pltpu.VMEM((N_SUB, BAND, HEADS_PER_KV * WINDOW), jnp.float32),
            pltpu.VMEM((N_SUB, 1, HEADS_PER_KV * WINDOW), jnp.float32),
            pltpu.VMEM((N_SUB, BAND, HEADS_PER_KV * WINDOW), jnp.bfloat16),
        ],
        compiler_params=pltpu.CompilerParams(
            dimension_semantics=("arbitrary", "arbitrary", "arbitrary"),
            vmem_limit_bytes=VMEM_LIMIT),
    )(q_t, k, k, k, v_t, v_t, v_t, bias, sink)


def _mix_out_ffn_kernel(x_ref, oa_ref, ob_ref, ga_ref, gb_ref, w_ref, post_ref,
                        pre2_ref, post2_ref, wg_ref, wu_ref, wd_ref, o_ref, xn_ref, acc_ref):
    na = _rms_cols(oa_ref[0], ga_ref[...])
    nb = _rms_cols(ob_ref[0], gb_ref[...])
    mt = jnp.concatenate([na, nb], axis=0).astype(jnp.bfloat16)
    h = lax.dot_general(mt, w_ref[...], (((0,), (0,)), ((), ())),
                        preferred_element_type=jnp.float32)
    x1 = x_ref[0] + _rms_rows(h, post_ref[...])
    o_ref[0] = x1
    o_ref[0] += _ffn_branch(x1, pre2_ref, post2_ref, wg_ref, wu_ref, wd_ref, xn_ref, acc_ref)


def _mix_out_ffn(x, oa_t, ob_t, ga, gb, w_out, post, pre2, post2, wg, wu, wd):
    b, s, _ = x.shape
    vec = lambda bi, i: (0, 0)
    return pl.pallas_call(
        _mix_out_ffn_kernel,
        name="mix_out_ffn",
        grid=(b, s // TOKEN_TILE),
        in_specs=[
            pl.BlockSpec((1, TOKEN_TILE, D_MODEL), lambda bi, i: (bi, i, 0)),
            pl.BlockSpec((1, GROUP_WIDTH, TOKEN_TILE), lambda bi, i: (bi, 0, i)),
            pl.BlockSpec((1, GROUP_WIDTH, TOKEN_TILE), lambda bi, i: (bi, 0, i)),
            pl.BlockSpec((GROUP_WIDTH, 1), vec),
            pl.BlockSpec((GROUP_WIDTH, 1), vec),
            pl.BlockSpec((D_MODEL, D_MODEL), vec, pipeline_mode=RESIDENT),
            pl.BlockSpec((1, D_MODEL), vec),
            pl.BlockSpec((1, D_MODEL), vec),
            pl.BlockSpec((1, D_MODEL), vec),
            pl.BlockSpec((D_MODEL, D_FF), vec, pipeline_mode=RESIDENT),
            pl.BlockSpec((D_MODEL, D_FF), vec, pipeline_mode=RESIDENT),
            pl.BlockSpec((D_FF, D_MODEL), vec, pipeline_mode=RESIDENT),
        ],
        out_specs=pl.BlockSpec((1, TOKEN_TILE, D_MODEL), lambda bi, i: (bi, i, 0)),
        out_shape=jax.ShapeDtypeStruct((b, s, D_MODEL), jnp.float32),
        scratch_shapes=[
            pltpu.VMEM((TOKEN_TILE, D_MODEL), jnp.bfloat16),
            pltpu.VMEM((TOKEN_TILE, D_MODEL), jnp.float32),
        ],
        compiler_params=pltpu.CompilerParams(
            dimension_semantics=("arbitrary", "arbitrary"), vmem_limit_bytes=VMEM_LIMIT),
    )(x, oa_t, ob_t, ga, gb, w_out, post, pre2, post2, wg, wu, wd)


_DEINTERLEAVE = np.concatenate([np.arange(0, HEAD_DIM, 2), np.arange(1, HEAD_DIM, 2)])


def _w_in_rows():
    n_rot = (Q_HEADS + KV_HEADS) * HEAD_DIM
    rot = (np.arange(Q_HEADS + KV_HEADS)[:, None] * HEAD_DIM + _DEINTERLEAVE[None, :]).reshape(-1)
    return np.concatenate([rot, np.arange(n_rot, IN_WIDTH)])


def _rope_tables_t(seq_len):
    n_rows = seq_len // GRID_W
    row = jnp.repeat(jnp.arange(n_rows, dtype=jnp.float32), GRID_W)
    col = jnp.tile(jnp.arange(GRID_W, dtype=jnp.float32), n_rows)
    n_freq = HEAD_DIM // 4
    inv_freq = ROPE_THETA ** (-jnp.arange(n_freq, dtype=jnp.float32) / n_freq)
    ang = jnp.concatenate([inv_freq[:, None] * row[None, :], inv_freq[:, None] * col[None, :]], axis=0)
    return jnp.cos(ang), jnp.sin(ang)


def _window_bias():
    slopes = 2.0 ** (-8.0 * np.arange(1, Q_HEADS + 1, dtype=np.float64) / Q_HEADS)
    kpos = np.arange(BAND)[:, None] - WINDOW
    qpos = np.arange(WINDOW)[None, :]
    dist = np.abs(qpos - kpos).astype(np.float64)
    per_head = np.where(dist[None] <= WINDOW, -slopes[:, None, None] * dist[None] * LOG2E, NEG_BIG)
    per_kv = per_head.reshape(KV_HEADS, HEADS_PER_KV, BAND, WINDOW).transpose(0, 2, 1, 3)
    return jnp.asarray(per_kv.reshape(KV_HEADS, BAND, HEADS_PER_KV * WINDOW), jnp.float32)


def _trunk(x, layers, bias):
    b, s, _ = x.shape
    cos_t, sin_t = _rope_tables_t(s)
    for p in layers:
        x, qa_t, qb_t, k, v_t, q_norm2, k_norm2 = _ffn_mix_in(
            x, p["ffn1_pre"], p["ffn1_post"], p["ffn1_wg"], p["ffn1_wu"], p["ffn1_wd"],
            p["mix_pre"], p["w_in_t"], p["gq"], p["gk"], cos_t, sin_t)
        oa_t = _global_attention(qa_t, k, v_t, q_norm2, k_norm2)
        ob_t = _window_attention(qb_t, k, v_t, bias, p["sink"])
        x = _mix_out_ffn(x, oa_t, ob_t, p["ga"], p["gb"], p["w_out"], p["mix_post"],
                         p["ffn2_pre"], p["ffn2_post"], p["ffn2_wg"], p["ffn2_wu"], p["ffn2_wd"])
    return x


def kernel(x_prompt, x_sample, ffn1_pre, ffn1_post, ffn1_w_gate, ffn1_w_up, ffn1_w_down, mix_pre, mix_post, w_in, a_q_norm, a_k_norm, b_sink, a_out_norm, b_out_norm, w_out, ffn2_pre, ffn2_post, ffn2_w_gate, ffn2_w_up, ffn2_w_down):
    depth = w_in.shape[0]
    rows = _w_in_rows()
    layers = []
    for l in range(depth):
        bf16 = lambda w: w[l].astype(jnp.bfloat16)
        sink = jnp.repeat(b_sink[l].astype(jnp.float32) * LOG2E, WINDOW)
        layers.append(dict(
            ffn1_pre=ffn1_pre[l][None, :], ffn1_post=ffn1_post[l][None, :],
            ffn1_wg=bf16(ffn1_w_gate), ffn1_wu=bf16(ffn1_w_up), ffn1_wd=bf16(ffn1_w_down),
            ffn2_pre=ffn2_pre[l][None, :], ffn2_post=ffn2_post[l][None, :],
            ffn2_wg=bf16(ffn2_w_gate), ffn2_wu=bf16(ffn2_w_up), ffn2_wd=bf16(ffn2_w_down),
            mix_pre=mix_pre[l][None, :], mix_post=mix_post[l][None, :],
            w_in_t=w_in[l].T[rows].astype(jnp.bfloat16),
            gq=a_q_norm[l][_DEINTERLEAVE][:, None], gk=a_k_norm[l][_DEINTERLEAVE][:, None],
            sink=sink.reshape(KV_HEADS, 1, HEADS_PER_KV * WINDOW),
            ga=a_out_norm[l][:, None], gb=b_out_norm[l][:, None],
            w_out=w_out[l].astype(jnp.bfloat16),
        ))
    bias = _window_bias()
    return (_trunk(x_prompt, layers, bias), _trunk(x_sample, layers, bias))
```

```python
import functools
import math

import jax
import jax.numpy as jnp
import numpy as np
from jax import lax
from jax.experimental import pallas as pl
from jax.experimental.pallas import tpu as pltpu

D_MODEL = 1024
HEAD_DIM = 64
HALF_DIM = HEAD_DIM // 2
Q_HEADS = 8
KV_HEADS = 2
HEADS_PER_KV = Q_HEADS // KV_HEADS
GROUP_WIDTH = Q_HEADS * HEAD_DIM
KV_WIDTH = KV_HEADS * HEAD_DIM
Q_PER_KV_ROWS = HEADS_PER_KV * HEAD_DIM
GROUP_IN = GROUP_WIDTH + 2 * KV_WIDTH
IN_WIDTH = 2 * GROUP_IN
D_FF = 2816
GRID_W = 64
ROPE_THETA = 10000.0
WINDOW = 128
NORM_EPS = 1e-6
FFN_RESID = 0.5
LOG2E = math.log2(math.e)
Q_SCALE = HEAD_DIM ** -0.5 * LOG2E
NEG_BIG = -1e30
BOUND_SLACK = 1.01
MAX_FIXED_SHIFT = 48.0

FF_CHUNK = 256
N_FF_CHUNKS = D_FF // FF_CHUNK
TOKEN_TILE = 512
V_CHUNK = TOKEN_TILE
Q_TILE = 512
ONES_ROWS = 16
ROW_BLOCK = 128
FIXED_UNROLL = 10
WQ_TILE = 512
N_SUB = WQ_TILE // WINDOW
BAND = 3 * WINDOW

VMEM_LIMIT = 56 * 1024 * 1024
RESIDENT = pl.Buffered(1)


def _rms_rows(x, gain_row):
    ms = jnp.mean(x * x, axis=-1, keepdims=True)
    return x * lax.rsqrt(ms + NORM_EPS) * gain_row


def _rms_cols(xt, gain_col):
    ms = jnp.mean(xt * xt, axis=0, keepdims=True)
    return xt * lax.rsqrt(ms + NORM_EPS) * gain_col


def _zero_after(x):
    bits = lax.bitcast_convert_type(x, jnp.uint32)
    bits = lax.shift_right_logical(lax.shift_right_logical(bits, jnp.uint32(16)), jnp.uint32(16))
    return bits.astype(jnp.float32)


def _ffn_branch(x, pre_ref, post_ref, wg_ref, wu_ref, wd_ref, xn_ref, acc_ref):
    xn_ref[...] = _rms_rows(x, pre_ref[...]).astype(jnp.bfloat16)
    acc_ref[...] = jnp.zeros_like(acc_ref)

    for c in range(N_FF_CHUNKS):
        cols = slice(c * FF_CHUNK, (c + 1) * FF_CHUNK)
        g = jnp.dot(xn_ref[...], wg_ref[:, cols], preferred_element_type=jnp.float32)
        u = jnp.dot(xn_ref[...], wu_ref[:, cols], preferred_element_type=jnp.float32)
        h = (g * jax.nn.sigmoid(g) * u).astype(jnp.bfloat16)
        acc_ref[...] += jnp.dot(h, wd_ref[cols, :], preferred_element_type=jnp.float32)
    return FFN_RESID * _rms_rows(acc_ref[...], post_ref[...])


def _rope_norm_head(xt, gain_col, cos_t, sin_t):
    y = _rms_cols(xt, gain_col)
    x0 = y[:HALF_DIM]
    x1 = y[HALF_DIM:]
    return jnp.concatenate([x0 * cos_t - x1 * sin_t, x0 * sin_t + x1 * cos_t], axis=0)


def _ffn_mix_in_kernel(x_ref, pre1_ref, post1_ref, wg_ref, wu_ref, wd_ref,
                       pre_ref, w_ref, gq_ref, gk_ref, cos_ref, sin_ref,
                       x1_ref, qa_ref, qb_ref, k_ref, v_ref, qn_ref, kn_ref, xn_ref, acc_ref):
    x = x_ref[0]
    x1 = x + _ffn_branch(x, pre1_ref, post1_ref, wg_ref, wu_ref, wd_ref, xn_ref, acc_ref)
    x1_ref[0] = x1
    xn = _rms_rows(x1, pre_ref[...]).astype(jnp.bfloat16)
    pt = lax.dot_general(w_ref[...], xn, (((1,), (1,)), ((), ())),
                         preferred_element_type=jnp.float32)
    pa = pt[:GROUP_IN]
    pb = pt[GROUP_IN:]
    cos_t = cos_ref[...]
    sin_t = sin_ref[...]
    gq = gq_ref[...]
    gk = gk_ref[...]
    off = 0
    for h in range(Q_HEADS):
        r = _rope_norm_head(pa[off:off + HEAD_DIM], gq, cos_t, sin_t) * Q_SCALE
        qa_ref[0, h * HEAD_DIM:(h + 1) * HEAD_DIM, :] = r.astype(jnp.bfloat16)
        qn_ref[0, h] = jnp.sum(r * r, axis=0, keepdims=True)
        off += HEAD_DIM
    for h in range(KV_HEADS):
        r = _rope_norm_head(pa[off:off + HEAD_DIM], gk, cos_t, sin_t)
        k_ref[0, h] = r.T.astype(jnp.bfloat16)
        kn_ref[0, h] = jnp.sum(r * r, axis=0, keepdims=True)
        off += HEAD_DIM
    v_ref[0, 0, :KV_WIDTH, :] = pa[off:off + KV_WIDTH].astype(jnp.bfloat16)
    for h in range(Q_HEADS):
        r = pb[h * HEAD_DIM:(h + 1) * HEAD_DIM] * Q_SCALE
        qb_ref[0, h * HEAD_DIM:(h + 1) * HEAD_DIM, :] = r.astype(jnp.bfloat16)
        qn_ref[0, Q_HEADS + h] = jnp.sum(r * r, axis=0, keepdims=True)
    off = GROUP_WIDTH
    for h in range(KV_HEADS):
        r = pb[off:off + HEAD_DIM]
        k_ref[0, KV_HEADS + h] = r.T.astype(jnp.bfloat16)
        kn_ref[0, KV_HEADS + h] = jnp.sum(r * r, axis=0, keepdims=True)
        off += HEAD_DIM
    v_ref[0, 0, KV_WIDTH:, :] = pb[off:off + KV_WIDTH].astype(jnp.bfloat16)


def _ffn_mix_in(x, pre1, post1, wg, wu, wd, pre, w_in_t, gq, gk, cos_t, sin_t):
    b, s, _ = x.shape
    n_tiles = s // TOKEN_TILE
    vec = lambda bi, i: (0, 0)
    return pl.pallas_call(
        _ffn_mix_in_kernel,
        name="ffn_mix_in",
        grid=(b, n_tiles),
        in_specs=[
            pl.BlockSpec((1, TOKEN_TILE, D_MODEL), lambda bi, i: (bi, i, 0)),
            pl.BlockSpec((1, D_MODEL), vec),
            pl.BlockSpec((1, D_MODEL), vec),
            pl.BlockSpec((D_MODEL, D_FF), vec, pipeline_mode=RESIDENT),
            pl.BlockSpec((D_MODEL, D_FF), vec, pipeline_mode=RESIDENT),
            pl.BlockSpec((D_FF, D_MODEL), vec, pipeline_mode=RESIDENT),
            pl.BlockSpec((1, D_MODEL), vec),
            pl.BlockSpec((IN_WIDTH, D_MODEL), vec, pipeline_mode=RESIDENT),
            pl.BlockSpec((HEAD_DIM, 1), vec),
            pl.BlockSpec((HEAD_DIM, 1), vec),
            pl.BlockSpec((HALF_DIM, TOKEN_TILE), lambda bi, i: (0, i)),
            pl.BlockSpec((HALF_DIM, TOKEN_TILE), lambda bi, i: (0, i)),
        ],
        out_specs=[
            pl.BlockSpec((1, TOKEN_TILE, D_MODEL), lambda bi, i: (bi, i, 0)),
            pl.BlockSpec((1, GROUP_WIDTH, TOKEN_TILE), lambda bi, i: (bi, 0, i)),
            pl.BlockSpec((1, GROUP_WIDTH, TOKEN_TILE), lambda bi, i: (bi, 0, i)),
            pl.BlockSpec((1, 2 * KV_HEADS, TOKEN_TILE, HEAD_DIM), lambda bi, i: (bi, 0, i, 0)),
            pl.BlockSpec((1, 1, 2 * KV_WIDTH, V_CHUNK), lambda bi, i: (bi, i, 0, 0)),
            pl.BlockSpec((1, 2 * Q_HEADS, 1, TOKEN_TILE), lambda bi, i: (bi, 0, 0, i)),
            pl.BlockSpec((1, 2 * KV_HEADS, 1, TOKEN_TILE), lambda bi, i: (bi, 0, 0, i)),
        ],
        out_shape=[
            jax.ShapeDtypeStruct((b, s, D_MODEL), jnp.float32),
            jax.ShapeDtypeStruct((b, GROUP_WIDTH, s), jnp.bfloat16),
            jax.ShapeDtypeStruct((b, GROUP_WIDTH, s), jnp.bfloat16),
            jax.ShapeDtypeStruct((b, 2 * KV_HEADS, s, HEAD_DIM), jnp.bfloat16),
            jax.ShapeDtypeStruct((b, n_tiles, 2 * KV_WIDTH, V_CHUNK), jnp.bfloat16),
            jax.ShapeDtypeStruct((b, 2 * Q_HEADS, 1, s), jnp.float32),
            jax.ShapeDtypeStruct((b, 2 * KV_HEADS, 1, s), jnp.float32),
        ],
        scratch_shapes=[
            pltpu.VMEM((TOKEN_TILE, D_MODEL), jnp.bfloat16),
            pltpu.VMEM((TOKEN_TILE, D_MODEL), jnp.float32),
        ],
        compiler_params=pltpu.CompilerParams(
            dimension_semantics=("arbitrary", "arbitrary"), vmem_limit_bytes=VMEM_LIMIT),
    )(x, pre1, post1, wg, wu, wd, pre, w_in_t, gq, gk, cos_t, sin_t)


def _shift_stats(stat_ref, qn_all_ref, kn_ref):
    @pl.when(pl.program_id(2) == 0)
    def _():
        kmax2 = jnp.max(kn_ref[0, 0], axis=1, keepdims=True)
        qmax2 = jnp.max(jnp.max(qn_all_ref[0], axis=0), axis=1, keepdims=True)
        stat_ref[0] = jnp.max(kmax2)
        stat_ref[1] = jnp.max(BOUND_SLACK * jnp.sqrt(qmax2 * kmax2))
    return stat_ref[0], stat_ref[1] <= MAX_FIXED_SHIFT


def _global_kernel(q_ref, k_ref, v_ref, qn_ref, qn_all_ref, kn_ref, o_ref, s_ref, smax_ref, p_ref,
                   alpha_ref, m_ref, c_ref, l_ref, acc_ref, stat_ref, *, n_chunks):
    ones = jnp.ones((ONES_ROWS, V_CHUNK), jnp.bfloat16)
    n_blocks = V_CHUNK // ROW_BLOCK
    acc_ref[...] = jnp.zeros_like(acc_ref)

    kmax2, use_fixed_shift = _shift_stats(stat_ref, qn_all_ref, kn_ref)
    c_ref[...] = BOUND_SLACK * jnp.sqrt(qn_ref[0] * kmax2)

    def k_rows(chunk, r):
        start = chunk * V_CHUNK + r * ROW_BLOCK
        if not isinstance(start, int):
            start = pl.multiple_of(start, ROW_BLOCK)
        return k_ref[0, 0, pl.ds(start, ROW_BLOCK), :]

    def v_ext(chunk):
        return jnp.concatenate([v_ref[0, chunk], ones], axis=0)

    def values_block(vext, slot, h, r):
        cols = slice((r - 1) * ROW_BLOCK, (r + 1) * ROW_BLOCK)
        return jnp.dot(vext[:, cols], p_ref[slot, h, cols, :], preferred_element_type=jnp.float32)

    def fixed_step(j, parity, do_scores=True, do_values=True):
        cur, other = parity, 1 - parity
        if do_values:
            vt = v_ref[0, j]
        after_values = None
        for h in range(HEADS_PER_KV):
            q_h = q_ref[0, h * HEAD_DIM:(h + 1) * HEAD_DIM, :]
            c_h = c_ref[h]
            pv = None
            row_sum = None
            for r in range(n_blocks):
                rows = slice(r * ROW_BLOCK, (r + 1) * ROW_BLOCK)
                if do_scores:
                    st = jnp.dot(k_rows(j + 1, r), q_h, preferred_element_type=jnp.float32)
                    shift = c_h if after_values is None else c_h + after_values
                    p = jnp.exp2(st - shift)
                    p_ref[other, h, rows, :] = p.astype(jnp.bfloat16)
                    part = jnp.sum(p.reshape(ROW_BLOCK // 8, 8, Q_TILE), axis=0)
                    row_sum = part if row_sum is None else row_sum + part
                if do_values and r % 2 == 1:
                    d = values_block(vt, cur, h, r)
                    pv = d if pv is None else pv + d
                    after_values = _zero_after(d[:1])
            if do_scores:
                l_ref[h] += row_sum
            if do_values:
                acc_ref[h, :HEAD_DIM, :] += pv

    def online_step(j, parity, do_scores=True, do_softmax=True, do_values=True):
        cur, other = parity, 1 - parity
        if do_values:
            vext = v_ext(j - 1)
        for h in range(HEADS_PER_KV):
            q_h = q_ref[0, h * HEAD_DIM:(h + 1) * HEAD_DIM, :]
            if do_softmax:
                m_old = m_ref[h]
                m_new = jnp.maximum(m_old, smax_ref[cur, h])
                alpha_ref[cur, h] = jnp.exp2(m_old - m_new)
                m_ref[h] = m_new
            blk_max = None
            pv = None
            for r in range(n_blocks):
                rows = slice(r * ROW_BLOCK, (r + 1) * ROW_BLOCK)
                if do_scores:
                    st = jnp.dot(k_rows(j + 1, r), q_h, preferred_element_type=jnp.float32)
                    s_ref[other, h, rows, :] = st
                    part = jnp.max(st.reshape(ROW_BLOCK // 8, 8, Q_TILE), axis=0)
                    blk_max = part if blk_max is None else jnp.maximum(blk_max, part)
                if do_softmax:
                    p_ref[cur, h, rows, :] = jnp.exp2(s_ref[cur, h, rows, :] - m_new).astype(
                        jnp.bfloat16)
                if do_values and r % 2 == 1:
                    d = values_block(vext, other, h, r)
                    pv = d if pv is None else pv + d
            if do_scores:
                smax_ref[other, h] = jnp.max(blk_max, axis=0, keepdims=True)
            if do_values:
                acc_ref[h] = alpha_ref[other, h] * acc_ref[h] + pv

    @pl.when(use_fixed_shift)
    def _():
        l_ref[...] = jnp.zeros_like(l_ref)
        fixed_step(-1, 1, do_values=False)

        def steps(t, carry):
            for u in range(FIXED_UNROLL):
                fixed_step(FIXED_UNROLL * t + u, u % 2)
            return carry

        n_loop = (n_chunks - 2) // FIXED_UNROLL
        lax.fori_loop(0, n_loop, steps, 0)
        for j in range(n_loop * FIXED_UNROLL, n_chunks - 2):
            fixed_step(j, j % 2)
        fixed_step(n_chunks - 2, 0)
        fixed_step(n_chunks - 1, 1, do_scores=False)
        for h in range(HEADS_PER_KV):
            acc_ref[h, HEAD_DIM:HEAD_DIM + 1, :] = jnp.sum(l_ref[h], axis=0, keepdims=True)

    @pl.when(jnp.logical_not(use_fixed_shift))
    def _():
        m_ref[...] = jnp.full_like(m_ref, NEG_BIG)
        online_step(-1, 1, do_softmax=False, do_values=False)
        online_step(0, 0, do_values=False)

        def pair(t, carry):
            online_step(2 * t + 1, 1)
            online_step(2 * t + 2, 0)
            return carry

        lax.fori_loop(0, (n_chunks - 2) // 2, pair, 0)
        online_step(n_chunks - 1, 1, do_scores=False)
        online_step(n_chunks, 0, do_scores=False, do_softmax=False)

    for h in range(HEADS_PER_KV):
        a = acc_ref[h]
        o_ref[0, h * HEAD_DIM:(h + 1) * HEAD_DIM, :] = a[:HEAD_DIM] / a[HEAD_DIM:HEAD_DIM + 1]


def _global_attention(q_t, k, v_t, q_norm2, k_norm2):
    b, _, s = q_t.shape
    n_chunks = s // V_CHUNK
    assert n_chunks >= 2 and n_chunks % 2 == 0, "the kv pipeline runs chunks in pairs"
    return pl.pallas_call(
        functools.partial(_global_kernel, n_chunks=n_chunks),
        name="global_attn",
        grid=(b, KV_HEADS, s // Q_TILE),
        in_specs=[
            pl.BlockSpec((1, Q_PER_KV_ROWS, Q_TILE), lambda bi, g, i: (bi, g, i)),
            pl.BlockSpec((1, 1, s, HEAD_DIM), lambda bi, g, i: (bi, g, 0, 0)),
            pl.BlockSpec((1, n_chunks, HEAD_DIM, V_CHUNK), lambda bi, g, i: (bi, 0, g, 0)),
            pl.BlockSpec((1, HEADS_PER_KV, 1, Q_TILE), lambda bi, g, i: (bi, g, 0, i)),
            pl.BlockSpec((1, HEADS_PER_KV, 1, s), lambda bi, g, i: (bi, g, 0, 0)),
            pl.BlockSpec((1, 1, 1, s), lambda bi, g, i: (bi, g, 0, 0)),
        ],
        out_specs=pl.BlockSpec((1, Q_PER_KV_ROWS, Q_TILE), lambda bi, g, i: (bi, g, i)),
        out_shape=jax.ShapeDtypeStruct((b, GROUP_WIDTH, s), jnp.float32),
        scratch_shapes=[
            pltpu.VMEM((2, HEADS_PER_KV, V_CHUNK, Q_TILE), jnp.float32),
            pltpu.VMEM((2, HEADS_PER_KV, 1, Q_TILE), jnp.float32),
            pltpu.VMEM((2, HEADS_PER_KV, V_CHUNK, Q_TILE), jnp.bfloat16),
            pltpu.VMEM((2, HEADS_PER_KV, 1, Q_TILE), jnp.float32),
            pltpu.VMEM((HEADS_PER_KV, 1, Q_TILE), jnp.float32),
            pltpu.VMEM((HEADS_PER_KV, 1, Q_TILE), jnp.float32),
            pltpu.VMEM((HEADS_PER_KV, 8, Q_TILE), jnp.float32),
            pltpu.VMEM((HEADS_PER_KV, HEAD_DIM + ONES_ROWS, Q_TILE), jnp.float32),
            pltpu.SMEM((2,), jnp.float32),
        ],
        compiler_params=pltpu.CompilerParams(
            dimension_semantics=("arbitrary", "arbitrary", "arbitrary"),
            vmem_limit_bytes=VMEM_LIMIT),
    )(q_t, k, v_t, q_norm2, q_norm2, k_norm2)


def _window_kernel(q_ref, kp_ref, kc_ref, kn_ref, vp_ref, vc_ref, vn_ref, bias_ref, sink_ref,
                   qn2_ref, qn2_all_ref, kn2_ref, o_ref, s_ref, mx_ref, p_ref, stat_ref):
    i = pl.program_id(2)
    has_prev = i > 0
    has_next = i < pl.num_programs(2) - 1
    sink = sink_ref[0]
    no_prev = jnp.where(has_prev, 0.0, NEG_BIG)
    no_next = jnp.where(has_next, 0.0, NEG_BIG)
    blocks = WQ_TILE // WINDOW

    def key_block(blk, prev_ref, cur_ref, next_ref, axis):
        if blk < 0:
            return prev_ref[0, 0]
        if blk >= blocks:
            return next_ref[0, 0]
        lo = blk * WINDOW
        if axis == 0:
            return cur_ref[0, 0, lo:lo + WINDOW, :]
        return cur_ref[0, 0, :, lo:lo + WINDOW]

    def band(sb, prev_ref, cur_ref, next_ref, axis):
        return jnp.concatenate(
            [key_block(sb - 1 + r, prev_ref, cur_ref, next_ref, axis) for r in range(3)], axis=axis)

    def queries(sb):
        lo = sb * WINDOW
        return jnp.concatenate(
            [q_ref[0, h * HEAD_DIM:(h + 1) * HEAD_DIM, lo:lo + WINDOW] for h in range(HEADS_PER_KV)],
            axis=1)

    def edge_penalty(sb, r):
        if sb == 0 and r == 0:
            return no_prev
        if sb == N_SUB - 1 and r == 2:
            return no_next
        return None

    def store_out(sb, ot):
        lo = sb * WINDOW
        for h in range(HEADS_PER_KV):
            o_ref[0, h * HEAD_DIM:(h + 1) * HEAD_DIM, lo:lo + WINDOW] = (
                ot[:, h * WINDOW:(h + 1) * WINDOW])

    kmax2, use_fixed_shift = _shift_stats(stat_ref, qn2_all_ref, kn2_ref)
    c_all = BOUND_SLACK * jnp.sqrt(qn2_ref[0] * kmax2)

    @pl.when(use_fixed_shift)
    def _():
        def shift_of(sb):
            lo = sb * WINDOW
            c = jnp.concatenate([c_all[h, :, lo:lo + WINDOW] for h in range(HEADS_PER_KV)], axis=1)
            return jnp.maximum(c, sink)

        def values_part(sb, k_lo, k_hi):
            vs = jnp.concatenate(
                [key_block(sb - 1 + r, vp_ref, vc_ref, vn_ref, 1)
                 for r in range(k_lo // WINDOW, k_hi // WINDOW)], axis=1)
            return jnp.dot(vs, p_ref[sb, k_lo:k_hi, :], preferred_element_type=jnp.float32)

        def finish(sb, ot, row_sum, shift):
            denom = jnp.sum(row_sum, axis=0, keepdims=True) + jnp.exp2(sink - shift)
            store_out(sb, ot / denom)

        prev = None
        for sb in range(N_SUB + 1):
            scoring = sb < N_SUB
            if scoring:
                qs = queries(sb)
                shift = shift_of(sb)
                row_sum = None
            after_values = None
            ot = None
            for r in range(3):
                if scoring:
                    rows = slice(r * WINDOW, (r + 1) * WINDOW)
                    ks = key_block(sb - 1 + r, kp_ref, kc_ref, kn_ref, 0)
                    st = jnp.dot(ks, qs, preferred_element_type=jnp.float32) + bias_ref[0, rows, :]
                    pen = edge_penalty(sb, r)
                    sh = shift if pen is None else shift - pen
                    if after_values is not None:
                        sh = sh + after_values
                    p = jnp.exp2(st - sh)
                    p_ref[sb, rows, :] = p.astype(jnp.bfloat16)
                    part = jnp.sum(p.reshape(WINDOW // 8, 8, HEADS_PER_KV * WINDOW), axis=0)
                    row_sum = part if row_sum is None else row_sum + part
                if prev is not None and r < 2:
                    d = values_part(prev[0], 0, 2 * WINDOW) if r == 0 else values_part(
                        prev[0], 2 * WINDOW, BAND)
                    ot = d if ot is None else ot + d
                    after_values = _zero_after(d[:1])
            if prev is not None:
                finish(prev[0], ot, prev[1], prev[2])
            prev = (sb, row_sum, shift) if scoring else None

    @pl.when(jnp.logical_not(use_fixed_shift))
    def _():
        ones = jnp.ones((ONES_ROWS, BAND), jnp.bfloat16)

        def scores(sb):
            ks = band(sb, kp_ref, kc_ref, kn_ref, 0)
            st = jnp.dot(ks, queries(sb), preferred_element_type=jnp.float32) + bias_ref[0]
            if sb == 0:
                st = jnp.concatenate([st[:WINDOW] + no_prev, st[WINDOW:]], axis=0)
            if sb == N_SUB - 1:
                st = jnp.concatenate([st[:2 * WINDOW], st[2 * WINDOW:] + no_next], axis=0)
            s_ref[sb] = st
            mx_ref[sb] = jnp.maximum(jnp.max(st, axis=0, keepdims=True), sink)

        def probabilities(sb):
            p_ref[sb] = jnp.exp2(s_ref[sb] - mx_ref[sb]).astype(jnp.bfloat16)

        def values(sb):
            vext = jnp.concatenate([band(sb, vp_ref, vc_ref, vn_ref, 1), ones], axis=0)
            ot = jnp.dot(vext, p_ref[sb], preferred_element_type=jnp.float32)
            denom = ot[HEAD_DIM:HEAD_DIM + 1] + jnp.exp2(sink - mx_ref[sb])
            store_out(sb, ot[:HEAD_DIM] / denom)

        for t in range(N_SUB + 2):
            if t < N_SUB:
                scores(t)
            if 1 <= t <= N_SUB:
                probabilities(t - 1)
            if t >= 2:
                values(t - 2)


def _window_attention(q_t, k, v_t, bias, sink, q_norm2, k_norm2):
    b, _, s = q_t.shape
    n_blk = s // WINDOW
    per = WQ_TILE // WINDOW
    vper = V_CHUNK // WINDOW

    def prev_blk(i):
        return jnp.maximum(i * per - 1, 0)

    def next_blk(i):
        return jnp.minimum(i * per + per, n_blk - 1)

    kh = lambda g: KV_HEADS + g
    return pl.pallas_call(
        _window_kernel,
        name="window_attn",
        grid=(b, KV_HEADS, s // WQ_TILE),
        in_specs=[
            pl.BlockSpec((1, Q_PER_KV_ROWS, WQ_TILE), lambda bi, g, i: (bi, g, i)),
            pl.BlockSpec((1, 1, WINDOW, HEAD_DIM), lambda bi, g, i: (bi, kh(g), prev_blk(i), 0)),
            pl.BlockSpec((1, 1, WQ_TILE, HEAD_DIM), lambda bi, g, i: (bi, kh(g), i, 0)),
            pl.BlockSpec((1, 1, WINDOW, HEAD_DIM), lambda bi, g, i: (bi, kh(g), next_blk(i), 0)),
            pl.BlockSpec((1, 1, HEAD_DIM, WINDOW),
                         lambda bi, g, i: (bi, prev_blk(i) // vper, kh(g), prev_blk(i) % vper)),
            pl.BlockSpec((1, 1, HEAD_DIM, WQ_TILE), lambda bi, g, i: (bi, i, kh(g), 0)),
            pl.BlockSpec((1, 1, HEAD_DIM, WINDOW),
                         lambda bi, g, i: (bi, next_blk(i) // vper, kh(g), next_blk(i) % vper)),
            pl.BlockSpec((1, BAND, HEADS_PER_KV * WINDOW), lambda bi, g, i: (g, 0, 0)),
            pl.BlockSpec((1, 1, HEADS_PER_KV * WINDOW), lambda bi, g, i: (g, 0, 0)),
            pl.BlockSpec((1, HEADS_PER_KV, 1, WQ_TILE), lambda bi, g, i: (bi, kh(g), 0, i)),
            pl.BlockSpec((1, HEADS_PER_KV, 1, s), lambda bi, g, i: (bi, kh(g), 0, 0)),
            pl.BlockSpec((1, 1, 1, s), lambda bi, g, i: (bi, kh(g), 0, 0)),
        ],
        out_specs=pl.BlockSpec((1, Q_PER_KV_ROWS, WQ_TILE), lambda bi, g, i: (bi, g, i)),
        out_shape=jax.ShapeDtypeStruct((b, GROUP_WIDTH, s), jnp.float32),
        scratch_shapes=[
            pltpu.VMEM((N_SUB, BAND, HEADS_PER_KV * WINDOW), jnp.float32),
            pltpu.VMEM((N_SUB, 1, HEADS_PER_KV * WINDOW), jnp.float32),
            pltpu.VMEM((N_SUB, BAND, HEADS_PER_KV * WINDOW), jnp.bfloat16),
            pltpu.SMEM((2,), jnp.float32),
        ],
        compiler_params=pltpu.CompilerParams(
            dimension_semantics=("arbitrary", "arbitrary", "arbitrary"),
            vmem_limit_bytes=VMEM_LIMIT),
    )(q_t, k, k, k, v_t, v_t, v_t, bias, sink, q_norm2, q_norm2, k_norm2)


def _mix_out_ffn_kernel(x_ref, oa_ref, ob_ref, ga_ref, gb_ref, w_ref, post_ref,
                        pre2_ref, post2_ref, wg_ref, wu_ref, wd_ref, o_ref, xn_ref, acc_ref):
    na = _rms_cols(oa_ref[0], ga_ref[...])
    nb = _rms_cols(ob_ref[0], gb_ref[...])
    mt = jnp.concatenate([na, nb], axis=0).astype(jnp.bfloat16)
    h = lax.dot_general(mt, w_ref[...], (((0,), (0,)), ((), ())),
                        preferred_element_type=jnp.float32)
    x1 = x_ref[0] + _rms_rows(h, post_ref[...])
    o_ref[0] = x1
    o_ref[0] += _ffn_branch(x1, pre2_ref, post2_ref, wg_ref, wu_ref, wd_ref, xn_ref, acc_ref)


def _mix_out_ffn(x, oa_t, ob_t, ga, gb, w_out, post, pre2, post2, wg, wu, wd):
    b, s, _ = x.shape
    vec = lambda bi, i: (0, 0)
    return pl.pallas_call(
        _mix_out_ffn_kernel,
        name="mix_out_ffn",
        grid=(b, s // TOKEN_TILE),
        in_specs=[
            pl.BlockSpec((1, TOKEN_TILE, D_MODEL), lambda bi, i: (bi, i, 0)),
            pl.BlockSpec((1, GROUP_WIDTH, TOKEN_TILE), lambda bi, i: (bi, 0, i)),
            pl.BlockSpec((1, GROUP_WIDTH, TOKEN_TILE), lambda bi, i: (bi, 0, i)),
            pl.BlockSpec((GROUP_WIDTH, 1), vec),
            pl.BlockSpec((GROUP_WIDTH, 1), vec),
            pl.BlockSpec((D_MODEL, D_MODEL), vec, pipeline_mode=RESIDENT),
            pl.BlockSpec((1, D_MODEL), vec),
            pl.BlockSpec((1, D_MODEL), vec),
            pl.BlockSpec((1, D_MODEL), vec),
            pl.BlockSpec((D_MODEL, D_FF), vec, pipeline_mode=RESIDENT),
            pl.BlockSpec((D_MODEL, D_FF), vec, pipeline_mode=RESIDENT),
            pl.BlockSpec((D_FF, D_MODEL), vec, pipeline_mode=RESIDENT),
        ],
        out_specs=pl.BlockSpec((1, TOKEN_TILE, D_MODEL), lambda bi, i: (bi, i, 0)),
        out_shape=jax.ShapeDtypeStruct((b, s, D_MODEL), jnp.float32),
        scratch_shapes=[
            pltpu.VMEM((TOKEN_TILE, D_MODEL), jnp.bfloat16),
            pltpu.VMEM((TOKEN_TILE, D_MODEL), jnp.float32),
        ],
        compiler_params=pltpu.CompilerParams(
            dimension_semantics=("arbitrary", "arbitrary"), vmem_limit_bytes=VMEM_LIMIT),
    )(x, oa_t, ob_t, ga, gb, w_out, post, pre2, post2, wg, wu, wd)


_DEINTERLEAVE = np.concatenate([np.arange(0, HEAD_DIM, 2), np.arange(1, HEAD_DIM, 2)])


def _w_in_rows():
    n_rot = (Q_HEADS + KV_HEADS) * HEAD_DIM
    rot = (np.arange(Q_HEADS + KV_HEADS)[:, None] * HEAD_DIM + _DEINTERLEAVE[None, :]).reshape(-1)
    return np.concatenate([rot, np.arange(n_rot, IN_WIDTH)])


def _rope_tables_t(seq_len):
    n_rows = seq_len // GRID_W
    row = jnp.repeat(jnp.arange(n_rows, dtype=jnp.float32), GRID_W)
    col = jnp.tile(jnp.arange(GRID_W, dtype=jnp.float32), n_rows)
    n_freq = HEAD_DIM // 4
    inv_freq = ROPE_THETA ** (-jnp.arange(n_freq, dtype=jnp.float32) / n_freq)
    ang = jnp.concatenate([inv_freq[:, None] * row[None, :], inv_freq[:, None] * col[None, :]], axis=0)
    return jnp.cos(ang), jnp.sin(ang)


def _window_bias():
    slopes = 2.0 ** (-8.0 * np.arange(1, Q_HEADS + 1, dtype=np.float64) / Q_HEADS)
    kpos = np.arange(BAND)[:, None] - WINDOW
    qpos = np.arange(WINDOW)[None, :]
    dist = np.abs(qpos - kpos).astype(np.float64)
    per_head = np.where(dist[None] <= WINDOW, -slopes[:, None, None] * dist[None] * LOG2E, NEG_BIG)
    per_kv = per_head.reshape(KV_HEADS, HEADS_PER_KV, BAND, WINDOW).transpose(0, 2, 1, 3)
    return jnp.asarray(per_kv.reshape(KV_HEADS, BAND, HEADS_PER_KV * WINDOW), jnp.float32)


def _trunk(x, layers, bias):
    b, s, _ = x.shape
    cos_t, sin_t = _rope_tables_t(s)
    for p in layers:
        x, qa_t, qb_t, k, v_t, q_norm2, k_norm2 = _ffn_mix_in(
            x, p["ffn1_pre"], p["ffn1_post"], p["ffn1_wg"], p["ffn1_wu"], p["ffn1_wd"],
            p["mix_pre"], p["w_in_t"], p["gq"], p["gk"], cos_t, sin_t)
        oa_t = _global_attention(qa_t, k, v_t, q_norm2, k_norm2)
        ob_t = _window_attention(qb_t, k, v_t, bias, p["sink"], q_norm2, k_norm2)
        x = _mix_out_ffn(x, oa_t, ob_t, p["ga"], p["gb"], p["w_out"], p["mix_post"],
                         p["ffn2_pre"], p["ffn2_post"], p["ffn2_wg"], p["ffn2_wu"], p["ffn2_wd"])
    return x


def kernel(x_prompt, x_sample, ffn1_pre, ffn1_post, ffn1_w_gate, ffn1_w_up, ffn1_w_down, mix_pre, mix_post, w_in, a_q_norm, a_k_norm, b_sink, a_out_norm, b_out_norm, w_out, ffn2_pre, ffn2_post, ffn2_w_gate, ffn2_w_up, ffn2_w_down):
    depth = w_in.shape[0]
    rows = _w_in_rows()
    layers = []
    for l in range(depth):
        bf16 = lambda w: w[l].astype(jnp.bfloat16)
        sink = jnp.repeat(b_sink[l].astype(jnp.float32) * LOG2E, WINDOW)
        layers.append(dict(
            ffn1_pre=ffn1_pre[l][None, :], ffn1_post=ffn1_post[l][None, :],
            ffn1_wg=bf16(ffn1_w_gate), ffn1_wu=bf16(ffn1_w_up), ffn1_wd=bf16(ffn1_w_down),
            ffn2_pre=ffn2_pre[l][None, :], ffn2_post=ffn2_post[l][None, :],
            ffn2_wg=bf16(ffn2_w_gate), ffn2_wu=bf16(ffn2_w_up), ffn2_wd=bf16(ffn2_w_down),
            mix_pre=mix_pre[l][None, :], mix_post=mix_post[l][None, :],
            w_in_t=w_in[l].T[rows].astype(jnp.bfloat16),
            gq=a_q_norm[l][_DEINTERLEAVE][:, None], gk=a_k_norm[l][_DEINTERLEAVE][:, None],
            sink=sink.reshape(KV_HEADS, 1, HEADS_PER_KV * WINDOW),
            ga=a_out_norm[l][:, None], gb=b_out_norm[l][:, None],
            w_out=w_out[l].astype(jnp.bfloat16),
        ))
    bias = _window_bias()
    return (_trunk(x_prompt, layers, bias), _trunk(x_sample, layers, bias))
```

```python
import functools
import math

import jax
import jax.numpy as jnp
import numpy as np
from jax import lax
from jax.experimental import pallas as pl
from jax.experimental.pallas import tpu as pltpu

D_MODEL = 1024
HEAD_DIM = 64
HALF_DIM = HEAD_DIM // 2
Q_HEADS = 8
KV_HEADS = 2
HEADS_PER_KV = Q_HEADS // KV_HEADS
GROUP_WIDTH = Q_HEADS * HEAD_DIM
KV_WIDTH = KV_HEADS * HEAD_DIM
Q_PER_KV_ROWS = HEADS_PER_KV * HEAD_DIM
GROUP_IN = GROUP_WIDTH + 2 * KV_WIDTH
IN_WIDTH = 2 * GROUP_IN
D_FF = 2816
GRID_W = 64
ROPE_THETA = 10000.0
WINDOW = 128
NORM_EPS = 1e-6
FFN_RESID = 0.5
LOG2E = math.log2(math.e)
Q_SCALE = HEAD_DIM ** -0.5 * LOG2E
NEG_BIG = -1e30
BOUND_SLACK = 1.01
MAX_FIXED_SHIFT = 48.0

FF_CHUNK = 256
N_FF_CHUNKS = D_FF // FF_CHUNK
TOKEN_TILE = 512
V_CHUNK = TOKEN_TILE
Q_TILE = 512
ONES_ROWS = 16
ROW_BLOCK = 128
FIXED_UNROLL = 10
WQ_TILE = 512
N_SUB = WQ_TILE // WINDOW
BAND = 3 * WINDOW

VMEM_LIMIT = 56 * 1024 * 1024
RESIDENT = pl.Buffered(1)


def _rms_rows(x, gain_row):
    ms = jnp.mean(x * x, axis=-1, keepdims=True)
    return x * lax.rsqrt(ms + NORM_EPS) * gain_row


def _rms_cols(xt, gain_col):
    ms = jnp.mean(xt * xt, axis=0, keepdims=True)
    return xt * lax.rsqrt(ms + NORM_EPS) * gain_col


def _zero_after(x):
    bits = lax.bitcast_convert_type(x, jnp.uint32)
    bits = lax.shift_right_logical(lax.shift_right_logical(bits, jnp.uint32(16)), jnp.uint32(16))
    return bits.astype(jnp.float32)


def _ffn_branch(x, pre_ref, post_ref, wg_ref, wu_ref, wd_ref, xn_ref, acc_ref):
    xn_ref[...] = _rms_rows(x, pre_ref[...]).astype(jnp.bfloat16)
    acc_ref[...] = jnp.zeros_like(acc_ref)

    for c in range(N_FF_CHUNKS):
        cols = slice(c * FF_CHUNK, (c + 1) * FF_CHUNK)
        g = jnp.dot(xn_ref[...], wg_ref[:, cols], preferred_element_type=jnp.float32)
        u = jnp.dot(xn_ref[...], wu_ref[:, cols], preferred_element_type=jnp.float32)
        h = (g * jax.nn.sigmoid(g) * u).astype(jnp.bfloat16)
        acc_ref[...] += jnp.dot(h, wd_ref[cols, :], preferred_element_type=jnp.float32)
    return FFN_RESID * _rms_rows(acc_ref[...], post_ref[...])


def _rope_norm_head(xt, gain_col, cos_t, sin_t):
    y = _rms_cols(xt, gain_col)
    x0 = y[:HALF_DIM]
    x1 = y[HALF_DIM:]
    return jnp.concatenate([x0 * cos_t - x1 * sin_t, x0 * sin_t + x1 * cos_t], axis=0)


def _ffn_mix_in_kernel(x_ref, pre1_ref, post1_ref, wg_ref, wu_ref, wd_ref,
                       pre_ref, w_ref, gq_ref, gk_ref, cos_ref, sin_ref,
                       x1_ref, qa_ref, qb_ref, k_ref, v_ref, qn_ref, kn_ref, xn_ref, acc_ref):
    x = x_ref[0]
    x1 = x + _ffn_branch(x, pre1_ref, post1_ref, wg_ref, wu_ref, wd_ref, xn_ref, acc_ref)
    x1_ref[0] = x1
    xn = _rms_rows(x1, pre_ref[...]).astype(jnp.bfloat16)
    pt = lax.dot_general(w_ref[...], xn, (((1,), (1,)), ((), ())),
                         preferred_element_type=jnp.float32)
    pa = pt[:GROUP_IN]
    pb = pt[GROUP_IN:]
    cos_t = cos_ref[...]
    sin_t = sin_ref[...]
    gq = gq_ref[...]
    gk = gk_ref[...]
    off = 0
    for h in range(Q_HEADS):
        r = _rope_norm_head(pa[off:off + HEAD_DIM], gq, cos_t, sin_t) * Q_SCALE
        qa_ref[0, h * HEAD_DIM:(h + 1) * HEAD_DIM, :] = r.astype(jnp.bfloat16)
        qn_ref[0, h] = jnp.sum(r * r, axis=0, keepdims=True)
        off += HEAD_DIM
    for h in range(KV_HEADS):
        r = _rope_norm_head(pa[off:off + HEAD_DIM], gk, cos_t, sin_t)
        k_ref[0, h] = r.T.astype(jnp.bfloat16)
        kn_ref[0, h] = jnp.sum(r * r, axis=0, keepdims=True)
        off += HEAD_DIM
    v_ref[0, 0, :KV_WIDTH, :] = pa[off:off + KV_WIDTH].astype(jnp.bfloat16)
    for h in range(Q_HEADS):
        r = pb[h * HEAD_DIM:(h + 1) * HEAD_DIM] * Q_SCALE
        qb_ref[0, h * HEAD_DIM:(h + 1) * HEAD_DIM, :] = r.astype(jnp.bfloat16)
        qn_ref[0, Q_HEADS + h] = jnp.sum(r * r, axis=0, keepdims=True)
    off = GROUP_WIDTH
    for h in range(KV_HEADS):
        r = pb[off:off + HEAD_DIM]
        k_ref[0, KV_HEADS + h] = r.T.astype(jnp.bfloat16)
        kn_ref[0, KV_HEADS + h] = jnp.sum(r * r, axis=0, keepdims=True)
        off += HEAD_DIM
    v_ref[0, 0, KV_WIDTH:, :] = pb[off:off + KV_WIDTH].astype(jnp.bfloat16)


def _ffn_mix_in(x, pre1, post1, wg, wu, wd, pre, w_in_t, gq, gk, cos_t, sin_t):
    b, s, _ = x.shape
    n_tiles = s // TOKEN_TILE
    vec = lambda bi, i: (0, 0)
    return pl.pallas_call(
        _ffn_mix_in_kernel,
        name="ffn_mix_in",
        grid=(b, n_tiles),
        in_specs=[
            pl.BlockSpec((1, TOKEN_TILE, D_MODEL), lambda bi, i: (bi, i, 0)),
            pl.BlockSpec((1, D_MODEL), vec),
            pl.BlockSpec((1, D_MODEL), vec),
            pl.BlockSpec((D_MODEL, D_FF), vec, pipeline_mode=RESIDENT),
            pl.BlockSpec((D_MODEL, D_FF), vec, pipeline_mode=RESIDENT),
            pl.BlockSpec((D_FF, D_MODEL), vec, pipeline_mode=RESIDENT),
            pl.BlockSpec((1, D_MODEL), vec),
            pl.BlockSpec((IN_WIDTH, D_MODEL), vec, pipeline_mode=RESIDENT),
            pl.BlockSpec((HEAD_DIM, 1), vec),
            pl.BlockSpec((HEAD_DIM, 1), vec),
            pl.BlockSpec((HALF_DIM, TOKEN_TILE), lambda bi, i: (0, i)),
            pl.BlockSpec((HALF_DIM, TOKEN_TILE), lambda bi, i: (0, i)),
        ],
        out_specs=[
            pl.BlockSpec((1, TOKEN_TILE, D_MODEL), lambda bi, i: (bi, i, 0)),
            pl.BlockSpec((1, GROUP_WIDTH, TOKEN_TILE), lambda bi, i: (bi, 0, i)),
            pl.BlockSpec((1, GROUP_WIDTH, TOKEN_TILE), lambda bi, i: (bi, 0, i)),
            pl.BlockSpec((1, 2 * KV_HEADS, TOKEN_TILE, HEAD_DIM), lambda bi, i: (bi, 0, i, 0)),
            pl.BlockSpec((1, 1, 2 * KV_WIDTH, V_CHUNK), lambda bi, i: (bi, i, 0, 0)),
            pl.BlockSpec((1, 2 * Q_HEADS, 1, TOKEN_TILE), lambda bi, i: (bi, 0, 0, i)),
            pl.BlockSpec((1, 2 * KV_HEADS, 1, TOKEN_TILE), lambda bi, i: (bi, 0, 0, i)),
        ],
        out_shape=[
            jax.ShapeDtypeStruct((b, s, D_MODEL), jnp.float32),
            jax.ShapeDtypeStruct((b, GROUP_WIDTH, s), jnp.bfloat16),
            jax.ShapeDtypeStruct((b, GROUP_WIDTH, s), jnp.bfloat16),
            jax.ShapeDtypeStruct((b, 2 * KV_HEADS, s, HEAD_DIM), jnp.bfloat16),
            jax.ShapeDtypeStruct((b, n_tiles, 2 * KV_WIDTH, V_CHUNK), jnp.bfloat16),
            jax.ShapeDtypeStruct((b, 2 * Q_HEADS, 1, s), jnp.float32),
            jax.ShapeDtypeStruct((b, 2 * KV_HEADS, 1, s), jnp.float32),
        ],
        scratch_shapes=[
            pltpu.VMEM((TOKEN_TILE, D_MODEL), jnp.bfloat16),
            pltpu.VMEM((TOKEN_TILE, D_MODEL), jnp.float32),
        ],
        compiler_params=pltpu.CompilerParams(
            dimension_semantics=("arbitrary", "arbitrary"), vmem_limit_bytes=VMEM_LIMIT),
    )(x, pre1, post1, wg, wu, wd, pre, w_in_t, gq, gk, cos_t, sin_t)


def _shift_stats(stat_ref, qn_all_ref, kn_ref):
    @pl.when(pl.program_id(2) == 0)
    def _():
        kmax2 = jnp.max(kn_ref[0, 0], axis=1, keepdims=True)
        qmax2 = jnp.max(jnp.max(qn_all_ref[0], axis=0), axis=1, keepdims=True)
        stat_ref[0] = jnp.max(kmax2)
        stat_ref[1] = jnp.max(BOUND_SLACK * jnp.sqrt(qmax2 * kmax2))
    return stat_ref[0], stat_ref[1] <= MAX_FIXED_SHIFT


def _global_kernel(q_ref, k_ref, v_ref, qn_ref, qn_all_ref, kn_ref, o_ref, s_ref, smax_ref, p_ref,
                   alpha_ref, m_ref, c_ref, l_ref, acc_ref, stat_ref, *, n_chunks):
    ones = jnp.ones((ONES_ROWS, V_CHUNK), jnp.bfloat16)
    n_blocks = V_CHUNK // ROW_BLOCK
    acc_ref[...] = jnp.zeros_like(acc_ref)

    kmax2, use_fixed_shift = _shift_stats(stat_ref, qn_all_ref, kn_ref)
    c_ref[...] = BOUND_SLACK * jnp.sqrt(qn_ref[0] * kmax2)

    def k_rows(chunk, r):
        start = chunk * V_CHUNK + r * ROW_BLOCK
        if not isinstance(start, int):
            start = pl.multiple_of(start, ROW_BLOCK)
        return k_ref[0, 0, pl.ds(start, ROW_BLOCK), :]

    def v_ext(chunk):
        return jnp.concatenate([v_ref[0, chunk], ones], axis=0)

    def values_block(vext, slot, h, r):
        cols = slice((r - 1) * ROW_BLOCK, (r + 1) * ROW_BLOCK)
        return jnp.dot(vext[:, cols], p_ref[slot, h, cols, :], preferred_element_type=jnp.float32)

    def fixed_step(j, parity, do_scores=True, do_values=True):
        cur, other = parity, 1 - parity
        if do_values:
            vt = v_ref[0, j]
        after_values = None
        for h in range(HEADS_PER_KV):
            q_h = q_ref[0, h * HEAD_DIM:(h + 1) * HEAD_DIM, :]
            c_h = c_ref[h]
            pv = None
            row_sum = None
            for r in range(n_blocks):
                rows = slice(r * ROW_BLOCK, (r + 1) * ROW_BLOCK)
                if do_scores:
                    st = jnp.dot(k_rows(j + 1, r), q_h, preferred_element_type=jnp.float32)
                    shift = c_h if after_values is None else c_h + after_values
                    p = jnp.exp2(st - shift)
                    p_ref[other, h, rows, :] = p.astype(jnp.bfloat16)
                    part = jnp.sum(p.reshape(ROW_BLOCK // 8, 8, Q_TILE), axis=0)
                    row_sum = part if row_sum is None else row_sum + part
                if do_values and r % 2 == 1:
                    d = values_block(vt, cur, h, r)
                    pv = d if pv is None else pv + d
                    after_values = _zero_after(d[:1])
            if do_scores:
                l_ref[h] += row_sum
            if do_values:
                acc_ref[h, :HEAD_DIM, :] += pv

    def online_step(j, parity, do_scores=True, do_softmax=True, do_values=True):
        cur, other = parity, 1 - parity
        if do_values:
            vext = v_ext(j - 1)
        for h in range(HEADS_PER_KV):
            q_h = q_ref[0, h * HEAD_DIM:(h + 1) * HEAD_DIM, :]
            if do_softmax:
                m_old = m_ref[h]
                m_new = jnp.maximum(m_old, smax_ref[cur, h])
                alpha_ref[cur, h] = jnp.exp2(m_old - m_new)
                m_ref[h] = m_new
            blk_max = None
            pv = None
            for r in range(n_blocks):
                rows = slice(r * ROW_BLOCK, (r + 1) * ROW_BLOCK)
                if do_scores:
                    st = jnp.dot(k_rows(j + 1, r), q_h, preferred_element_type=jnp.float32)
                    s_ref[other, h, rows, :] = st
                    part = jnp.max(st.reshape(ROW_BLOCK // 8, 8, Q_TILE), axis=0)
                    blk_max = part if blk_max is None else jnp.maximum(blk_max, part)
                if do_softmax:
                    p_ref[cur, h, rows, :] = jnp.exp2(s_ref[cur, h, rows, :] - m_new).astype(
                        jnp.bfloat16)
                if do_values and r % 2 == 1:
                    d = values_block(vext, other, h, r)
                    pv = d if pv is None else pv + d
            if do_scores:
                smax_ref[other, h] = jnp.max(blk_max, axis=0, keepdims=True)
            if do_values:
                acc_ref[h] = alpha_ref[other, h] * acc_ref[h] + pv

    @pl.when(use_fixed_shift)
    def _():
        l_ref[...] = jnp.zeros_like(l_ref)
        fixed_step(-1, 1, do_values=False)

        def steps(t, carry):
            for u in range(FIXED_UNROLL):
                fixed_step(FIXED_UNROLL * t + u, u % 2)
            return carry

        n_loop = (n_chunks - 2) // FIXED_UNROLL
        lax.fori_loop(0, n_loop, steps, 0)
        for j in range(n_loop * FIXED_UNROLL, n_chunks - 2):
            fixed_step(j, j % 2)
        fixed_step(n_chunks - 2, 0)
        fixed_step(n_chunks - 1, 1, do_scores=False)
        for h in range(HEADS_PER_KV):
            acc_ref[h, HEAD_DIM:HEAD_DIM + 1, :] = jnp.sum(l_ref[h], axis=0, keepdims=True)

    @pl.when(jnp.logical_not(use_fixed_shift))
    def _():
        m_ref[...] = jnp.full_like(m_ref, NEG_BIG)
        online_step(-1, 1, do_softmax=False, do_values=False)
        online_step(0, 0, do_values=False)

        def pair(t, carry):
            online_step(2 * t + 1, 1)
            online_step(2 * t + 2, 0)
            return carry

        lax.fori_loop(0, (n_chunks - 2) // 2, pair, 0)
        online_step(n_chunks - 1, 1, do_scores=False)
        online_step(n_chunks, 0, do_scores=False, do_softmax=False)

    for h in range(HEADS_PER_KV):
        a = acc_ref[h]
        o_ref[0, h * HEAD_DIM:(h + 1) * HEAD_DIM, :] = a[:HEAD_DIM] / a[HEAD_DIM:HEAD_DIM + 1]


def _global_attention(q_t, k, v_t, q_norm2, k_norm2):
    b, _, s = q_t.shape
    n_chunks = s // V_CHUNK
    assert n_chunks >= 2 and n_chunks % 2 == 0, "the kv pipeline runs chunks in pairs"
    return pl.pallas_call(
        functools.partial(_global_kernel, n_chunks=n_chunks),
        name="global_attn",
        grid=(b, KV_HEADS, s // Q_TILE),
        in_specs=[
            pl.BlockSpec((1, Q_PER_KV_ROWS, Q_TILE), lambda bi, g, i: (bi, g, i)),
            pl.BlockSpec((1, 1, s, HEAD_DIM), lambda bi, g, i: (bi, g, 0, 0)),
            pl.BlockSpec((1, n_chunks, HEAD_DIM, V_CHUNK), lambda bi, g, i: (bi, 0, g, 0)),
            pl.BlockSpec((1, HEADS_PER_KV, 1, Q_TILE), lambda bi, g, i: (bi, g, 0, i)),
            pl.BlockSpec((1, HEADS_PER_KV, 1, s), lambda bi, g, i: (bi, g, 0, 0)),
            pl.BlockSpec((1, 1, 1, s), lambda bi, g, i: (bi, g, 0, 0)),
        ],
        out_specs=pl.BlockSpec((1, Q_PER_KV_ROWS, Q_TILE), lambda bi, g, i: (bi, g, i)),
        out_shape=jax.ShapeDtypeStruct((b, GROUP_WIDTH, s), jnp.float32),
        scratch_shapes=[
            pltpu.VMEM((2, HEADS_PER_KV, V_CHUNK, Q_TILE), jnp.float32),
            pltpu.VMEM((2, HEADS_PER_KV, 1, Q_TILE), jnp.float32),
            pltpu.VMEM((2, HEADS_PER_KV, V_CHUNK, Q_TILE), jnp.bfloat16),
            pltpu.VMEM((2, HEADS_PER_KV, 1, Q_TILE), jnp.float32),
            pltpu.VMEM((HEADS_PER_KV, 1, Q_TILE), jnp.float32),
            pltpu.VMEM((HEADS_PER_KV, 1, Q_TILE), jnp.float32),
            pltpu.VMEM((HEADS_PER_KV, 8, Q_TILE), jnp.float32),
            pltpu.VMEM((HEADS_PER_KV, HEAD_DIM + ONES_ROWS, Q_TILE), jnp.float32),
            pltpu.SMEM((2,), jnp.float32),
        ],
        compiler_params=pltpu.CompilerParams(
            dimension_semantics=("arbitrary", "arbitrary", "arbitrary"),
            vmem_limit_bytes=VMEM_LIMIT),
    )(q_t, k, v_t, q_norm2, q_norm2, k_norm2)


def _window_kernel(q_ref, k_ref, v_ref, bias_ref, sink_ref, qn2_ref, qn2_all_ref, kn2_ref,
                   o_ref, s_ref, mx_ref, p_ref, stat_ref, *, seq_len):
    i = pl.program_id(2)
    has_prev = i > 0
    has_next = i < pl.num_programs(2) - 1
    sink = sink_ref[0]
    no_prev = jnp.where(has_prev, 0.0, NEG_BIG)
    no_next = jnp.where(has_next, 0.0, NEG_BIG)
    blocks = WQ_TILE // WINDOW
    last_chunk = seq_len // V_CHUNK - 1

    def k_block(blk):
        start = jnp.clip(i * WQ_TILE + blk * WINDOW, 0, seq_len - WINDOW)
        return k_ref[0, 0, pl.ds(pl.multiple_of(start, WINDOW), WINDOW), :]

    def v_block(blk):
        if blk < 0:
            return v_ref[0, jnp.maximum(i - 1, 0), :, V_CHUNK - WINDOW:]
        if blk >= blocks:
            return v_ref[0, jnp.minimum(i + 1, last_chunk), :, :WINDOW]
        return v_ref[0, i, :, blk * WINDOW:(blk + 1) * WINDOW]

    def band(sb, block_fn, axis):
        return jnp.concatenate([block_fn(sb - 1 + r) for r in range(3)], axis=axis)

    def queries(sb):
        lo = sb * WINDOW
        return jnp.concatenate(
            [q_ref[0, h * HEAD_DIM:(h + 1) * HEAD_DIM, lo:lo + WINDOW] for h in range(HEADS_PER_KV)],
            axis=1)

    def edge_penalty(sb, r):
        if sb == 0 and r == 0:
            return no_prev
        if sb == N_SUB - 1 and r == 2:
            return no_next
        return None

    def store_out(sb, ot):
        lo = sb * WINDOW
        for h in range(HEADS_PER_KV):
            o_ref[0, h * HEAD_DIM:(h + 1) * HEAD_DIM, lo:lo + WINDOW] = (
                ot[:, h * WINDOW:(h + 1) * WINDOW])

    kmax2, use_fixed_shift = _shift_stats(stat_ref, qn2_all_ref, kn2_ref)
    c_all = BOUND_SLACK * jnp.sqrt(qn2_ref[0] * kmax2)

    @pl.when(use_fixed_shift)
    def _():
        def shift_of(sb):
            lo = sb * WINDOW
            c = jnp.concatenate([c_all[h, :, lo:lo + WINDOW] for h in range(HEADS_PER_KV)], axis=1)
            return jnp.maximum(c, sink)

        def values_part(sb, k_lo, k_hi):
            vs = jnp.concatenate(
                [v_block(sb - 1 + r) for r in range(k_lo // WINDOW, k_hi // WINDOW)], axis=1)
            return jnp.dot(vs, p_ref[sb, k_lo:k_hi, :], preferred_element_type=jnp.float32)

        def finish(sb, ot, row_sum, shift):
            denom = jnp.sum(row_sum, axis=0, keepdims=True) + jnp.exp2(sink - shift)
            store_out(sb, ot / denom)

        prev = None
        for sb in range(N_SUB + 1):
            scoring = sb < N_SUB
            if scoring:
                qs = queries(sb)
                shift = shift_of(sb)
                row_sum = None
            after_values = None
            ot = None
            for r in range(3):
                if scoring:
                    rows = slice(r * WINDOW, (r + 1) * WINDOW)
                    ks = k_block(sb - 1 + r)
                    st = jnp.dot(ks, qs, preferred_element_type=jnp.float32) + bias_ref[0, rows, :]
                    pen = edge_penalty(sb, r)
                    sh = shift if pen is None else shift - pen
                    if after_values is not None:
                        sh = sh + after_values
                    p = jnp.exp2(st - sh)
                    p_ref[sb, rows, :] = p.astype(jnp.bfloat16)
                    part = jnp.sum(p.reshape(WINDOW // 8, 8, HEADS_PER_KV * WINDOW), axis=0)
                    row_sum = part if row_sum is None else row_sum + part
                if prev is not None and r < 2:
                    d = values_part(prev[0], 0, 2 * WINDOW) if r == 0 else values_part(
                        prev[0], 2 * WINDOW, BAND)
                    ot = d if ot is None else ot + d
                    after_values = _zero_after(d[:1])
            if prev is not None:
                finish(prev[0], ot, prev[1], prev[2])
            prev = (sb, row_sum, shift) if scoring else None

    @pl.when(jnp.logical_not(use_fixed_shift))
    def _():
        ones = jnp.ones((ONES_ROWS, BAND), jnp.bfloat16)

        def scores(sb):
            ks = band(sb, k_block, 0)
            st = jnp.dot(ks, queries(sb), preferred_element_type=jnp.float32) + bias_ref[0]
            if sb == 0:
                st = jnp.concatenate([st[:WINDOW] + no_prev, st[WINDOW:]], axis=0)
            if sb == N_SUB - 1:
                st = jnp.concatenate([st[:2 * WINDOW], st[2 * WINDOW:] + no_next], axis=0)
            s_ref[sb] = st
            mx_ref[sb] = jnp.maximum(jnp.max(st, axis=0, keepdims=True), sink)

        def probabilities(sb):
            p_ref[sb] = jnp.exp2(s_ref[sb] - mx_ref[sb]).astype(jnp.bfloat16)

        def values(sb):
            vext = jnp.concatenate([band(sb, v_block, 1), ones], axis=0)
            ot = jnp.dot(vext, p_ref[sb], preferred_element_type=jnp.float32)
            denom = ot[HEAD_DIM:HEAD_DIM + 1] + jnp.exp2(sink - mx_ref[sb])
            store_out(sb, ot[:HEAD_DIM] / denom)

        for t in range(N_SUB + 2):
            if t < N_SUB:
                scores(t)
            if 1 <= t <= N_SUB:
                probabilities(t - 1)
            if t >= 2:
                values(t - 2)


def _window_attention(q_t, k, v_t, bias, sink, q_norm2, k_norm2):
    b, _, s = q_t.shape
    assert WQ_TILE == V_CHUNK, "a query tile's own keys are one v^T chunk"
    n_chunks = s // V_CHUNK
    kh = lambda g: KV_HEADS + g
    return pl.pallas_call(
        functools.partial(_window_kernel, seq_len=s),
        name="window_attn",
        grid=(b, KV_HEADS, s // WQ_TILE),
        in_specs=[
            pl.BlockSpec((1, Q_PER_KV_ROWS, WQ_TILE), lambda bi, g, i: (bi, g, i)),
            pl.BlockSpec((1, 1, s, HEAD_DIM), lambda bi, g, i: (bi, kh(g), 0, 0)),
            pl.BlockSpec((1, n_chunks, HEAD_DIM, V_CHUNK), lambda bi, g, i: (bi, 0, kh(g), 0)),
            pl.BlockSpec((1, BAND, HEADS_PER_KV * WINDOW), lambda bi, g, i: (g, 0, 0)),
            pl.BlockSpec((1, 1, HEADS_PER_KV * WINDOW), lambda bi, g, i: (g, 0, 0)),
            pl.BlockSpec((1, HEADS_PER_KV, 1, WQ_TILE), lambda bi, g, i: (bi, kh(g), 0, i)),
            pl.BlockSpec((1, HEADS_PER_KV, 1, s), lambda bi, g, i: (bi, kh(g), 0, 0)),
            pl.BlockSpec((1, 1, 1, s), lambda bi, g, i: (bi, kh(g), 0, 0)),
        ],
        out_specs=pl.BlockSpec((1, Q_PER_KV_ROWS, WQ_TILE), lambda bi, g, i: (bi, g, i)),
        out_shape=jax.ShapeDtypeStruct((b, GROUP_WIDTH, s), jnp.float32),
        scratch_shapes=[
            pltpu.VMEM((N_SUB, BAND, HEADS_PER_KV * WINDOW), jnp.float32),
            pltpu.VMEM((N_SUB, 1, HEADS_PER_KV * WINDOW), jnp.float32),
            pltpu.VMEM((N_SUB, BAND, HEADS_PER_KV * WINDOW), jnp.bfloat16),
            pltpu.SMEM((2,), jnp.float32),
        ],
        compiler_params=pltpu.CompilerParams(
            dimension_semantics=("arbitrary", "arbitrary", "arbitrary"),
            vmem_limit_bytes=VMEM_LIMIT),
    )(q_t, k, v_t, bias, sink, q_norm2, q_norm2, k_norm2)


def _mix_out_ffn_kernel(x_ref, oa_ref, ob_ref, ga_ref, gb_ref, w_ref, post_ref,
                        pre2_ref, post2_ref, wg_ref, wu_ref, wd_ref, o_ref, xn_ref, acc_ref):
    na = _rms_cols(oa_ref[0], ga_ref[...])
    nb = _rms_cols(ob_ref[0], gb_ref[...])
    mt = jnp.concatenate([na, nb], axis=0).astype(jnp.bfloat16)
    h = lax.dot_general(mt, w_ref[...], (((0,), (0,)), ((), ())),
                        preferred_element_type=jnp.float32)
    x1 = x_ref[0] + _rms_rows(h, post_ref[...])
    o_ref[0] = x1
    o_ref[0] += _ffn_branch(x1, pre2_ref, post2_ref, wg_ref, wu_ref, wd_ref, xn_ref, acc_ref)


def _mix_out_ffn(x, oa_t, ob_t, ga, gb, w_out, post, pre2, post2, wg, wu, wd):
    b, s, _ = x.shape
    vec = lambda bi, i: (0, 0)
    return pl.pallas_call(
        _mix_out_ffn_kernel,
        name="mix_out_ffn",
        grid=(b, s // TOKEN_TILE),
        in_specs=[
            pl.BlockSpec((1, TOKEN_TILE, D_MODEL), lambda bi, i: (bi, i, 0)),
            pl.BlockSpec((1, GROUP_WIDTH, TOKEN_TILE), lambda bi, i: (bi, 0, i)),
            pl.BlockSpec((1, GROUP_WIDTH, TOKEN_TILE), lambda bi, i: (bi, 0, i)),
            pl.BlockSpec((GROUP_WIDTH, 1), vec),
            pl.BlockSpec((GROUP_WIDTH, 1), vec),
            pl.BlockSpec((D_MODEL, D_MODEL), vec, pipeline_mode=RESIDENT),
            pl.BlockSpec((1, D_MODEL), vec),
            pl.BlockSpec((1, D_MODEL), vec),
            pl.BlockSpec((1, D_MODEL), vec),
            pl.BlockSpec((D_MODEL, D_FF), vec, pipeline_mode=RESIDENT),
            pl.BlockSpec((D_MODEL, D_FF), vec, pipeline_mode=RESIDENT),
            pl.BlockSpec((D_FF, D_MODEL), vec, pipeline_mode=RESIDENT),
        ],
        out_specs=pl.BlockSpec((1, TOKEN_TILE, D_MODEL), lambda bi, i: (bi, i, 0)),
        out_shape=jax.ShapeDtypeStruct((b, s, D_MODEL), jnp.float32),
        scratch_shapes=[
            pltpu.VMEM((TOKEN_TILE, D_MODEL), jnp.bfloat16),
            pltpu.VMEM((TOKEN_TILE, D_MODEL), jnp.float32),
        ],
        compiler_params=pltpu.CompilerParams(
            dimension_semantics=("arbitrary", "arbitrary"), vmem_limit_bytes=VMEM_LIMIT),
    )(x, oa_t, ob_t, ga, gb, w_out, post, pre2, post2, wg, wu, wd)


_DEINTERLEAVE = np.concatenate([np.arange(0, HEAD_DIM, 2), np.arange(1, HEAD_DIM, 2)])


def _w_in_rows():
    n_rot = (Q_HEADS + KV_HEADS) * HEAD_DIM
    rot = (np.arange(Q_HEADS + KV_HEADS)[:, None] * HEAD_DIM + _DEINTERLEAVE[None, :]).reshape(-1)
    return np.concatenate([rot, np.arange(n_rot, IN_WIDTH)])


def _rope_tables_t(seq_len):
    n_rows = seq_len // GRID_W
    row = jnp.repeat(jnp.arange(n_rows, dtype=jnp.float32), GRID_W)
    col = jnp.tile(jnp.arange(GRID_W, dtype=jnp.float32), n_rows)
    n_freq = HEAD_DIM // 4
    inv_freq = ROPE_THETA ** (-jnp.arange(n_freq, dtype=jnp.float32) / n_freq)
    ang = jnp.concatenate([inv_freq[:, None] * row[None, :], inv_freq[:, None] * col[None, :]], axis=0)
    return jnp.cos(ang), jnp.sin(ang)


def _window_bias():
    slopes = 2.0 ** (-8.0 * np.arange(1, Q_HEADS + 1, dtype=np.float64) / Q_HEADS)
    kpos = np.arange(BAND)[:, None] - WINDOW
    qpos = np.arange(WINDOW)[None, :]
    dist = np.abs(qpos - kpos).astype(np.float64)
    per_head = np.where(dist[None] <= WINDOW, -slopes[:, None, None] * dist[None] * LOG2E, NEG_BIG)
    per_kv = per_head.reshape(KV_HEADS, HEADS_PER_KV, BAND, WINDOW).transpose(0, 2, 1, 3)
    return jnp.asarray(per_kv.reshape(KV_HEADS, BAND, HEADS_PER_KV * WINDOW), jnp.float32)


def _trunk(x, layers, bias):
    b, s, _ = x.shape
    cos_t, sin_t = _rope_tables_t(s)
    for p in layers:
        x, qa_t, qb_t, k, v_t, q_norm2, k_norm2 = _ffn_mix_in(
            x, p["ffn1_pre"], p["ffn1_post"], p["ffn1_wg"], p["ffn1_wu"], p["ffn1_wd"],
            p["mix_pre"], p["w_in_t"], p["gq"], p["gk"], cos_t, sin_t)
        oa_t = _global_attention(qa_t, k, v_t, q_norm2, k_norm2)
        ob_t = _window_attention(qb_t, k, v_t, bias, p["sink"], q_norm2, k_norm2)
        x = _mix_out_ffn(x, oa_t, ob_t, p["ga"], p["gb"], p["w_out"], p["mix_post"],
                         p["ffn2_pre"], p["ffn2_post"], p["ffn2_wg"], p["ffn2_wu"], p["ffn2_wd"])
    return x


def kernel(x_prompt, x_sample, ffn1_pre, ffn1_post, ffn1_w_gate, ffn1_w_up, ffn1_w_down, mix_pre, mix_post, w_in, a_q_norm, a_k_norm, b_sink, a_out_norm, b_out_norm, w_out, ffn2_pre, ffn2_post, ffn2_w_gate, ffn2_w_up, ffn2_w_down):
    depth = w_in.shape[0]
    rows = _w_in_rows()
    layers = []
    for l in range(depth):
        bf16 = lambda w: w[l].astype(jnp.bfloat16)
        sink = jnp.repeat(b_sink[l].astype(jnp.float32) * LOG2E, WINDOW)
        layers.append(dict(
            ffn1_pre=ffn1_pre[l][None, :], ffn1_post=ffn1_post[l][None, :],
            ffn1_wg=bf16(ffn1_w_gate), ffn1_wu=bf16(ffn1_w_up), ffn1_wd=bf16(ffn1_w_down),
            ffn2_pre=ffn2_pre[l][None, :], ffn2_post=ffn2_post[l][None, :],
            ffn2_wg=bf16(ffn2_w_gate), ffn2_wu=bf16(ffn2_w_up), ffn2_wd=bf16(ffn2_w_down),
            mix_pre=mix_pre[l][None, :], mix_post=mix_post[l][None, :],
            w_in_t=w_in[l].T[rows].astype(jnp.bfloat16),
            gq=a_q_norm[l][_DEINTERLEAVE][:, None], gk=a_k_norm[l][_DEINTERLEAVE][:, None],
            sink=sink.reshape(KV_HEADS, 1, HEADS_PER_KV * WINDOW),
            ga=a_out_norm[l][:, None], gb=b_out_norm[l][:, None],
            w_out=w_out[l].astype(jnp.bfloat16),
        ))
    bias = _window_bias()
    return (_trunk(x_prompt, layers, bias), _trunk(x_sample, layers, bias))
```

```python
import functools
import math

import jax
import jax.numpy as jnp
import numpy as np
from jax import lax
from jax.experimental import pallas as pl
from jax.experimental.pallas import tpu as pltpu

D_MODEL = 1024
HEAD_DIM = 64
HALF_DIM = HEAD_DIM // 2
Q_HEADS = 8
KV_HEADS = 2
HEADS_PER_KV = Q_HEADS // KV_HEADS
GROUP_WIDTH = Q_HEADS * HEAD_DIM
KV_WIDTH = KV_HEADS * HEAD_DIM
Q_PER_KV_ROWS = HEADS_PER_KV * HEAD_DIM
GROUP_IN = GROUP_WIDTH + 2 * KV_WIDTH
IN_WIDTH = 2 * GROUP_IN
D_FF = 2816
GRID_W = 64
ROPE_THETA = 10000.0
WINDOW = 128
NORM_EPS = 1e-6
FFN_RESID = 0.5
LOG2E = math.log2(math.e)
Q_SCALE = HEAD_DIM ** -0.5 * LOG2E
NEG_BIG = -1e30
BOUND_SLACK = 1.01
MAX_FIXED_SHIFT = 48.0

FF_CHUNK = 256
N_FF_CHUNKS = D_FF // FF_CHUNK
TOKEN_TILE = 512
V_CHUNK = TOKEN_TILE
Q_TILE = 512
ONES_ROWS = 16
ROW_BLOCK = 128
FIXED_UNROLL = 10
WQ_TILE = 1024
N_SUB = WQ_TILE // WINDOW
BAND = 3 * WINDOW

VMEM_LIMIT = 56 * 1024 * 1024
RESIDENT = pl.Buffered(1)


def _rms_rows(x, gain_row):
    ms = jnp.mean(x * x, axis=-1, keepdims=True)
    return x * lax.rsqrt(ms + NORM_EPS) * gain_row


def _rms_cols(xt, gain_col):
    ms = jnp.mean(xt * xt, axis=0, keepdims=True)
    return xt * lax.rsqrt(ms + NORM_EPS) * gain_col


def _zero_after(x):
    bits = lax.bitcast_convert_type(x, jnp.uint32)
    bits = lax.shift_right_logical(lax.shift_right_logical(bits, jnp.uint32(16)), jnp.uint32(16))
    return bits.astype(jnp.float32)


def _ffn_branch(x, pre_ref, post_ref, wg_ref, wu_ref, wd_ref, xn_ref, acc_ref):
    xn_ref[...] = _rms_rows(x, pre_ref[...]).astype(jnp.bfloat16)
    acc_ref[...] = jnp.zeros_like(acc_ref)

    for c in range(N_FF_CHUNKS):
        cols = slice(c * FF_CHUNK, (c + 1) * FF_CHUNK)
        g = jnp.dot(xn_ref[...], wg_ref[:, cols], preferred_element_type=jnp.float32)
        u = jnp.dot(xn_ref[...], wu_ref[:, cols], preferred_element_type=jnp.float32)
        h = (g * jax.nn.sigmoid(g) * u).astype(jnp.bfloat16)
        acc_ref[...] += jnp.dot(h, wd_ref[cols, :], preferred_element_type=jnp.float32)
    return FFN_RESID * _rms_rows(acc_ref[...], post_ref[...])


def _rope_norm_head(xt, gain_col, cos_t, sin_t):
    y = _rms_cols(xt, gain_col)
    x0 = y[:HALF_DIM]
    x1 = y[HALF_DIM:]
    return jnp.concatenate([x0 * cos_t - x1 * sin_t, x0 * sin_t + x1 * cos_t], axis=0)


def _ffn_mix_in_kernel(x_ref, pre1_ref, post1_ref, wg_ref, wu_ref, wd_ref,
                       pre_ref, w_ref, gq_ref, gk_ref, cos_ref, sin_ref,
                       x1_ref, qa_ref, qb_ref, k_ref, v_ref, qn_ref, kn_ref, xn_ref, acc_ref):
    x = x_ref[0]
    x1 = x + _ffn_branch(x, pre1_ref, post1_ref, wg_ref, wu_ref, wd_ref, xn_ref, acc_ref)
    x1_ref[0] = x1
    xn = _rms_rows(x1, pre_ref[...]).astype(jnp.bfloat16)
    pt = lax.dot_general(w_ref[...], xn, (((1,), (1,)), ((), ())),
                         preferred_element_type=jnp.float32)
    pa = pt[:GROUP_IN]
    pb = pt[GROUP_IN:]
    cos_t = cos_ref[...]
    sin_t = sin_ref[...]
    gq = gq_ref[...]
    gk = gk_ref[...]
    off = 0
    for h in range(Q_HEADS):
        r = _rope_norm_head(pa[off:off + HEAD_DIM], gq, cos_t, sin_t) * Q_SCALE
        qa_ref[0, h * HEAD_DIM:(h + 1) * HEAD_DIM, :] = r.astype(jnp.bfloat16)
        qn_ref[0, h] = jnp.sum(r * r, axis=0, keepdims=True)
        off += HEAD_DIM
    for h in range(KV_HEADS):
        r = _rope_norm_head(pa[off:off + HEAD_DIM], gk, cos_t, sin_t)
        k_ref[0, h] = r.T.astype(jnp.bfloat16)
        kn_ref[0, h] = jnp.sum(r * r, axis=0, keepdims=True)
        off += HEAD_DIM
    v_ref[0, 0, :KV_WIDTH, :] = pa[off:off + KV_WIDTH].astype(jnp.bfloat16)
    for h in range(Q_HEADS):
        r = pb[h * HEAD_DIM:(h + 1) * HEAD_DIM] * Q_SCALE
        qb_ref[0, h * HEAD_DIM:(h + 1) * HEAD_DIM, :] = r.astype(jnp.bfloat16)
        qn_ref[0, Q_HEADS + h] = jnp.sum(r * r, axis=0, keepdims=True)
    off = GROUP_WIDTH
    for h in range(KV_HEADS):
        r = pb[off:off + HEAD_DIM]
        k_ref[0, KV_HEADS + h] = r.T.astype(jnp.bfloat16)
        kn_ref[0, KV_HEADS + h] = jnp.sum(r * r, axis=0, keepdims=True)
        off += HEAD_DIM
    v_ref[0, 0, KV_WIDTH:, :] = pb[off:off + KV_WIDTH].astype(jnp.bfloat16)


def _ffn_mix_in(x, pre1, post1, wg, wu, wd, pre, w_in_t, gq, gk, cos_t, sin_t):
    b, s, _ = x.shape
    n_tiles = s // TOKEN_TILE
    vec = lambda bi, i: (0, 0)
    return pl.pallas_call(
        _ffn_mix_in_kernel,
        name="ffn_mix_in",
        grid=(b, n_tiles),
        in_specs=[
            pl.BlockSpec((1, TOKEN_TILE, D_MODEL), lambda bi, i: (bi, i, 0)),
            pl.BlockSpec((1, D_MODEL), vec),
            pl.BlockSpec((1, D_MODEL), vec),
            pl.BlockSpec((D_MODEL, D_FF), vec, pipeline_mode=RESIDENT),
            pl.BlockSpec((D_MODEL, D_FF), vec, pipeline_mode=RESIDENT),
            pl.BlockSpec((D_FF, D_MODEL), vec, pipeline_mode=RESIDENT),
            pl.BlockSpec((1, D_MODEL), vec),
            pl.BlockSpec((IN_WIDTH, D_MODEL), vec, pipeline_mode=RESIDENT),
            pl.BlockSpec((HEAD_DIM, 1), vec),
            pl.BlockSpec((HEAD_DIM, 1), vec),
            pl.BlockSpec((HALF_DIM, TOKEN_TILE), lambda bi, i: (0, i)),
            pl.BlockSpec((HALF_DIM, TOKEN_TILE), lambda bi, i: (0, i)),
        ],
        out_specs=[
            pl.BlockSpec((1, TOKEN_TILE, D_MODEL), lambda bi, i: (bi, i, 0)),
            pl.BlockSpec((1, GROUP_WIDTH, TOKEN_TILE), lambda bi, i: (bi, 0, i)),
            pl.BlockSpec((1, GROUP_WIDTH, TOKEN_TILE), lambda bi, i: (bi, 0, i)),
            pl.BlockSpec((1, 2 * KV_HEADS, TOKEN_TILE, HEAD_DIM), lambda bi, i: (bi, 0, i, 0)),
            pl.BlockSpec((1, 1, 2 * KV_WIDTH, V_CHUNK), lambda bi, i: (bi, i, 0, 0)),
            pl.BlockSpec((1, 2 * Q_HEADS, 1, TOKEN_TILE), lambda bi, i: (bi, 0, 0, i)),
            pl.BlockSpec((1, 2 * KV_HEADS, 1, TOKEN_TILE), lambda bi, i: (bi, 0, 0, i)),
        ],
        out_shape=[
            jax.ShapeDtypeStruct((b, s, D_MODEL), jnp.float32),
            jax.ShapeDtypeStruct((b, GROUP_WIDTH, s), jnp.bfloat16),
            jax.ShapeDtypeStruct((b, GROUP_WIDTH, s), jnp.bfloat16),
            jax.ShapeDtypeStruct((b, 2 * KV_HEADS, s, HEAD_DIM), jnp.bfloat16),
            jax.ShapeDtypeStruct((b, n_tiles, 2 * KV_WIDTH, V_CHUNK), jnp.bfloat16),
            jax.ShapeDtypeStruct((b, 2 * Q_HEADS, 1, s), jnp.float32),
            jax.ShapeDtypeStruct((b, 2 * KV_HEADS, 1, s), jnp.float32),
        ],
        scratch_shapes=[
            pltpu.VMEM((TOKEN_TILE, D_MODEL), jnp.bfloat16),
            pltpu.VMEM((TOKEN_TILE, D_MODEL), jnp.float32),
        ],
        compiler_params=pltpu.CompilerParams(
            dimension_semantics=("arbitrary", "arbitrary"), vmem_limit_bytes=VMEM_LIMIT),
    )(x, pre1, post1, wg, wu, wd, pre, w_in_t, gq, gk, cos_t, sin_t)


def _shift_stats(stat_ref, qn_all_ref, kn_ref):
    @pl.when(pl.program_id(2) == 0)
    def _():
        kmax2 = jnp.max(kn_ref[0, 0], axis=1, keepdims=True)
        qmax2 = jnp.max(jnp.max(qn_all_ref[0], axis=0), axis=1, keepdims=True)
        stat_ref[0] = jnp.max(kmax2)
        stat_ref[1] = jnp.max(BOUND_SLACK * jnp.sqrt(qmax2 * kmax2))
    return stat_ref[0], stat_ref[1] <= MAX_FIXED_SHIFT


def _global_kernel(q_ref, k_ref, v_ref, qn_ref, qn_all_ref, kn_ref, o_ref, s_ref, smax_ref, p_ref,
                   alpha_ref, m_ref, c_ref, l_ref, acc_ref, stat_ref, *, n_chunks):
    ones = jnp.ones((ONES_ROWS, V_CHUNK), jnp.bfloat16)
    n_blocks = V_CHUNK // ROW_BLOCK
    acc_ref[...] = jnp.zeros_like(acc_ref)

    kmax2, use_fixed_shift = _shift_stats(stat_ref, qn_all_ref, kn_ref)
    c_ref[...] = BOUND_SLACK * jnp.sqrt(qn_ref[0] * kmax2)

    def k_rows(chunk, r):
        start = chunk * V_CHUNK + r * ROW_BLOCK
        if not isinstance(start, int):
            start = pl.multiple_of(start, ROW_BLOCK)
        return k_ref[0, 0, pl.ds(start, ROW_BLOCK), :]

    def v_ext(chunk):
        return jnp.concatenate([v_ref[0, chunk], ones], axis=0)

    def values_block(vext, slot, h, r):
        cols = slice((r - 1) * ROW_BLOCK, (r + 1) * ROW_BLOCK)
        return jnp.dot(vext[:, cols], p_ref[slot, h, cols, :], preferred_element_type=jnp.float32)

    def fixed_step(j, parity, do_scores=True, do_values=True):
        cur, other = parity, 1 - parity
        if do_values:
            vt = v_ref[0, j]
        after_values = None
        for h in range(HEADS_PER_KV):
            q_h = q_ref[0, h * HEAD_DIM:(h + 1) * HEAD_DIM, :]
            c_h = c_ref[h]
            pv = None
            row_sum = None
            for r in range(n_blocks):
                rows = slice(r * ROW_BLOCK, (r + 1) * ROW_BLOCK)
                if do_scores:
                    st = jnp.dot(k_rows(j + 1, r), q_h, preferred_element_type=jnp.float32)
                    shift = c_h if after_values is None else c_h + after_values
                    p = jnp.exp2(st - shift)
                    p_ref[other, h, rows, :] = p.astype(jnp.bfloat16)
                    part = jnp.sum(p.reshape(ROW_BLOCK // 8, 8, Q_TILE), axis=0)
                    row_sum = part if row_sum is None else row_sum + part
                if do_values and r % 2 == 1:
                    d = values_block(vt, cur, h, r)
                    pv = d if pv is None else pv + d
                    after_values = _zero_after(d[:1])
            if do_scores:
                l_ref[h] += row_sum
            if do_values:
                acc_ref[h, :HEAD_DIM, :] += pv

    def online_step(j, parity, do_scores=True, do_softmax=True, do_values=True):
        cur, other = parity, 1 - parity
        if do_values:
            vext = v_ext(j - 1)
        for h in range(HEADS_PER_KV):
            q_h = q_ref[0, h * HEAD_DIM:(h + 1) * HEAD_DIM, :]
            if do_softmax:
                m_old = m_ref[h]
                m_new = jnp.maximum(m_old, smax_ref[cur, h])
                alpha_ref[cur, h] = jnp.exp2(m_old - m_new)
                m_ref[h] = m_new
            blk_max = None
            pv = None
            for r in range(n_blocks):
                rows = slice(r * ROW_BLOCK, (r + 1) * ROW_BLOCK)
                if do_scores:
                    st = jnp.dot(k_rows(j + 1, r), q_h, preferred_element_type=jnp.float32)
                    s_ref[other, h, rows, :] = st
                    part = jnp.max(st.reshape(ROW_BLOCK // 8, 8, Q_TILE), axis=0)
                    blk_max = part if blk_max is None else jnp.maximum(blk_max, part)
                if do_softmax:
                    p_ref[cur, h, rows, :] = jnp.exp2(s_ref[cur, h, rows, :] - m_new).astype(
                        jnp.bfloat16)
                if do_values and r % 2 == 1:
                    d = values_block(vext, other, h, r)
                    pv = d if pv is None else pv + d
            if do_scores:
                smax_ref[other, h] = jnp.max(blk_max, axis=0, keepdims=True)
            if do_values:
                acc_ref[h] = alpha_ref[other, h] * acc_ref[h] + pv

    @pl.when(use_fixed_shift)
    def _():
        l_ref[...] = jnp.zeros_like(l_ref)
        fixed_step(-1, 1, do_values=False)

        def steps(t, carry):
            for u in range(FIXED_UNROLL):
                fixed_step(FIXED_UNROLL * t + u, u % 2)
            return carry

        n_loop = (n_chunks - 2) // FIXED_UNROLL
        lax.fori_loop(0, n_loop, steps, 0)
        for j in range(n_loop * FIXED_UNROLL, n_chunks - 2):
            fixed_step(j, j % 2)
        fixed_step(n_chunks - 2, 0)
        fixed_step(n_chunks - 1, 1, do_scores=False)
        for h in range(HEADS_PER_KV):
            acc_ref[h, HEAD_DIM:HEAD_DIM + 1, :] = jnp.sum(l_ref[h], axis=0, keepdims=True)

    @pl.when(jnp.logical_not(use_fixed_shift))
    def _():
        m_ref[...] = jnp.full_like(m_ref, NEG_BIG)
        online_step(-1, 1, do_softmax=False, do_values=False)
        online_step(0, 0, do_values=False)

        def pair(t, carry):
            online_step(2 * t + 1, 1)
            online_step(2 * t + 2, 0)
            return carry

        lax.fori_loop(0, (n_chunks - 2) // 2, pair, 0)
        online_step(n_chunks - 1, 1, do_scores=False)
        online_step(n_chunks, 0, do_scores=False, do_softmax=False)

    for h in range(HEADS_PER_KV):
        a = acc_ref[h]
        o_ref[0, h * HEAD_DIM:(h + 1) * HEAD_DIM, :] = a[:HEAD_DIM] / a[HEAD_DIM:HEAD_DIM + 1]


def _global_attention(q_t, k, v_t, q_norm2, k_norm2):
    b, _, s = q_t.shape
    n_chunks = s // V_CHUNK
    assert n_chunks >= 2 and n_chunks % 2 == 0, "the kv pipeline runs chunks in pairs"
    return pl.pallas_call(
        functools.partial(_global_kernel, n_chunks=n_chunks),
        name="global_attn",
        grid=(b, KV_HEADS, s // Q_TILE),
        in_specs=[
            pl.BlockSpec((1, Q_PER_KV_ROWS, Q_TILE), lambda bi, g, i: (bi, g, i)),
            pl.BlockSpec((1, 1, s, HEAD_DIM), lambda bi, g, i: (bi, g, 0, 0)),
            pl.BlockSpec((1, n_chunks, HEAD_DIM, V_CHUNK), lambda bi, g, i: (bi, 0, g, 0)),
            pl.BlockSpec((1, HEADS_PER_KV, 1, Q_TILE), lambda bi, g, i: (bi, g, 0, i)),
            pl.BlockSpec((1, HEADS_PER_KV, 1, s), lambda bi, g, i: (bi, g, 0, 0)),
            pl.BlockSpec((1, 1, 1, s), lambda bi, g, i: (bi, g, 0, 0)),
        ],
        out_specs=pl.BlockSpec((1, Q_PER_KV_ROWS, Q_TILE), lambda bi, g, i: (bi, g, i)),
        out_shape=jax.ShapeDtypeStruct((b, GROUP_WIDTH, s), jnp.float32),
        scratch_shapes=[
            pltpu.VMEM((2, HEADS_PER_KV, V_CHUNK, Q_TILE), jnp.float32),
            pltpu.VMEM((2, HEADS_PER_KV, 1, Q_TILE), jnp.float32),
            pltpu.VMEM((2, HEADS_PER_KV, V_CHUNK, Q_TILE), jnp.bfloat16),
            pltpu.VMEM((2, HEADS_PER_KV, 1, Q_TILE), jnp.float32),
            pltpu.VMEM((HEADS_PER_KV, 1, Q_TILE), jnp.float32),
            pltpu.VMEM((HEADS_PER_KV, 1, Q_TILE), jnp.float32),
            pltpu.VMEM((HEADS_PER_KV, 8, Q_TILE), jnp.float32),
            pltpu.VMEM((HEADS_PER_KV, HEAD_DIM + ONES_ROWS, Q_TILE), jnp.float32),
            pltpu.SMEM((2,), jnp.float32),
        ],
        compiler_params=pltpu.CompilerParams(
            dimension_semantics=("arbitrary", "arbitrary", "arbitrary"),
            vmem_limit_bytes=VMEM_LIMIT),
    )(q_t, k, v_t, q_norm2, q_norm2, k_norm2)


def _window_kernel(q_ref, k_ref, v_ref, bias_ref, sink_ref, qn2_ref, qn2_all_ref, kn2_ref,
                   o_ref, s_ref, mx_ref, p_ref, stat_ref, *, seq_len):
    i = pl.program_id(2)
    has_prev = i > 0
    has_next = i < pl.num_programs(2) - 1
    sink = sink_ref[0]
    no_prev = jnp.where(has_prev, 0.0, NEG_BIG)
    no_next = jnp.where(has_next, 0.0, NEG_BIG)
    blocks = WQ_TILE // WINDOW
    chunks_per_tile = WQ_TILE // V_CHUNK
    blocks_per_chunk = V_CHUNK // WINDOW
    last_chunk = seq_len // V_CHUNK - 1

    def k_block(blk):
        start = jnp.clip(i * WQ_TILE + blk * WINDOW, 0, seq_len - WINDOW)
        return k_ref[0, 0, pl.ds(pl.multiple_of(start, WINDOW), WINDOW), :]

    def v_block(blk):
        first = i * chunks_per_tile
        if blk < 0:
            return v_ref[0, jnp.maximum(first - 1, 0), :, V_CHUNK - WINDOW:]
        if blk >= blocks:
            return v_ref[0, jnp.minimum(first + chunks_per_tile, last_chunk), :, :WINDOW]
        lo = (blk % blocks_per_chunk) * WINDOW
        return v_ref[0, first + blk // blocks_per_chunk, :, lo:lo + WINDOW]

    def band(sb, block_fn, axis):
        return jnp.concatenate([block_fn(sb - 1 + r) for r in range(3)], axis=axis)

    def queries(sb):
        lo = sb * WINDOW
        return jnp.concatenate(
            [q_ref[0, h * HEAD_DIM:(h + 1) * HEAD_DIM, lo:lo + WINDOW] for h in range(HEADS_PER_KV)],
            axis=1)

    def edge_penalty(sb, r):
        if sb == 0 and r == 0:
            return no_prev
        if sb == N_SUB - 1 and r == 2:
            return no_next
        return None

    def store_out(sb, ot):
        lo = sb * WINDOW
        for h in range(HEADS_PER_KV):
            o_ref[0, h * HEAD_DIM:(h + 1) * HEAD_DIM, lo:lo + WINDOW] = (
                ot[:, h * WINDOW:(h + 1) * WINDOW])

    kmax2, use_fixed_shift = _shift_stats(stat_ref, qn2_all_ref, kn2_ref)
    c_all = BOUND_SLACK * jnp.sqrt(qn2_ref[0] * kmax2)

    @pl.when(use_fixed_shift)
    def _():
        def shift_of(sb):
            lo = sb * WINDOW
            c = jnp.concatenate([c_all[h, :, lo:lo + WINDOW] for h in range(HEADS_PER_KV)], axis=1)
            return jnp.maximum(c, sink)

        def values_part(sb, k_lo, k_hi):
            vs = jnp.concatenate(
                [v_block(sb - 1 + r) for r in range(k_lo // WINDOW, k_hi // WINDOW)], axis=1)
            return jnp.dot(vs, p_ref[sb, k_lo:k_hi, :], preferred_element_type=jnp.float32)

        def finish(sb, ot, row_sum, shift):
            denom = jnp.sum(row_sum, axis=0, keepdims=True) + jnp.exp2(sink - shift)
            store_out(sb, ot / denom)

        prev = None
        for sb in range(N_SUB + 1):
            scoring = sb < N_SUB
            if scoring:
                qs = queries(sb)
                shift = shift_of(sb)
                row_sum = None
            after_values = None
            ot = None
            for r in range(3):
                if scoring:
                    rows = slice(r * WINDOW, (r + 1) * WINDOW)
                    ks = k_block(sb - 1 + r)
                    st = jnp.dot(ks, qs, preferred_element_type=jnp.float32) + bias_ref[0, rows, :]
                    pen = edge_penalty(sb, r)
                    sh = shift if pen is None else shift - pen
                    if after_values is not None:
                        sh = sh + after_values
                    p = jnp.exp2(st - sh)
                    p_ref[sb, rows, :] = p.astype(jnp.bfloat16)
                    part = jnp.sum(p.reshape(WINDOW // 8, 8, HEADS_PER_KV * WINDOW), axis=0)
                    row_sum = part if row_sum is None else row_sum + part
                if prev is not None and r < 2:
                    d = values_part(prev[0], 0, 2 * WINDOW) if r == 0 else values_part(
                        prev[0], 2 * WINDOW, BAND)
                    ot = d if ot is None else ot + d
                    after_values = _zero_after(d[:1])
            if prev is not None:
                finish(prev[0], ot, prev[1], prev[2])
            prev = (sb, row_sum, shift) if scoring else None

    @pl.when(jnp.logical_not(use_fixed_shift))
    def _():
        ones = jnp.ones((ONES_ROWS, BAND), jnp.bfloat16)

        def scores(sb):
            ks = band(sb, k_block, 0)
            st = jnp.dot(ks, queries(sb), preferred_element_type=jnp.float32) + bias_ref[0]
            if sb == 0:
                st = jnp.concatenate([st[:WINDOW] + no_prev, st[WINDOW:]], axis=0)
            if sb == N_SUB - 1:
                st = jnp.concatenate([st[:2 * WINDOW], st[2 * WINDOW:] + no_next], axis=0)
            s_ref[sb] = st
            mx_ref[sb] = jnp.maximum(jnp.max(st, axis=0, keepdims=True), sink)

        def probabilities(sb):
            p_ref[sb] = jnp.exp2(s_ref[sb] - mx_ref[sb]).astype(jnp.bfloat16)

        def values(sb):
            vext = jnp.concatenate([band(sb, v_block, 1), ones], axis=0)
            ot = jnp.dot(vext, p_ref[sb], preferred_element_type=jnp.float32)
            denom = ot[HEAD_DIM:HEAD_DIM + 1] + jnp.exp2(sink - mx_ref[sb])
            store_out(sb, ot[:HEAD_DIM] / denom)

        for t in range(N_SUB + 2):
            if t < N_SUB:
                scores(t)
            if 1 <= t <= N_SUB:
                probabilities(t - 1)
            if t >= 2:
                values(t - 2)


def _window_attention(q_t, k, v_t, bias, sink, q_norm2, k_norm2):
    b, _, s = q_t.shape
    assert WQ_TILE % V_CHUNK == 0, "a query tile's own keys are whole v^T chunks"
    n_chunks = s // V_CHUNK
    kh = lambda g: KV_HEADS + g
    return pl.pallas_call(
        functools.partial(_window_kernel, seq_len=s),
        name="window_attn",
        grid=(b, KV_HEADS, s // WQ_TILE),
        in_specs=[
            pl.BlockSpec((1, Q_PER_KV_ROWS, WQ_TILE), lambda bi, g, i: (bi, g, i)),
            pl.BlockSpec((1, 1, s, HEAD_DIM), lambda bi, g, i: (bi, kh(g), 0, 0)),
            pl.BlockSpec((1, n_chunks, HEAD_DIM, V_CHUNK), lambda bi, g, i: (bi, 0, kh(g), 0)),
            pl.BlockSpec((1, BAND, HEADS_PER_KV * WINDOW), lambda bi, g, i: (g, 0, 0)),
            pl.BlockSpec((1, 1, HEADS_PER_KV * WINDOW), lambda bi, g, i: (g, 0, 0)),
            pl.BlockSpec((1, HEADS_PER_KV, 1, WQ_TILE), lambda bi, g, i: (bi, kh(g), 0, i)),
            pl.BlockSpec((1, HEADS_PER_KV, 1, s), lambda bi, g, i: (bi, kh(g), 0, 0)),
            pl.BlockSpec((1, 1, 1, s), lambda bi, g, i: (bi, kh(g), 0, 0)),
        ],
        out_specs=pl.BlockSpec((1, Q_PER_KV_ROWS, WQ_TILE), lambda bi, g, i: (bi, g, i)),
        out_shape=jax.ShapeDtypeStruct((b, GROUP_WIDTH, s), jnp.float32),
        scratch_shapes=[
            pltpu.VMEM((N_SUB, BAND, HEADS_PER_KV * WINDOW), jnp.float32),
            pltpu.VMEM((N_SUB, 1, HEADS_PER_KV * WINDOW), jnp.float32),
            pltpu.VMEM((N_SUB, BAND, HEADS_PER_KV * WINDOW), jnp.bfloat16),
            pltpu.SMEM((2,), jnp.float32),
        ],
        compiler_params=pltpu.CompilerParams(
            dimension_semantics=("arbitrary", "arbitrary", "arbitrary"),
            vmem_limit_bytes=VMEM_LIMIT),
    )(q_t, k, v_t, bias, sink, q_norm2, q_norm2, k_norm2)


def _mix_out_ffn_kernel(x_ref, oa_ref, ob_ref, ga_ref, gb_ref, w_ref, post_ref,
                        pre2_ref, post2_ref, wg_ref, wu_ref, wd_ref, o_ref, xn_ref, acc_ref):
    na = _rms_cols(oa_ref[0], ga_ref[...])
    nb = _rms_cols(ob_ref[0], gb_ref[...])
    mt = jnp.concatenate([na, nb], axis=0).astype(jnp.bfloat16)
    h = lax.dot_general(mt, w_ref[...], (((0,), (0,)), ((), ())),
                        preferred_element_type=jnp.float32)
    x1 = x_ref[0] + _rms_rows(h, post_ref[...])
    o_ref[0] = x1
    o_ref[0] += _ffn_branch(x1, pre2_ref, post2_ref, wg_ref, wu_ref, wd_ref, xn_ref, acc_ref)


def _mix_out_ffn(x, oa_t, ob_t, ga, gb, w_out, post, pre2, post2, wg, wu, wd):
    b, s, _ = x.shape
    vec = lambda bi, i: (0, 0)
    return pl.pallas_call(
        _mix_out_ffn_kernel,
        name="mix_out_ffn",
        grid=(b, s // TOKEN_TILE),
        in_specs=[
            pl.BlockSpec((1, TOKEN_TILE, D_MODEL), lambda bi, i: (bi, i, 0)),
            pl.BlockSpec((1, GROUP_WIDTH, TOKEN_TILE), lambda bi, i: (bi, 0, i)),
            pl.BlockSpec((1, GROUP_WIDTH, TOKEN_TILE), lambda bi, i: (bi, 0, i)),
            pl.BlockSpec((GROUP_WIDTH, 1), vec),
            pl.BlockSpec((GROUP_WIDTH, 1), vec),
            pl.BlockSpec((D_MODEL, D_MODEL), vec, pipeline_mode=RESIDENT),
            pl.BlockSpec((1, D_MODEL), vec),
            pl.BlockSpec((1, D_MODEL), vec),
            pl.BlockSpec((1, D_MODEL), vec),
            pl.BlockSpec((D_MODEL, D_FF), vec, pipeline_mode=RESIDENT),
            pl.BlockSpec((D_MODEL, D_FF), vec, pipeline_mode=RESIDENT),
            pl.BlockSpec((D_FF, D_MODEL), vec, pipeline_mode=RESIDENT),
        ],
        out_specs=pl.BlockSpec((1, TOKEN_TILE, D_MODEL), lambda bi, i: (bi, i, 0)),
        out_shape=jax.ShapeDtypeStruct((b, s, D_MODEL), jnp.float32),
        scratch_shapes=[
            pltpu.VMEM((TOKEN_TILE, D_MODEL), jnp.bfloat16),
            pltpu.VMEM((TOKEN_TILE, D_MODEL), jnp.float32),
        ],
        compiler_params=pltpu.CompilerParams(
            dimension_semantics=("arbitrary", "arbitrary"), vmem_limit_bytes=VMEM_LIMIT),
    )(x, oa_t, ob_t, ga, gb, w_out, post, pre2, post2, wg, wu, wd)


_DEINTERLEAVE = np.concatenate([np.arange(0, HEAD_DIM, 2), np.arange(1, HEAD_DIM, 2)])


def _w_in_rows():
    n_rot = (Q_HEADS + KV_HEADS) * HEAD_DIM
    rot = (np.arange(Q_HEADS + KV_HEADS)[:, None] * HEAD_DIM + _DEINTERLEAVE[None, :]).reshape(-1)
    return np.concatenate([rot, np.arange(n_rot, IN_WIDTH)])


def _rope_tables_t(seq_len):
    n_rows = seq_len // GRID_W
    row = jnp.repeat(jnp.arange(n_rows, dtype=jnp.float32), GRID_W)
    col = jnp.tile(jnp.arange(GRID_W, dtype=jnp.float32), n_rows)
    n_freq = HEAD_DIM // 4
    inv_freq = ROPE_THETA ** (-jnp.arange(n_freq, dtype=jnp.float32) / n_freq)
    ang = jnp.concatenate([inv_freq[:, None] * row[None, :], inv_freq[:, None] * col[None, :]], axis=0)
    return jnp.cos(ang), jnp.sin(ang)


def _window_bias():
    slopes = 2.0 ** (-8.0 * np.arange(1, Q_HEADS + 1, dtype=np.float64) / Q_HEADS)
    kpos = np.arange(BAND)[:, None] - WINDOW
    qpos = np.arange(WINDOW)[None, :]
    dist = np.abs(qpos - kpos).astype(np.float64)
    per_head = np.where(dist[None] <= WINDOW, -slopes[:, None, None] * dist[None] * LOG2E, NEG_BIG)
    per_kv = per_head.reshape(KV_HEADS, HEADS_PER_KV, BAND, WINDOW).transpose(0, 2, 1, 3)
    return jnp.asarray(per_kv.reshape(KV_HEADS, BAND, HEADS_PER_KV * WINDOW), jnp.float32)


def _trunk(x, layers, bias):
    b, s, _ = x.shape
    cos_t, sin_t = _rope_tables_t(s)
    for p in layers:
        x, qa_t, qb_t, k, v_t, q_norm2, k_norm2 = _ffn_mix_in(
            x, p["ffn1_pre"], p["ffn1_post"], p["ffn1_wg"], p["ffn1_wu"], p["ffn1_wd"],
            p["mix_pre"], p["w_in_t"], p["gq"], p["gk"], cos_t, sin_t)
        oa_t = _global_attention(qa_t, k, v_t, q_norm2, k_norm2)
        ob_t = _window_attention(qb_t, k, v_t, bias, p["sink"], q_norm2, k_norm2)
        x = _mix_out_ffn(x, oa_t, ob_t, p["ga"], p["gb"], p["w_out"], p["mix_post"],
                         p["ffn2_pre"], p["ffn2_post"], p["ffn2_wg"], p["ffn2_wu"], p["ffn2_wd"])
    return x


def kernel(x_prompt, x_sample, ffn1_pre, ffn1_post, ffn1_w_gate, ffn1_w_up, ffn1_w_down, mix_pre, mix_post, w_in, a_q_norm, a_k_norm, b_sink, a_out_norm, b_out_norm, w_out, ffn2_pre, ffn2_post, ffn2_w_gate, ffn2_w_up, ffn2_w_down):
    depth = w_in.shape[0]
    rows = _w_in_rows()
    layers = []
    for l in range(depth):
        bf16 = lambda w: w[l].astype(jnp.bfloat16)
        sink = jnp.repeat(b_sink[l].astype(jnp.float32) * LOG2E, WINDOW)
        layers.append(dict(
            ffn1_pre=ffn1_pre[l][None, :], ffn1_post=ffn1_post[l][None, :],
            ffn1_wg=bf16(ffn1_w_gate), ffn1_wu=bf16(ffn1_w_up), ffn1_wd=bf16(ffn1_w_down),
            ffn2_pre=ffn2_pre[l][None, :], ffn2_post=ffn2_post[l][None, :],
            ffn2_wg=bf16(ffn2_w_gate), ffn2_wu=bf16(ffn2_w_up), ffn2_wd=bf16(ffn2_w_down),
            mix_pre=mix_pre[l][None, :], mix_post=mix_post[l][None, :],
            w_in_t=w_in[l].T[rows].astype(jnp.bfloat16),
            gq=a_q_norm[l][_DEINTERLEAVE][:, None], gk=a_k_norm[l][_DEINTERLEAVE][:, None],
            sink=sink.reshape(KV_HEADS, 1, HEADS_PER_KV * WINDOW),
            ga=a_out_norm[l][:, None], gb=b_out_norm[l][:, None],
            w_out=w_out[l].astype(jnp.bfloat16),
        ))
    bias = _window_bias()
    return (_trunk(x_prompt, layers, bias), _trunk(x_sample, layers, bias))
```

```python
import functools
import math

import jax
import jax.numpy as jnp
import numpy as np
from jax import lax
from jax.experimental import pallas as pl
from jax.experimental.pallas import tpu as pltpu

D_MODEL = 1024
HEAD_DIM = 64
HALF_DIM = HEAD_DIM // 2
Q_HEADS = 8
KV_HEADS = 2
HEADS_PER_KV = Q_HEADS // KV_HEADS
GROUP_WIDTH = Q_HEADS * HEAD_DIM
KV_WIDTH = KV_HEADS * HEAD_DIM
Q_PER_KV_ROWS = HEADS_PER_KV * HEAD_DIM
GROUP_IN = GROUP_WIDTH + 2 * KV_WIDTH
IN_WIDTH = 2 * GROUP_IN
D_FF = 2816
GRID_W = 64
ROPE_THETA = 10000.0
WINDOW = 128
NORM_EPS = 1e-6
FFN_RESID = 0.5
LOG2E = math.log2(math.e)
Q_SCALE = HEAD_DIM ** -0.5 * LOG2E
NEG_BIG = -1e30
BOUND_SLACK = 1.01
MAX_FIXED_SHIFT = 48.0

FF_CHUNK = 256
N_FF_CHUNKS = D_FF // FF_CHUNK
TOKEN_TILE = 512
V_CHUNK = TOKEN_TILE
Q_TILE = 512
ONES_ROWS = 16
ROW_BLOCK = 128
FIXED_UNROLL = 10
WQ_TILE = 1024
N_SUB = WQ_TILE // WINDOW
BAND = 3 * WINDOW

VMEM_LIMIT = 56 * 1024 * 1024
RESIDENT = pl.Buffered(1)


def _rms_rows(x, gain_row):
    ms = jnp.mean(x * x, axis=-1, keepdims=True)
    return x * lax.rsqrt(ms + NORM_EPS) * gain_row


def _rms_cols(xt, gain_col):
    ms = jnp.mean(xt * xt, axis=0, keepdims=True)
    return xt * lax.rsqrt(ms + NORM_EPS) * gain_col


def _zero_after(x):
    bits = lax.bitcast_convert_type(x, jnp.uint32)
    bits = lax.shift_right_logical(lax.shift_right_logical(bits, jnp.uint32(16)), jnp.uint32(16))
    return bits.astype(jnp.float32)


def _ffn_branch(x, pre_ref, post_ref, wg_ref, wu_ref, wd_ref, xn_ref, acc_ref):
    xn_ref[...] = _rms_rows(x, pre_ref[...]).astype(jnp.bfloat16)
    acc_ref[...] = jnp.zeros_like(acc_ref)

    for c in range(N_FF_CHUNKS):
        cols = slice(c * FF_CHUNK, (c + 1) * FF_CHUNK)
        g = jnp.dot(xn_ref[...], wg_ref[:, cols], preferred_element_type=jnp.float32)
        u = jnp.dot(xn_ref[...], wu_ref[:, cols], preferred_element_type=jnp.float32)
        h = (g * jax.nn.sigmoid(g) * u).astype(jnp.bfloat16)
        acc_ref[...] += jnp.dot(h, wd_ref[cols, :], preferred_element_type=jnp.float32)
    return FFN_RESID * _rms_rows(acc_ref[...], post_ref[...])


def _rope_norm_head(xt, gain_col, cos_t, sin_t):
    y = _rms_cols(xt, gain_col)
    x0 = y[:HALF_DIM]
    x1 = y[HALF_DIM:]
    return jnp.concatenate([x0 * cos_t - x1 * sin_t, x0 * sin_t + x1 * cos_t], axis=0)


def _ffn_mix_in_kernel(x_ref, pre1_ref, post1_ref, wg_ref, wu_ref, wd_ref,
                       pre_ref, w_ref, gq_ref, gk_ref, cos_ref, sin_ref,
                       x1_ref, qa_ref, qb_ref, k_ref, v_ref, qn_ref, kn_ref, xn_ref, acc_ref):
    x = x_ref[0]
    x1 = x + _ffn_branch(x, pre1_ref, post1_ref, wg_ref, wu_ref, wd_ref, xn_ref, acc_ref)
    x1_ref[0] = x1
    xn = _rms_rows(x1, pre_ref[...]).astype(jnp.bfloat16)
    pt = lax.dot_general(w_ref[...], xn, (((1,), (1,)), ((), ())),
                         preferred_element_type=jnp.float32)
    pa = pt[:GROUP_IN]
    pb = pt[GROUP_IN:]
    cos_t = cos_ref[...]
    sin_t = sin_ref[...]
    gq = gq_ref[...]
    gk = gk_ref[...]
    off = 0
    for h in range(Q_HEADS):
        r = _rope_norm_head(pa[off:off + HEAD_DIM], gq, cos_t, sin_t) * Q_SCALE
        qa_ref[0, h * HEAD_DIM:(h + 1) * HEAD_DIM, :] = r.astype(jnp.bfloat16)
        qn_ref[0, h] = jnp.sum(r * r, axis=0, keepdims=True)
        off += HEAD_DIM
    for h in range(KV_HEADS):
        r = _rope_norm_head(pa[off:off + HEAD_DIM], gk, cos_t, sin_t)
        k_ref[0, h] = r.T.astype(jnp.bfloat16)
        kn_ref[0, h] = jnp.sum(r * r, axis=0, keepdims=True)
        off += HEAD_DIM
    v_ref[0, 0, :KV_WIDTH, :] = pa[off:off + KV_WIDTH].astype(jnp.bfloat16)
    for h in range(Q_HEADS):
        r = pb[h * HEAD_DIM:(h + 1) * HEAD_DIM] * Q_SCALE
        qb_ref[0, h * HEAD_DIM:(h + 1) * HEAD_DIM, :] = r.astype(jnp.bfloat16)
        qn_ref[0, Q_HEADS + h] = jnp.sum(r * r, axis=0, keepdims=True)
    off = GROUP_WIDTH
    for h in range(KV_HEADS):
        r = pb[off:off + HEAD_DIM]
        k_ref[0, KV_HEADS + h] = r.T.astype(jnp.bfloat16)
        kn_ref[0, KV_HEADS + h] = jnp.sum(r * r, axis=0, keepdims=True)
        off += HEAD_DIM
    v_ref[0, 0, KV_WIDTH:, :] = pb[off:off + KV_WIDTH].astype(jnp.bfloat16)


def _ffn_weight_specs(layer):
    pick = lambda bi, i: (layer, 0, 0)
    return [
        pl.BlockSpec((None, D_MODEL, D_FF), pick, pipeline_mode=RESIDENT),
        pl.BlockSpec((None, D_MODEL, D_FF), pick, pipeline_mode=RESIDENT),
        pl.BlockSpec((None, D_FF, D_MODEL), pick, pipeline_mode=RESIDENT),
    ]


def _ffn_mix_in(x, pre1, post1, wg, wu, wd, layer, pre, w_in_t, gq, gk, cos_t, sin_t):
    b, s, _ = x.shape
    n_tiles = s // TOKEN_TILE
    vec = lambda bi, i: (0, 0)
    return pl.pallas_call(
        _ffn_mix_in_kernel,
        name="ffn_mix_in",
        grid=(b, n_tiles),
        in_specs=[
            pl.BlockSpec((1, TOKEN_TILE, D_MODEL), lambda bi, i: (bi, i, 0)),
            pl.BlockSpec((1, D_MODEL), vec),
            pl.BlockSpec((1, D_MODEL), vec),
            *_ffn_weight_specs(layer),
            pl.BlockSpec((1, D_MODEL), vec),
            pl.BlockSpec((IN_WIDTH, D_MODEL), vec, pipeline_mode=RESIDENT),
            pl.BlockSpec((HEAD_DIM, 1), vec),
            pl.BlockSpec((HEAD_DIM, 1), vec),
            pl.BlockSpec((HALF_DIM, TOKEN_TILE), lambda bi, i: (0, i)),
            pl.BlockSpec((HALF_DIM, TOKEN_TILE), lambda bi, i: (0, i)),
        ],
        out_specs=[
            pl.BlockSpec((1, TOKEN_TILE, D_MODEL), lambda bi, i: (bi, i, 0)),
            pl.BlockSpec((1, GROUP_WIDTH, TOKEN_TILE), lambda bi, i: (bi, 0, i)),
            pl.BlockSpec((1, GROUP_WIDTH, TOKEN_TILE), lambda bi, i: (bi, 0, i)),
            pl.BlockSpec((1, 2 * KV_HEADS, TOKEN_TILE, HEAD_DIM), lambda bi, i: (bi, 0, i, 0)),
            pl.BlockSpec((1, 1, 2 * KV_WIDTH, V_CHUNK), lambda bi, i: (bi, i, 0, 0)),
            pl.BlockSpec((1, 2 * Q_HEADS, 1, TOKEN_TILE), lambda bi, i: (bi, 0, 0, i)),
            pl.BlockSpec((1, 2 * KV_HEADS, 1, TOKEN_TILE), lambda bi, i: (bi, 0, 0, i)),
        ],
        out_shape=[
            jax.ShapeDtypeStruct((b, s, D_MODEL), jnp.float32),
            jax.ShapeDtypeStruct((b, GROUP_WIDTH, s), jnp.bfloat16),
            jax.ShapeDtypeStruct((b, GROUP_WIDTH, s), jnp.bfloat16),
            jax.ShapeDtypeStruct((b, 2 * KV_HEADS, s, HEAD_DIM), jnp.bfloat16),
            jax.ShapeDtypeStruct((b, n_tiles, 2 * KV_WIDTH, V_CHUNK), jnp.bfloat16),
            jax.ShapeDtypeStruct((b, 2 * Q_HEADS, 1, s), jnp.float32),
            jax.ShapeDtypeStruct((b, 2 * KV_HEADS, 1, s), jnp.float32),
        ],
        scratch_shapes=[
            pltpu.VMEM((TOKEN_TILE, D_MODEL), jnp.bfloat16),
            pltpu.VMEM((TOKEN_TILE, D_MODEL), jnp.float32),
        ],
        compiler_params=pltpu.CompilerParams(
            dimension_semantics=("arbitrary", "arbitrary"), vmem_limit_bytes=VMEM_LIMIT),
    )(x, pre1, post1, wg, wu, wd, pre, w_in_t, gq, gk, cos_t, sin_t)


def _shift_stats(stat_ref, qn_all_ref, kn_ref):
    @pl.when(pl.program_id(2) == 0)
    def _():
        kmax2 = jnp.max(kn_ref[0, 0], axis=1, keepdims=True)
        qmax2 = jnp.max(jnp.max(qn_all_ref[0], axis=0), axis=1, keepdims=True)
        stat_ref[0] = jnp.max(kmax2)
        stat_ref[1] = jnp.max(BOUND_SLACK * jnp.sqrt(qmax2 * kmax2))
    return stat_ref[0], stat_ref[1] <= MAX_FIXED_SHIFT


def _global_kernel(q_ref, k_ref, v_ref, qn_ref, qn_all_ref, kn_ref, o_ref, s_ref, smax_ref, p_ref,
                   alpha_ref, m_ref, c_ref, l_ref, acc_ref, stat_ref, *, n_chunks):
    ones = jnp.ones((ONES_ROWS, V_CHUNK), jnp.bfloat16)
    n_blocks = V_CHUNK // ROW_BLOCK
    acc_ref[...] = jnp.zeros_like(acc_ref)

    kmax2, use_fixed_shift = _shift_stats(stat_ref, qn_all_ref, kn_ref)
    c_ref[...] = BOUND_SLACK * jnp.sqrt(qn_ref[0] * kmax2)

    def k_rows(chunk, r):
        start = chunk * V_CHUNK + r * ROW_BLOCK
        if not isinstance(start, int):
            start = pl.multiple_of(start, ROW_BLOCK)
        return k_ref[0, 0, pl.ds(start, ROW_BLOCK), :]

    def v_ext(chunk):
        return jnp.concatenate([v_ref[0, chunk], ones], axis=0)

    def values_block(vext, slot, h, r):
        cols = slice((r - 1) * ROW_BLOCK, (r + 1) * ROW_BLOCK)
        return jnp.dot(vext[:, cols], p_ref[slot, h, cols, :], preferred_element_type=jnp.float32)

    def fixed_step(j, parity, do_scores=True, do_values=True):
        cur, other = parity, 1 - parity
        if do_values:
            vt = v_ref[0, j]
        after_values = None
        for h in range(HEADS_PER_KV):
            q_h = q_ref[0, h * HEAD_DIM:(h + 1) * HEAD_DIM, :]
            c_h = c_ref[h]
            pv = None
            row_sum = None
            for r in range(n_blocks):
                rows = slice(r * ROW_BLOCK, (r + 1) * ROW_BLOCK)
                if do_scores:
                    st = jnp.dot(k_rows(j + 1, r), q_h, preferred_element_type=jnp.float32)
                    shift = c_h if after_values is None else c_h + after_values
                    p = jnp.exp2(st - shift)
                    p_ref[other, h, rows, :] = p.astype(jnp.bfloat16)
                    part = jnp.sum(p.reshape(ROW_BLOCK // 8, 8, Q_TILE), axis=0)
                    row_sum = part if row_sum is None else row_sum + part
                if do_values and r % 2 == 1:
                    d = values_block(vt, cur, h, r)
                    pv = d if pv is None else pv + d
                    after_values = _zero_after(d[:1])
            if do_scores:
                l_ref[h] += row_sum
            if do_values:
                acc_ref[h, :HEAD_DIM, :] += pv

    def online_step(j, parity, do_scores=True, do_softmax=True, do_values=True):
        cur, other = parity, 1 - parity
        if do_values:
            vext = v_ext(j - 1)
        for h in range(HEADS_PER_KV):
            q_h = q_ref[0, h * HEAD_DIM:(h + 1) * HEAD_DIM, :]
            if do_softmax:
                m_old = m_ref[h]
                m_new = jnp.maximum(m_old, smax_ref[cur, h])
                alpha_ref[cur, h] = jnp.exp2(m_old - m_new)
                m_ref[h] = m_new
            blk_max = None
            pv = None
            for r in range(n_blocks):
                rows = slice(r * ROW_BLOCK, (r + 1) * ROW_BLOCK)
                if do_scores:
                    st = jnp.dot(k_rows(j + 1, r), q_h, preferred_element_type=jnp.float32)
                    s_ref[other, h, rows, :] = st
                    part = jnp.max(st.reshape(ROW_BLOCK // 8, 8, Q_TILE), axis=0)
                    blk_max = part if blk_max is None else jnp.maximum(blk_max, part)
                if do_softmax:
                    p_ref[cur, h, rows, :] = jnp.exp2(s_ref[cur, h, rows, :] - m_new).astype(
                        jnp.bfloat16)
                if do_values and r % 2 == 1:
                    d = values_block(vext, other, h, r)
                    pv = d if pv is None else pv + d
            if do_scores:
                smax_ref[other, h] = jnp.max(blk_max, axis=0, keepdims=True)
            if do_values:
                acc_ref[h] = alpha_ref[other, h] * acc_ref[h] + pv

    @pl.when(use_fixed_shift)
    def _():
        l_ref[...] = jnp.zeros_like(l_ref)
        fixed_step(-1, 1, do_values=False)

        def steps(t, carry):
            for u in range(FIXED_UNROLL):
                fixed_step(FIXED_UNROLL * t + u, u % 2)
            return carry

        n_loop = (n_chunks - 2) // FIXED_UNROLL
        lax.fori_loop(0, n_loop, steps, 0)
        for j in range(n_loop * FIXED_UNROLL, n_chunks - 2):
            fixed_step(j, j % 2)
        fixed_step(n_chunks - 2, 0)
        fixed_step(n_chunks - 1, 1, do_scores=False)
        for h in range(HEADS_PER_KV):
            acc_ref[h, HEAD_DIM:HEAD_DIM + 1, :] = jnp.sum(l_ref[h], axis=0, keepdims=True)

    @pl.when(jnp.logical_not(use_fixed_shift))
    def _():
        m_ref[...] = jnp.full_like(m_ref, NEG_BIG)
        online_step(-1, 1, do_softmax=False, do_values=False)
        online_step(0, 0, do_values=False)

        def pair(t, carry):
            online_step(2 * t + 1, 1)
            online_step(2 * t + 2, 0)
            return carry

        lax.fori_loop(0, (n_chunks - 2) // 2, pair, 0)
        online_step(n_chunks - 1, 1, do_scores=False)
        online_step(n_chunks, 0, do_scores=False, do_softmax=False)

    for h in range(HEADS_PER_KV):
        a = acc_ref[h]
        o_ref[0, h * HEAD_DIM:(h + 1) * HEAD_DIM, :] = a[:HEAD_DIM] / a[HEAD_DIM:HEAD_DIM + 1]


def _global_attention(q_t, k, v_t, q_norm2, k_norm2):
    b, _, s = q_t.shape
    n_chunks = s // V_CHUNK
    assert n_chunks >= 2 and n_chunks % 2 == 0, "the kv pipeline runs chunks in pairs"
    return pl.pallas_call(
        functools.partial(_global_kernel, n_chunks=n_chunks),
        name="global_attn",
        grid=(b, KV_HEADS, s // Q_TILE),
        in_specs=[
            pl.BlockSpec((1, Q_PER_KV_ROWS, Q_TILE), lambda bi, g, i: (bi, g, i)),
            pl.BlockSpec((1, 1, s, HEAD_DIM), lambda bi, g, i: (bi, g, 0, 0)),
            pl.BlockSpec((1, n_chunks, HEAD_DIM, V_CHUNK), lambda bi, g, i: (bi, 0, g, 0)),
            pl.BlockSpec((1, HEADS_PER_KV, 1, Q_TILE), lambda bi, g, i: (bi, g, 0, i)),
            pl.BlockSpec((1, HEADS_PER_KV, 1, s), lambda bi, g, i: (bi, g, 0, 0)),
            pl.BlockSpec((1, 1, 1, s), lambda bi, g, i: (bi, g, 0, 0)),
        ],
        out_specs=pl.BlockSpec((1, Q_PER_KV_ROWS, Q_TILE), lambda bi, g, i: (bi, g, i)),
        out_shape=jax.ShapeDtypeStruct((b, GROUP_WIDTH, s), jnp.float32),
        scratch_shapes=[
            pltpu.VMEM((2, HEADS_PER_KV, V_CHUNK, Q_TILE), jnp.float32),
            pltpu.VMEM((2, HEADS_PER_KV, 1, Q_TILE), jnp.float32),
            pltpu.VMEM((2, HEADS_PER_KV, V_CHUNK, Q_TILE), jnp.bfloat16),
            pltpu.VMEM((2, HEADS_PER_KV, 1, Q_TILE), jnp.float32),
            pltpu.VMEM((HEADS_PER_KV, 1, Q_TILE), jnp.float32),
            pltpu.VMEM((HEADS_PER_KV, 1, Q_TILE), jnp.float32),
            pltpu.VMEM((HEADS_PER_KV, 8, Q_TILE), jnp.float32),
            pltpu.VMEM((HEADS_PER_KV, HEAD_DIM + ONES_ROWS, Q_TILE), jnp.float32),
            pltpu.SMEM((2,), jnp.float32),
        ],
        compiler_params=pltpu.CompilerParams(
            dimension_semantics=("arbitrary", "arbitrary", "arbitrary"),
            vmem_limit_bytes=VMEM_LIMIT),
    )(q_t, k, v_t, q_norm2, q_norm2, k_norm2)


def _window_kernel(q_ref, k_ref, v_ref, bias_ref, sink_ref, qn2_ref, qn2_all_ref, kn2_ref,
                   o_ref, s_ref, mx_ref, p_ref, stat_ref, *, seq_len):
    i = pl.program_id(2)
    has_prev = i > 0
    has_next = i < pl.num_programs(2) - 1
    sink = sink_ref[0]
    no_prev = jnp.where(has_prev, 0.0, NEG_BIG)
    no_next = jnp.where(has_next, 0.0, NEG_BIG)
    blocks = WQ_TILE // WINDOW
    chunks_per_tile = WQ_TILE // V_CHUNK
    blocks_per_chunk = V_CHUNK // WINDOW
    last_chunk = seq_len // V_CHUNK - 1

    def k_block(blk):
        start = jnp.clip(i * WQ_TILE + blk * WINDOW, 0, seq_len - WINDOW)
        return k_ref[0, 0, pl.ds(pl.multiple_of(start, WINDOW), WINDOW), :]

    def v_block(blk):
        first = i * chunks_per_tile
        if blk < 0:
            return v_ref[0, jnp.maximum(first - 1, 0), :, V_CHUNK - WINDOW:]
        if blk >= blocks:
            return v_ref[0, jnp.minimum(first + chunks_per_tile, last_chunk), :, :WINDOW]
        lo = (blk % blocks_per_chunk) * WINDOW
        return v_ref[0, first + blk // blocks_per_chunk, :, lo:lo + WINDOW]

    def band(sb, block_fn, axis):
        return jnp.concatenate([block_fn(sb - 1 + r) for r in range(3)], axis=axis)

    def queries(sb):
        lo = sb * WINDOW
        return jnp.concatenate(
            [q_ref[0, h * HEAD_DIM:(h + 1) * HEAD_DIM, lo:lo + WINDOW] for h in range(HEADS_PER_KV)],
            axis=1)

    def edge_penalty(sb, r):
        if sb == 0 and r == 0:
            return no_prev
        if sb == N_SUB - 1 and r == 2:
            return no_next
        return None

    def store_out(sb, ot):
        lo = sb * WINDOW
        for h in range(HEADS_PER_KV):
            o_ref[0, h * HEAD_DIM:(h + 1) * HEAD_DIM, lo:lo + WINDOW] = (
                ot[:, h * WINDOW:(h + 1) * WINDOW])

    kmax2, use_fixed_shift = _shift_stats(stat_ref, qn2_all_ref, kn2_ref)
    c_all = BOUND_SLACK * jnp.sqrt(qn2_ref[0] * kmax2)

    @pl.when(use_fixed_shift)
    def _():
        def shift_of(sb):
            lo = sb * WINDOW
            c = jnp.concatenate([c_all[h, :, lo:lo + WINDOW] for h in range(HEADS_PER_KV)], axis=1)
            return jnp.maximum(c, sink)

        def values_part(sb, k_lo, k_hi):
            vs = jnp.concatenate(
                [v_block(sb - 1 + r) for r in range(k_lo // WINDOW, k_hi // WINDOW)], axis=1)
            return jnp.dot(vs, p_ref[sb, k_lo:k_hi, :], preferred_element_type=jnp.float32)

        def finish(sb, ot, row_sum, shift):
            denom = jnp.sum(row_sum, axis=0, keepdims=True) + jnp.exp2(sink - shift)
            store_out(sb, ot / denom)

        prev = None
        for sb in range(N_SUB + 1):
            scoring = sb < N_SUB
            if scoring:
                qs = queries(sb)
                shift = shift_of(sb)
                row_sum = None
            after_values = None
            ot = None
            for r in range(3):
                if scoring:
                    rows = slice(r * WINDOW, (r + 1) * WINDOW)
                    ks = k_block(sb - 1 + r)
                    st = jnp.dot(ks, qs, preferred_element_type=jnp.float32) + bias_ref[0, rows, :]
                    pen = edge_penalty(sb, r)
                    sh = shift if pen is None else shift - pen
                    if after_values is not None:
                        sh = sh + after_values
                    p = jnp.exp2(st - sh)
                    p_ref[sb, rows, :] = p.astype(jnp.bfloat16)
                    part = jnp.sum(p.reshape(WINDOW // 8, 8, HEADS_PER_KV * WINDOW), axis=0)
                    row_sum = part if row_sum is None else row_sum + part
                if prev is not None and r < 2:
                    d = values_part(prev[0], 0, 2 * WINDOW) if r == 0 else values_part(
                        prev[0], 2 * WINDOW, BAND)
                    ot = d if ot is None else ot + d
                    after_values = _zero_after(d[:1])
            if prev is not None:
                finish(prev[0], ot, prev[1], prev[2])
            prev = (sb, row_sum, shift) if scoring else None

    @pl.when(jnp.logical_not(use_fixed_shift))
    def _():
        ones = jnp.ones((ONES_ROWS, BAND), jnp.bfloat16)

        def scores(sb):
            ks = band(sb, k_block, 0)
            st = jnp.dot(ks, queries(sb), preferred_element_type=jnp.float32) + bias_ref[0]
            if sb == 0:
                st = jnp.concatenate([st[:WINDOW] + no_prev, st[WINDOW:]], axis=0)
            if sb == N_SUB - 1:
                st = jnp.concatenate([st[:2 * WINDOW], st[2 * WINDOW:] + no_next], axis=0)
            s_ref[sb] = st
            mx_ref[sb] = jnp.maximum(jnp.max(st, axis=0, keepdims=True), sink)

        def probabilities(sb):
            p_ref[sb] = jnp.exp2(s_ref[sb] - mx_ref[sb]).astype(jnp.bfloat16)

        def values(sb):
            vext = jnp.concatenate([band(sb, v_block, 1), ones], axis=0)
            ot = jnp.dot(vext, p_ref[sb], preferred_element_type=jnp.float32)
            denom = ot[HEAD_DIM:HEAD_DIM + 1] + jnp.exp2(sink - mx_ref[sb])
            store_out(sb, ot[:HEAD_DIM] / denom)

        for t in range(N_SUB + 2):
            if t < N_SUB:
                scores(t)
            if 1 <= t <= N_SUB:
                probabilities(t - 1)
            if t >= 2:
                values(t - 2)


def _window_attention(q_t, k, v_t, bias, sink, q_norm2, k_norm2):
    b, _, s = q_t.shape
    assert WQ_TILE % V_CHUNK == 0, "a query tile's own keys are whole v^T chunks"
    n_chunks = s // V_CHUNK
    kh = lambda g: KV_HEADS + g
    return pl.pallas_call(
        functools.partial(_window_kernel, seq_len=s),
        name="window_attn",
        grid=(b, KV_HEADS, s // WQ_TILE),
        in_specs=[
            pl.BlockSpec((1, Q_PER_KV_ROWS, WQ_TILE), lambda bi, g, i: (bi, g, i)),
            pl.BlockSpec((1, 1, s, HEAD_DIM), lambda bi, g, i: (bi, kh(g), 0, 0)),
            pl.BlockSpec((1, n_chunks, HEAD_DIM, V_CHUNK), lambda bi, g, i: (bi, 0, kh(g), 0)),
            pl.BlockSpec((1, BAND, HEADS_PER_KV * WINDOW), lambda bi, g, i: (g, 0, 0)),
            pl.BlockSpec((1, 1, HEADS_PER_KV * WINDOW), lambda bi, g, i: (g, 0, 0)),
            pl.BlockSpec((1, HEADS_PER_KV, 1, WQ_TILE), lambda bi, g, i: (bi, kh(g), 0, i)),
            pl.BlockSpec((1, HEADS_PER_KV, 1, s), lambda bi, g, i: (bi, kh(g), 0, 0)),
            pl.BlockSpec((1, 1, 1, s), lambda bi, g, i: (bi, kh(g), 0, 0)),
        ],
        out_specs=pl.BlockSpec((1, Q_PER_KV_ROWS, WQ_TILE), lambda bi, g, i: (bi, g, i)),
        out_shape=jax.ShapeDtypeStruct((b, GROUP_WIDTH, s), jnp.float32),
        scratch_shapes=[
            pltpu.VMEM((N_SUB, BAND, HEADS_PER_KV * WINDOW), jnp.float32),
            pltpu.VMEM((N_SUB, 1, HEADS_PER_KV * WINDOW), jnp.float32),
            pltpu.VMEM((N_SUB, BAND, HEADS_PER_KV * WINDOW), jnp.bfloat16),
            pltpu.SMEM((2,), jnp.float32),
        ],
        compiler_params=pltpu.CompilerParams(
            dimension_semantics=("arbitrary", "arbitrary", "arbitrary"),
            vmem_limit_bytes=VMEM_LIMIT),
    )(q_t, k, v_t, bias, sink, q_norm2, q_norm2, k_norm2)


def _mix_out_ffn_kernel(x_ref, oa_ref, ob_ref, ga_ref, gb_ref, w_ref, post_ref,
                        pre2_ref, post2_ref, wg_ref, wu_ref, wd_ref, o_ref, xn_ref, acc_ref):
    na = _rms_cols(oa_ref[0], ga_ref[...])
    nb = _rms_cols(ob_ref[0], gb_ref[...])
    mt = jnp.concatenate([na, nb], axis=0).astype(jnp.bfloat16)
    h = lax.dot_general(mt, w_ref[...], (((0,), (0,)), ((), ())),
                        preferred_element_type=jnp.float32)
    x1 = x_ref[0] + _rms_rows(h, post_ref[...])
    o_ref[0] = x1
    o_ref[0] += _ffn_branch(x1, pre2_ref, post2_ref, wg_ref, wu_ref, wd_ref, xn_ref, acc_ref)


def _mix_out_ffn(x, oa_t, ob_t, ga, gb, w_out, post, pre2, post2, wg, wu, wd, layer):
    b, s, _ = x.shape
    vec = lambda bi, i: (0, 0)
    return pl.pallas_call(
        _mix_out_ffn_kernel,
        name="mix_out_ffn",
        grid=(b, s // TOKEN_TILE),
        in_specs=[
            pl.BlockSpec((1, TOKEN_TILE, D_MODEL), lambda bi, i: (bi, i, 0)),
            pl.BlockSpec((1, GROUP_WIDTH, TOKEN_TILE), lambda bi, i: (bi, 0, i)),
            pl.BlockSpec((1, GROUP_WIDTH, TOKEN_TILE), lambda bi, i: (bi, 0, i)),
            pl.BlockSpec((GROUP_WIDTH, 1), vec),
            pl.BlockSpec((GROUP_WIDTH, 1), vec),
            pl.BlockSpec((D_MODEL, D_MODEL), vec, pipeline_mode=RESIDENT),
            pl.BlockSpec((1, D_MODEL), vec),
            pl.BlockSpec((1, D_MODEL), vec),
            pl.BlockSpec((1, D_MODEL), vec),
            *_ffn_weight_specs(layer),
        ],
        out_specs=pl.BlockSpec((1, TOKEN_TILE, D_MODEL), lambda bi, i: (bi, i, 0)),
        out_shape=jax.ShapeDtypeStruct((b, s, D_MODEL), jnp.float32),
        scratch_shapes=[
            pltpu.VMEM((TOKEN_TILE, D_MODEL), jnp.bfloat16),
            pltpu.VMEM((TOKEN_TILE, D_MODEL), jnp.float32),
        ],
        compiler_params=pltpu.CompilerParams(
            dimension_semantics=("arbitrary", "arbitrary"), vmem_limit_bytes=VMEM_LIMIT),
    )(x, oa_t, ob_t, ga, gb, w_out, post, pre2, post2, wg, wu, wd)


_DEINTERLEAVE = np.concatenate([np.arange(0, HEAD_DIM, 2), np.arange(1, HEAD_DIM, 2)])


def _w_in_rows():
    n_rot = (Q_HEADS + KV_HEADS) * HEAD_DIM
    rot = (np.arange(Q_HEADS + KV_HEADS)[:, None] * HEAD_DIM + _DEINTERLEAVE[None, :]).reshape(-1)
    return np.concatenate([rot, np.arange(n_rot, IN_WIDTH)])


def _rope_tables_t(seq_len):
    n_rows = seq_len // GRID_W
    row = jnp.repeat(jnp.arange(n_rows, dtype=jnp.float32), GRID_W)
    col = jnp.tile(jnp.arange(GRID_W, dtype=jnp.float32), n_rows)
    n_freq = HEAD_DIM // 4
    inv_freq = ROPE_THETA ** (-jnp.arange(n_freq, dtype=jnp.float32) / n_freq)
    ang = jnp.concatenate([inv_freq[:, None] * row[None, :], inv_freq[:, None] * col[None, :]], axis=0)
    return jnp.cos(ang), jnp.sin(ang)


def _window_bias():
    slopes = 2.0 ** (-8.0 * np.arange(1, Q_HEADS + 1, dtype=np.float64) / Q_HEADS)
    kpos = np.arange(BAND)[:, None] - WINDOW
    qpos = np.arange(WINDOW)[None, :]
    dist = np.abs(qpos - kpos).astype(np.float64)
    per_head = np.where(dist[None] <= WINDOW, -slopes[:, None, None] * dist[None] * LOG2E, NEG_BIG)
    per_kv = per_head.reshape(KV_HEADS, HEADS_PER_KV, BAND, WINDOW).transpose(0, 2, 1, 3)
    return jnp.asarray(per_kv.reshape(KV_HEADS, BAND, HEADS_PER_KV * WINDOW), jnp.float32)


def _trunk(x, layers, bias):
    b, s, _ = x.shape
    cos_t, sin_t = _rope_tables_t(s)
    for p in layers:
        x, qa_t, qb_t, k, v_t, q_norm2, k_norm2 = _ffn_mix_in(
            x, p["ffn1_pre"], p["ffn1_post"], *p["ffn1_w"], p["layer"],
            p["mix_pre"], p["w_in_t"], p["gq"], p["gk"], cos_t, sin_t)
        oa_t = _global_attention(qa_t, k, v_t, q_norm2, k_norm2)
        ob_t = _window_attention(qb_t, k, v_t, bias, p["sink"], q_norm2, k_norm2)
        x = _mix_out_ffn(x, oa_t, ob_t, p["ga"], p["gb"], p["w_out"], p["mix_post"],
                         p["ffn2_pre"], p["ffn2_post"], *p["ffn2_w"], p["layer"])
    return x


def kernel(x_prompt, x_sample, ffn1_pre, ffn1_post, ffn1_w_gate, ffn1_w_up, ffn1_w_down, mix_pre, mix_post, w_in, a_q_norm, a_k_norm, b_sink, a_out_norm, b_out_norm, w_out, ffn2_pre, ffn2_post, ffn2_w_gate, ffn2_w_up, ffn2_w_down):
    depth = w_in.shape[0]
    rows = _w_in_rows()
    layers = []
    ffn1_w = tuple(w.astype(jnp.bfloat16) for w in (ffn1_w_gate, ffn1_w_up, ffn1_w_down))
    ffn2_w = tuple(w.astype(jnp.bfloat16) for w in (ffn2_w_gate, ffn2_w_up, ffn2_w_down))
    for l in range(depth):
        sink = jnp.repeat(b_sink[l].astype(jnp.float32) * LOG2E, WINDOW)
        layers.append(dict(
            layer=l, ffn1_w=ffn1_w, ffn2_w=ffn2_w,
            ffn1_pre=ffn1_pre[l][None, :], ffn1_post=ffn1_post[l][None, :],
            ffn2_pre=ffn2_pre[l][None, :], ffn2_post=ffn2_post[l][None, :],
            mix_pre=mix_pre[l][None, :], mix_post=mix_post[l][None, :],
            w_in_t=w_in[l].T[rows].astype(jnp.bfloat16),
            gq=a_q_norm[l][_DEINTERLEAVE][:, None], gk=a_k_norm[l][_DEINTERLEAVE][:, None],
            sink=sink.reshape(KV_HEADS, 1, HEADS_PER_KV * WINDOW),
            ga=a_out_norm[l][:, None], gb=b_out_norm[l][:, None],
            w_out=w_out[l].astype(jnp.bfloat16),
        ))
    bias = _window_bias()
    return (_trunk(x_prompt, layers, bias), _trunk(x_sample, layers, bias))
```

```python
import functools
import math

import jax
import jax.numpy as jnp
import numpy as np
from jax import lax
from jax.experimental import pallas as pl
from jax.experimental.pallas import tpu as pltpu

D_MODEL = 1024
HEAD_DIM = 64
HALF_DIM = HEAD_DIM // 2
Q_HEADS = 8
KV_HEADS = 2
HEADS_PER_KV = Q_HEADS // KV_HEADS
GROUP_WIDTH = Q_HEADS * HEAD_DIM
KV_WIDTH = KV_HEADS * HEAD_DIM
Q_PER_KV_ROWS = HEADS_PER_KV * HEAD_DIM
GROUP_IN = GROUP_WIDTH + 2 * KV_WIDTH
IN_WIDTH = 2 * GROUP_IN
D_FF = 2816
GRID_W = 64
ROPE_THETA = 10000.0
WINDOW = 128
NORM_EPS = 1e-6
FFN_RESID = 0.5
LOG2E = math.log2(math.e)
Q_SCALE = HEAD_DIM ** -0.5 * LOG2E
NEG_BIG = -1e30
BOUND_SLACK = 1.01
MAX_FIXED_SHIFT = 48.0

FF_CHUNK = 256
N_FF_CHUNKS = D_FF // FF_CHUNK
TOKEN_TILE = 512
V_CHUNK = TOKEN_TILE
Q_TILE = 512
ONES_ROWS = 16
ROW_BLOCK = 128
FIXED_UNROLL = 10
WQ_TILE = 2048
N_SUB = WQ_TILE // WINDOW
BAND = 3 * WINDOW

VMEM_LIMIT = 56 * 1024 * 1024
RESIDENT = pl.Buffered(1)


def _rms_rows(x, gain_row):
    ms = jnp.mean(x * x, axis=-1, keepdims=True)
    return x * lax.rsqrt(ms + NORM_EPS) * gain_row


def _rms_cols(xt, gain_col):
    ms = jnp.mean(xt * xt, axis=0, keepdims=True)
    return xt * lax.rsqrt(ms + NORM_EPS) * gain_col


def _zero_after(x):
    bits = lax.bitcast_convert_type(x, jnp.uint32)
    bits = lax.shift_right_logical(lax.shift_right_logical(bits, jnp.uint32(16)), jnp.uint32(16))
    return bits.astype(jnp.float32)


def _ffn_branch(x, pre_ref, post_ref, wg_ref, wu_ref, wd_ref, xn_ref, acc_ref):
    xn_ref[...] = _rms_rows(x, pre_ref[...]).astype(jnp.bfloat16)
    acc_ref[...] = jnp.zeros_like(acc_ref)

    for c in range(N_FF_CHUNKS):
        cols = slice(c * FF_CHUNK, (c + 1) * FF_CHUNK)
        g = jnp.dot(xn_ref[...], wg_ref[:, cols], preferred_element_type=jnp.float32)
        u = jnp.dot(xn_ref[...], wu_ref[:, cols], preferred_element_type=jnp.float32)
        h = (g * jax.nn.sigmoid(g) * u).astype(jnp.bfloat16)
        acc_ref[...] += jnp.dot(h, wd_ref[cols, :], preferred_element_type=jnp.float32)
    return FFN_RESID * _rms_rows(acc_ref[...], post_ref[...])


def _rope_norm_head(xt, gain_col, cos_t, sin_t):
    y = _rms_cols(xt, gain_col)
    x0 = y[:HALF_DIM]
    x1 = y[HALF_DIM:]
    return jnp.concatenate([x0 * cos_t - x1 * sin_t, x0 * sin_t + x1 * cos_t], axis=0)


def _ffn_mix_in_kernel(x_ref, pre1_ref, post1_ref, wg_ref, wu_ref, wd_ref,
                       pre_ref, w_ref, gq_ref, gk_ref, cos_ref, sin_ref,
                       x1_ref, qa_ref, qb_ref, k_ref, v_ref, qn_ref, kn_ref, xn_ref, acc_ref):
    x = x_ref[0]
    x1 = x + _ffn_branch(x, pre1_ref, post1_ref, wg_ref, wu_ref, wd_ref, xn_ref, acc_ref)
    x1_ref[0] = x1
    xn = _rms_rows(x1, pre_ref[...]).astype(jnp.bfloat16)
    pt = lax.dot_general(w_ref[...], xn, (((1,), (1,)), ((), ())),
                         preferred_element_type=jnp.float32)
    pa = pt[:GROUP_IN]
    pb = pt[GROUP_IN:]
    cos_t = cos_ref[...]
    sin_t = sin_ref[...]
    gq = gq_ref[...]
    gk = gk_ref[...]
    off = 0
    for h in range(Q_HEADS):
        r = _rope_norm_head(pa[off:off + HEAD_DIM], gq, cos_t, sin_t) * Q_SCALE
        qa_ref[0, h * HEAD_DIM:(h + 1) * HEAD_DIM, :] = r.astype(jnp.bfloat16)
        qn_ref[0, h] = jnp.sum(r * r, axis=0, keepdims=True)
        off += HEAD_DIM
    for h in range(KV_HEADS):
        r = _rope_norm_head(pa[off:off + HEAD_DIM], gk, cos_t, sin_t)
        k_ref[0, h] = r.T.astype(jnp.bfloat16)
        kn_ref[0, h] = jnp.sum(r * r, axis=0, keepdims=True)
        off += HEAD_DIM
    v_ref[0, 0, :KV_WIDTH, :] = pa[off:off + KV_WIDTH].astype(jnp.bfloat16)
    for h in range(Q_HEADS):
        r = pb[h * HEAD_DIM:(h + 1) * HEAD_DIM] * Q_SCALE
        qb_ref[0, h * HEAD_DIM:(h + 1) * HEAD_DIM, :] = r.astype(jnp.bfloat16)
        qn_ref[0, Q_HEADS + h] = jnp.sum(r * r, axis=0, keepdims=True)
    off = GROUP_WIDTH
    for h in range(KV_HEADS):
        r = pb[off:off + HEAD_DIM]
        k_ref[0, KV_HEADS + h] = r.T.astype(jnp.bfloat16)
        kn_ref[0, KV_HEADS + h] = jnp.sum(r * r, axis=0, keepdims=True)
        off += HEAD_DIM
    v_ref[0, 0, KV_WIDTH:, :] = pb[off:off + KV_WIDTH].astype(jnp.bfloat16)


def _ffn_weight_specs(layer):
    pick = lambda bi, i: (layer, 0, 0)
    return [
        pl.BlockSpec((None, D_MODEL, D_FF), pick, pipeline_mode=RESIDENT),
        pl.BlockSpec((None, D_MODEL, D_FF), pick, pipeline_mode=RESIDENT),
        pl.BlockSpec((None, D_FF, D_MODEL), pick, pipeline_mode=RESIDENT),
    ]


def _ffn_mix_in(x, pre1, post1, wg, wu, wd, layer, pre, w_in_t, gq, gk, cos_t, sin_t):
    b, s, _ = x.shape
    n_tiles = s // TOKEN_TILE
    vec = lambda bi, i: (0, 0)
    return pl.pallas_call(
        _ffn_mix_in_kernel,
        name="ffn_mix_in",
        grid=(b, n_tiles),
        in_specs=[
            pl.BlockSpec((1, TOKEN_TILE, D_MODEL), lambda bi, i: (bi, i, 0)),
            pl.BlockSpec((1, D_MODEL), vec),
            pl.BlockSpec((1, D_MODEL), vec),
            *_ffn_weight_specs(layer),
            pl.BlockSpec((1, D_MODEL), vec),
            pl.BlockSpec((IN_WIDTH, D_MODEL), vec, pipeline_mode=RESIDENT),
            pl.BlockSpec((HEAD_DIM, 1), vec),
            pl.BlockSpec((HEAD_DIM, 1), vec),
            pl.BlockSpec((HALF_DIM, TOKEN_TILE), lambda bi, i: (0, i)),
            pl.BlockSpec((HALF_DIM, TOKEN_TILE), lambda bi, i: (0, i)),
        ],
        out_specs=[
            pl.BlockSpec((1, TOKEN_TILE, D_MODEL), lambda bi, i: (bi, i, 0)),
            pl.BlockSpec((1, GROUP_WIDTH, TOKEN_TILE), lambda bi, i: (bi, 0, i)),
            pl.BlockSpec((1, GROUP_WIDTH, TOKEN_TILE), lambda bi, i: (bi, 0, i)),
            pl.BlockSpec((1, 2 * KV_HEADS, TOKEN_TILE, HEAD_DIM), lambda bi, i: (bi, 0, i, 0)),
            pl.BlockSpec((1, 1, 2 * KV_WIDTH, V_CHUNK), lambda bi, i: (bi, i, 0, 0)),
            pl.BlockSpec((1, 2 * Q_HEADS, 1, TOKEN_TILE), lambda bi, i: (bi, 0, 0, i)),
            pl.BlockSpec((1, 2 * KV_HEADS, 1, TOKEN_TILE), lambda bi, i: (bi, 0, 0, i)),
        ],
        out_shape=[
            jax.ShapeDtypeStruct((b, s, D_MODEL), jnp.float32),
            jax.ShapeDtypeStruct((b, GROUP_WIDTH, s), jnp.bfloat16),
            jax.ShapeDtypeStruct((b, GROUP_WIDTH, s), jnp.bfloat16),
            jax.ShapeDtypeStruct((b, 2 * KV_HEADS, s, HEAD_DIM), jnp.bfloat16),
            jax.ShapeDtypeStruct((b, n_tiles, 2 * KV_WIDTH, V_CHUNK), jnp.bfloat16),
            jax.ShapeDtypeStruct((b, 2 * Q_HEADS, 1, s), jnp.float32),
            jax.ShapeDtypeStruct((b, 2 * KV_HEADS, 1, s), jnp.float32),
        ],
        scratch_shapes=[
            pltpu.VMEM((TOKEN_TILE, D_MODEL), jnp.bfloat16),
            pltpu.VMEM((TOKEN_TILE, D_MODEL), jnp.float32),
        ],
        compiler_params=pltpu.CompilerParams(
            dimension_semantics=("arbitrary", "arbitrary"), vmem_limit_bytes=VMEM_LIMIT),
    )(x, pre1, post1, wg, wu, wd, pre, w_in_t, gq, gk, cos_t, sin_t)


def _shift_stats(stat_ref, qn_all_ref, kn_ref):
    @pl.when(pl.program_id(2) == 0)
    def _():
        kmax2 = jnp.max(kn_ref[0, 0], axis=1, keepdims=True)
        qmax2 = jnp.max(jnp.max(qn_all_ref[0], axis=0), axis=1, keepdims=True)
        stat_ref[0] = jnp.max(kmax2)
        stat_ref[1] = jnp.max(BOUND_SLACK * jnp.sqrt(qmax2 * kmax2))
    return stat_ref[0], stat_ref[1] <= MAX_FIXED_SHIFT


def _global_kernel(q_ref, k_ref, v_ref, qn_ref, qn_all_ref, kn_ref, o_ref, s_ref, smax_ref, p_ref,
                   alpha_ref, m_ref, c_ref, l_ref, acc_ref, stat_ref, *, n_chunks):
    ones = jnp.ones((ONES_ROWS, V_CHUNK), jnp.bfloat16)
    n_blocks = V_CHUNK // ROW_BLOCK
    acc_ref[...] = jnp.zeros_like(acc_ref)

    kmax2, use_fixed_shift = _shift_stats(stat_ref, qn_all_ref, kn_ref)
    c_ref[...] = BOUND_SLACK * jnp.sqrt(qn_ref[0] * kmax2)

    def k_rows(chunk, r):
        start = chunk * V_CHUNK + r * ROW_BLOCK
        if not isinstance(start, int):
            start = pl.multiple_of(start, ROW_BLOCK)
        return k_ref[0, 0, pl.ds(start, ROW_BLOCK), :]

    def v_ext(chunk):
        return jnp.concatenate([v_ref[0, chunk], ones], axis=0)

    def values_block(vext, slot, h, r):
        cols = slice((r - 1) * ROW_BLOCK, (r + 1) * ROW_BLOCK)
        return jnp.dot(vext[:, cols], p_ref[slot, h, cols, :], preferred_element_type=jnp.float32)

    def fixed_step(j, parity, do_scores=True, do_values=True):
        cur, other = parity, 1 - parity
        if do_values:
            vt = v_ref[0, j]
        after_values = None
        for h in range(HEADS_PER_KV):
            q_h = q_ref[0, h * HEAD_DIM:(h + 1) * HEAD_DIM, :]
            c_h = c_ref[h]
            pv = None
            row_sum = None
            for r in range(n_blocks):
                rows = slice(r * ROW_BLOCK, (r + 1) * ROW_BLOCK)
                if do_scores:
                    st = jnp.dot(k_rows(j + 1, r), q_h, preferred_element_type=jnp.float32)
                    shift = c_h if after_values is None else c_h + after_values
                    p = jnp.exp2(st - shift)
                    p_ref[other, h, rows, :] = p.astype(jnp.bfloat16)
                    part = jnp.sum(p.reshape(ROW_BLOCK // 8, 8, Q_TILE), axis=0)
                    row_sum = part if row_sum is None else row_sum + part
                if do_values and r % 2 == 1:
                    d = values_block(vt, cur, h, r)
                    pv = d if pv is None else pv + d
                    after_values = _zero_after(d[:1])
            if do_scores:
                l_ref[h] += row_sum
            if do_values:
                acc_ref[h, :HEAD_DIM, :] += pv

    def online_step(j, parity, do_scores=True, do_softmax=True, do_values=True):
        cur, other = parity, 1 - parity
        if do_values:
            vext = v_ext(j - 1)
        for h in range(HEADS_PER_KV):
            q_h = q_ref[0, h * HEAD_DIM:(h + 1) * HEAD_DIM, :]
            if do_softmax:
                m_old = m_ref[h]
                m_new = jnp.maximum(m_old, smax_ref[cur, h])
                alpha_ref[cur, h] = jnp.exp2(m_old - m_new)
                m_ref[h] = m_new
            blk_max = None
            pv = None
            for r in range(n_blocks):
                rows = slice(r * ROW_BLOCK, (r + 1) * ROW_BLOCK)
                if do_scores:
                    st = jnp.dot(k_rows(j + 1, r), q_h, preferred_element_type=jnp.float32)
                    s_ref[other, h, rows, :] = st
                    part = jnp.max(st.reshape(ROW_BLOCK // 8, 8, Q_TILE), axis=0)
                    blk_max = part if blk_max is None else jnp.maximum(blk_max, part)
                if do_softmax:
                    p_ref[cur, h, rows, :] = jnp.exp2(s_ref[cur, h, rows, :] - m_new).astype(
                        jnp.bfloat16)
                if do_values and r % 2 == 1:
                    d = values_block(vext, other, h, r)
                    pv = d if pv is None else pv + d
            if do_scores:
                smax_ref[other, h] = jnp.max(blk_max, axis=0, keepdims=True)
            if do_values:
                acc_ref[h] = alpha_ref[other, h] * acc_ref[h] + pv

    @pl.when(use_fixed_shift)
    def _():
        l_ref[...] = jnp.zeros_like(l_ref)
        fixed_step(-1, 1, do_values=False)

        def steps(t, carry):
            for u in range(FIXED_UNROLL):
                fixed_step(FIXED_UNROLL * t + u, u % 2)
            return carry

        n_loop = (n_chunks - 2) // FIXED_UNROLL
        lax.fori_loop(0, n_loop, steps, 0)
        for j in range(n_loop * FIXED_UNROLL, n_chunks - 2):
            fixed_step(j, j % 2)
        fixed_step(n_chunks - 2, 0)
        fixed_step(n_chunks - 1, 1, do_scores=False)
        for h in range(HEADS_PER_KV):
            acc_ref[h, HEAD_DIM:HEAD_DIM + 1, :] = jnp.sum(l_ref[h], axis=0, keepdims=True)

    @pl.when(jnp.logical_not(use_fixed_shift))
    def _():
        m_ref[...] = jnp.full_like(m_ref, NEG_BIG)
        online_step(-1, 1, do_softmax=False, do_values=False)
        online_step(0, 0, do_values=False)

        def pair(t, carry):
            online_step(2 * t + 1, 1)
            online_step(2 * t + 2, 0)
            return carry

        lax.fori_loop(0, (n_chunks - 2) // 2, pair, 0)
        online_step(n_chunks - 1, 1, do_scores=False)
        online_step(n_chunks, 0, do_scores=False, do_softmax=False)

    for h in range(HEADS_PER_KV):
        a = acc_ref[h]
        o_ref[0, h * HEAD_DIM:(h + 1) * HEAD_DIM, :] = a[:HEAD_DIM] / a[HEAD_DIM:HEAD_DIM + 1]


def _global_attention(q_t, k, v_t, q_norm2, k_norm2):
    b, _, s = q_t.shape
    n_chunks = s // V_CHUNK
    assert n_chunks >= 2 and n_chunks % 2 == 0, "the kv pipeline runs chunks in pairs"
    return pl.pallas_call(
        functools.partial(_global_kernel, n_chunks=n_chunks),
        name="global_attn",
        grid=(b, KV_HEADS, s // Q_TILE),
        in_specs=[
            pl.BlockSpec((1, Q_PER_KV_ROWS, Q_TILE), lambda bi, g, i: (bi, g, i)),
            pl.BlockSpec((1, 1, s, HEAD_DIM), lambda bi, g, i: (bi, g, 0, 0)),
            pl.BlockSpec((1, n_chunks, HEAD_DIM, V_CHUNK), lambda bi, g, i: (bi, 0, g, 0)),
            pl.BlockSpec((1, HEADS_PER_KV, 1, Q_TILE), lambda bi, g, i: (bi, g, 0, i)),
            pl.BlockSpec((1, HEADS_PER_KV, 1, s), lambda bi, g, i: (bi, g, 0, 0)),
            pl.BlockSpec((1, 1, 1, s), lambda bi, g, i: (bi, g, 0, 0)),
        ],
        out_specs=pl.BlockSpec((1, Q_PER_KV_ROWS, Q_TILE), lambda bi, g, i: (bi, g, i)),
        out_shape=jax.ShapeDtypeStruct((b, GROUP_WIDTH, s), jnp.float32),
        scratch_shapes=[
            pltpu.VMEM((2, HEADS_PER_KV, V_CHUNK, Q_TILE), jnp.float32),
            pltpu.VMEM((2, HEADS_PER_KV, 1, Q_TILE), jnp.float32),
            pltpu.VMEM((2, HEADS_PER_KV, V_CHUNK, Q_TILE), jnp.bfloat16),
            pltpu.VMEM((2, HEADS_PER_KV, 1, Q_TILE), jnp.float32),
            pltpu.VMEM((HEADS_PER_KV, 1, Q_TILE), jnp.float32),
            pltpu.VMEM((HEADS_PER_KV, 1, Q_TILE), jnp.float32),
            pltpu.VMEM((HEADS_PER_KV, 8, Q_TILE), jnp.float32),
            pltpu.VMEM((HEADS_PER_KV, HEAD_DIM + ONES_ROWS, Q_TILE), jnp.float32),
            pltpu.SMEM((2,), jnp.float32),
        ],
        compiler_params=pltpu.CompilerParams(
            dimension_semantics=("arbitrary", "arbitrary", "arbitrary"),
            vmem_limit_bytes=VMEM_LIMIT),
    )(q_t, k, v_t, q_norm2, q_norm2, k_norm2)


def _window_kernel(q_ref, k_ref, v_ref, bias_ref, sink_ref, qn2_ref, qn2_all_ref, kn2_ref,
                   o_ref, s_ref, mx_ref, p_ref, stat_ref, *, seq_len):
    i = pl.program_id(2)
    has_prev = i > 0
    has_next = i < pl.num_programs(2) - 1
    sink = sink_ref[0]
    no_prev = jnp.where(has_prev, 0.0, NEG_BIG)
    no_next = jnp.where(has_next, 0.0, NEG_BIG)
    blocks = WQ_TILE // WINDOW
    chunks_per_tile = WQ_TILE // V_CHUNK
    blocks_per_chunk = V_CHUNK // WINDOW
    last_chunk = seq_len // V_CHUNK - 1

    def k_block(blk):
        start = jnp.clip(i * WQ_TILE + blk * WINDOW, 0, seq_len - WINDOW)
        return k_ref[0, 0, pl.ds(pl.multiple_of(start, WINDOW), WINDOW), :]

    def v_block(blk):
        first = i * chunks_per_tile
        if blk < 0:
            return v_ref[0, jnp.maximum(first - 1, 0), :, V_CHUNK - WINDOW:]
        if blk >= blocks:
            return v_ref[0, jnp.minimum(first + chunks_per_tile, last_chunk), :, :WINDOW]
        lo = (blk % blocks_per_chunk) * WINDOW
        return v_ref[0, first + blk // blocks_per_chunk, :, lo:lo + WINDOW]

    def band(sb, block_fn, axis):
        return jnp.concatenate([block_fn(sb - 1 + r) for r in range(3)], axis=axis)

    def queries(sb):
        lo = sb * WINDOW
        return jnp.concatenate(
            [q_ref[0, h * HEAD_DIM:(h + 1) * HEAD_DIM, lo:lo + WINDOW] for h in range(HEADS_PER_KV)],
            axis=1)

    def edge_penalty(sb, r):
        if sb == 0 and r == 0:
            return no_prev
        if sb == N_SUB - 1 and r == 2:
            return no_next
        return None

    def store_out(sb, ot):
        lo = sb * WINDOW
        for h in range(HEADS_PER_KV):
            o_ref[0, h * HEAD_DIM:(h + 1) * HEAD_DIM, lo:lo + WINDOW] = (
                ot[:, h * WINDOW:(h + 1) * WINDOW])

    kmax2, use_fixed_shift = _shift_stats(stat_ref, qn2_all_ref, kn2_ref)
    c_all = BOUND_SLACK * jnp.sqrt(qn2_ref[0] * kmax2)

    @pl.when(use_fixed_shift)
    def _():
        def shift_of(sb):
            lo = sb * WINDOW
            c = jnp.concatenate([c_all[h, :, lo:lo + WINDOW] for h in range(HEADS_PER_KV)], axis=1)
            return jnp.maximum(c, sink)

        def values_part(sb, k_lo, k_hi):
            vs = jnp.concatenate(
                [v_block(sb - 1 + r) for r in range(k_lo // WINDOW, k_hi // WINDOW)], axis=1)
            return jnp.dot(vs, p_ref[sb, k_lo:k_hi, :], preferred_element_type=jnp.float32)

        def finish(sb, ot, row_sum, shift):
            denom = jnp.sum(row_sum, axis=0, keepdims=True) + jnp.exp2(sink - shift)
            store_out(sb, ot / denom)

        prev = None
        for sb in range(N_SUB + 1):
            scoring = sb < N_SUB
            if scoring:
                qs = queries(sb)
                shift = shift_of(sb)
                row_sum = None
            after_values = None
            ot = None
            for r in range(3):
                if scoring:
                    rows = slice(r * WINDOW, (r + 1) * WINDOW)
                    ks = k_block(sb - 1 + r)
                    st = jnp.dot(ks, qs, preferred_element_type=jnp.float32) + bias_ref[0, rows, :]
                    pen = edge_penalty(sb, r)
                    sh = shift if pen is None else shift - pen
                    if after_values is not None:
                        sh = sh + after_values
                    p = jnp.exp2(st - sh)
                    p_ref[sb, rows, :] = p.astype(jnp.bfloat16)
                    part = jnp.sum(p.reshape(WINDOW // 8, 8, HEADS_PER_KV * WINDOW), axis=0)
                    row_sum = part if row_sum is None else row_sum + part
                if prev is not None and r < 2:
                    d = values_part(prev[0], 0, 2 * WINDOW) if r == 0 else values_part(
                        prev[0], 2 * WINDOW, BAND)
                    ot = d if ot is None else ot + d
                    after_values = _zero_after(d[:1])
            if prev is not None:
                finish(prev[0], ot, prev[1], prev[2])
            prev = (sb, row_sum, shift) if scoring else None

    @pl.when(jnp.logical_not(use_fixed_shift))
    def _():
        ones = jnp.ones((ONES_ROWS, BAND), jnp.bfloat16)

        def scores(sb):
            ks = band(sb, k_block, 0)
            st = jnp.dot(ks, queries(sb), preferred_element_type=jnp.float32) + bias_ref[0]
            if sb == 0:
                st = jnp.concatenate([st[:WINDOW] + no_prev, st[WINDOW:]], axis=0)
            if sb == N_SUB - 1:
                st = jnp.concatenate([st[:2 * WINDOW], st[2 * WINDOW:] + no_next], axis=0)
            s_ref[sb] = st
            mx_ref[sb] = jnp.maximum(jnp.max(st, axis=0, keepdims=True), sink)

        def probabilities(sb):
            p_ref[sb] = jnp.exp2(s_ref[sb] - mx_ref[sb]).astype(jnp.bfloat16)

        def values(sb):
            vext = jnp.concatenate([band(sb, v_block, 1), ones], axis=0)
            ot = jnp.dot(vext, p_ref[sb], preferred_element_type=jnp.float32)
            denom = ot[HEAD_DIM:HEAD_DIM + 1] + jnp.exp2(sink - mx_ref[sb])
            store_out(sb, ot[:HEAD_DIM] / denom)

        for t in range(N_SUB + 2):
            if t < N_SUB:
                scores(t)
            if 1 <= t <= N_SUB:
                probabilities(t - 1)
            if t >= 2:
                values(t - 2)


def _window_attention(q_t, k, v_t, bias, sink, q_norm2, k_norm2):
    b, _, s = q_t.shape
    assert WQ_TILE % V_CHUNK == 0, "a query tile's own keys are whole v^T chunks"
    n_chunks = s // V_CHUNK
    kh = lambda g: KV_HEADS + g
    return pl.pallas_call(
        functools.partial(_window_kernel, seq_len=s),
        name="window_attn",
        grid=(b, KV_HEADS, s // WQ_TILE),
        in_specs=[
            pl.BlockSpec((1, Q_PER_KV_ROWS, WQ_TILE), lambda bi, g, i: (bi, g, i)),
            pl.BlockSpec((1, 1, s, HEAD_DIM), lambda bi, g, i: (bi, kh(g), 0, 0)),
            pl.BlockSpec((1, n_chunks, HEAD_DIM, V_CHUNK), lambda bi, g, i: (bi, 0, kh(g), 0)),
            pl.BlockSpec((1, BAND, HEADS_PER_KV * WINDOW), lambda bi, g, i: (g, 0, 0)),
            pl.BlockSpec((1, 1, HEADS_PER_KV * WINDOW), lambda bi, g, i: (g, 0, 0)),
            pl.BlockSpec((1, HEADS_PER_KV, 1, WQ_TILE), lambda bi, g, i: (bi, kh(g), 0, i)),
            pl.BlockSpec((1, HEADS_PER_KV, 1, s), lambda bi, g, i: (bi, kh(g), 0, 0)),
            pl.BlockSpec((1, 1, 1, s), lambda bi, g, i: (bi, kh(g), 0, 0)),
        ],
        out_specs=pl.BlockSpec((1, Q_PER_KV_ROWS, WQ_TILE), lambda bi, g, i: (bi, g, i)),
        out_shape=jax.ShapeDtypeStruct((b, GROUP_WIDTH, s), jnp.float32),
        scratch_shapes=[
            pltpu.VMEM((N_SUB, BAND, HEADS_PER_KV * WINDOW), jnp.float32),
            pltpu.VMEM((N_SUB, 1, HEADS_PER_KV * WINDOW), jnp.float32),
            pltpu.VMEM((N_SUB, BAND, HEADS_PER_KV * WINDOW), jnp.bfloat16),
            pltpu.SMEM((2,), jnp.float32),
        ],
        compiler_params=pltpu.CompilerParams(
            dimension_semantics=("arbitrary", "arbitrary", "arbitrary"),
            vmem_limit_bytes=VMEM_LIMIT),
    )(q_t, k, v_t, bias, sink, q_norm2, q_norm2, k_norm2)


def _mix_out_ffn_kernel(x_ref, oa_ref, ob_ref, ga_ref, gb_ref, w_ref, post_ref,
                        pre2_ref, post2_ref, wg_ref, wu_ref, wd_ref, o_ref, xn_ref, acc_ref):
    na = _rms_cols(oa_ref[0], ga_ref[...])
    nb = _rms_cols(ob_ref[0], gb_ref[...])
    mt = jnp.concatenate([na, nb], axis=0).astype(jnp.bfloat16)
    h = lax.dot_general(mt, w_ref[...], (((0,), (0,)), ((), ())),
                        preferred_element_type=jnp.float32)
    x1 = x_ref[0] + _rms_rows(h, post_ref[...])
    o_ref[0] = x1
    o_ref[0] += _ffn_branch(x1, pre2_ref, post2_ref, wg_ref, wu_ref, wd_ref, xn_ref, acc_ref)


def _mix_out_ffn(x, oa_t, ob_t, ga, gb, w_out, post, pre2, post2, wg, wu, wd, layer):
    b, s, _ = x.shape
    vec = lambda bi, i: (0, 0)
    return pl.pallas_call(
        _mix_out_ffn_kernel,
        name="mix_out_ffn",
        grid=(b, s // TOKEN_TILE),
        in_specs=[
            pl.BlockSpec((1, TOKEN_TILE, D_MODEL), lambda bi, i: (bi, i, 0)),
            pl.BlockSpec((1, GROUP_WIDTH, TOKEN_TILE), lambda bi, i: (bi, 0, i)),
            pl.BlockSpec((1, GROUP_WIDTH, TOKEN_TILE), lambda bi, i: (bi, 0, i)),
            pl.BlockSpec((GROUP_WIDTH, 1), vec),
            pl.BlockSpec((GROUP_WIDTH, 1), vec),
            pl.BlockSpec((D_MODEL, D_MODEL), vec, pipeline_mode=RESIDENT),
            pl.BlockSpec((1, D_MODEL), vec),
            pl.BlockSpec((1, D_MODEL), vec),
            pl.BlockSpec((1, D_MODEL), vec),
            *_ffn_weight_specs(layer),
        ],
        out_specs=pl.BlockSpec((1, TOKEN_TILE, D_MODEL), lambda bi, i: (bi, i, 0)),
        out_shape=jax.ShapeDtypeStruct((b, s, D_MODEL), jnp.float32),
        scratch_shapes=[
            pltpu.VMEM((TOKEN_TILE, D_MODEL), jnp.bfloat16),
            pltpu.VMEM((TOKEN_TILE, D_MODEL), jnp.float32),
        ],
        compiler_params=pltpu.CompilerParams(
            dimension_semantics=("arbitrary", "arbitrary"), vmem_limit_bytes=VMEM_LIMIT),
    )(x, oa_t, ob_t, ga, gb, w_out, post, pre2, post2, wg, wu, wd)


_DEINTERLEAVE = np.concatenate([np.arange(0, HEAD_DIM, 2), np.arange(1, HEAD_DIM, 2)])


def _w_in_rows():
    n_rot = (Q_HEADS + KV_HEADS) * HEAD_DIM
    rot = (np.arange(Q_HEADS + KV_HEADS)[:, None] * HEAD_DIM + _DEINTERLEAVE[None, :]).reshape(-1)
    return np.concatenate([rot, np.arange(n_rot, IN_WIDTH)])


def _rope_tables_t(seq_len):
    n_rows = seq_len // GRID_W
    row = jnp.repeat(jnp.arange(n_rows, dtype=jnp.float32), GRID_W)
    col = jnp.tile(jnp.arange(GRID_W, dtype=jnp.float32), n_rows)
    n_freq = HEAD_DIM // 4
    inv_freq = ROPE_THETA ** (-jnp.arange(n_freq, dtype=jnp.float32) / n_freq)
    ang = jnp.concatenate([inv_freq[:, None] * row[None, :], inv_freq[:, None] * col[None, :]], axis=0)
    return jnp.cos(ang), jnp.sin(ang)


def _window_bias():
    slopes = 2.0 ** (-8.0 * np.arange(1, Q_HEADS + 1, dtype=np.float64) / Q_HEADS)
    kpos = np.arange(BAND)[:, None] - WINDOW
    qpos = np.arange(WINDOW)[None, :]
    dist = np.abs(qpos - kpos).astype(np.float64)
    per_head = np.where(dist[None] <= WINDOW, -slopes[:, None, None] * dist[None] * LOG2E, NEG_BIG)
    per_kv = per_head.reshape(KV_HEADS, HEADS_PER_KV, BAND, WINDOW).transpose(0, 2, 1, 3)
    return jnp.asarray(per_kv.reshape(KV_HEADS, BAND, HEADS_PER_KV * WINDOW), jnp.float32)


def _trunk(x, layers, bias):
    b, s, _ = x.shape
    cos_t, sin_t = _rope_tables_t(s)
    for p in layers:
        x, qa_t, qb_t, k, v_t, q_norm2, k_norm2 = _ffn_mix_in(
            x, p["ffn1_pre"], p["ffn1_post"], *p["ffn1_w"], p["layer"],
            p["mix_pre"], p["w_in_t"], p["gq"], p["gk"], cos_t, sin_t)
        oa_t = _global_attention(qa_t, k, v_t, q_norm2, k_norm2)
        ob_t = _window_attention(qb_t, k, v_t, bias, p["sink"], q_norm2, k_norm2)
        x = _mix_out_ffn(x, oa_t, ob_t, p["ga"], p["gb"], p["w_out"], p["mix_post"],
                         p["ffn2_pre"], p["ffn2_post"], *p["ffn2_w"], p["layer"])
    return x


def kernel(x_prompt, x_sample, ffn1_pre, ffn1_post, ffn1_w_gate, ffn1_w_up, ffn1_w_down, mix_pre, mix_post, w_in, a_q_norm, a_k_norm, b_sink, a_out_norm, b_out_norm, w_out, ffn2_pre, ffn2_post, ffn2_w_gate, ffn2_w_up, ffn2_w_down):
    depth = w_in.shape[0]
    rows = _w_in_rows()
    layers = []
    ffn1_w = tuple(w.astype(jnp.bfloat16) for w in (ffn1_w_gate, ffn1_w_up, ffn1_w_down))
    ffn2_w = tuple(w.astype(jnp.bfloat16) for w in (ffn2_w_gate, ffn2_w_up, ffn2_w_down))
    for l in range(depth):
        sink = jnp.repeat(b_sink[l].astype(jnp.float32) * LOG2E, WINDOW)
        layers.append(dict(
            layer=l, ffn1_w=ffn1_w, ffn2_w=ffn2_w,
            ffn1_pre=ffn1_pre[l][None, :], ffn1_post=ffn1_post[l][None, :],
            ffn2_pre=ffn2_pre[l][None, :], ffn2_post=ffn2_post[l][None, :],
            mix_pre=mix_pre[l][None, :], mix_post=mix_post[l][None, :],
            w_in_t=w_in[l].T[rows].astype(jnp.bfloat16),
            gq=a_q_norm[l][_DEINTERLEAVE][:, None], gk=a_k_norm[l][_DEINTERLEAVE][:, None],
            sink=sink.reshape(KV_HEADS, 1, HEADS_PER_KV * WINDOW),
            ga=a_out_norm[l][:, None], gb=b_out_norm[l][:, None],
            w_out=w_out[l].astype(jnp.bfloat16),
        ))
    bias = _window_bias()
    return (_trunk(x_prompt, layers, bias), _trunk(x_sample, layers, bias))
```
